```python
import math
import jax
import jax.numpy as jnp
from jax import lax
import numpy as np

D_MODEL = 1024
BATCH = 32
SEQ = 2048
DEPTH = 4

CTX_LEN = 256
GRID_W = 64

GDN_HEADS = 6
GDN_DK = 64
GDN_DV = 64
GDN_W = GDN_HEADS * GDN_DV
GDN_QKV = GDN_HEADS * (2 * GDN_DK + GDN_DV)
CONV_K = 5
CHUNK = 64
ATT_HEADS = 6
ATT_KV_HEADS = 2
ATT_DH = 64
ATT_W = ATT_HEADS * ATT_DH
ATT_GROUP = ATT_HEADS // ATT_KV_HEADS
Q_BLOCK = 128
ROPE_THETA = 10000.0
ROPE_PAIRS = ATT_DH // 4
S5_GROUPS = 16
S5_GH = 16
S5_P = 64
S5_W = S5_GROUPS * S5_GH
D_MIX = GDN_W + ATT_W + S5_W
D_FF = 2816
N_MOD = 9
ALPHA = (2.0 * DEPTH) ** 0.25
BETA_INIT = (8.0 * DEPTH) ** -0.25
EPS = 1e-6

OFF_GDN_Z = GDN_QKV
OFF_GDN_B = OFF_GDN_Z + GDN_W
OFF_GDN_A = OFF_GDN_B + 2 * GDN_HEADS
OFF_ATT_Q = OFF_GDN_A + 2 * GDN_HEADS
OFF_ATT_K = OFF_ATT_Q + ATT_W
OFF_ATT_V = OFF_ATT_K + ATT_KV_HEADS * ATT_DH
OFF_S5 = OFF_ATT_V + ATT_KV_HEADS * ATT_DH
IN_COLS = OFF_S5 + S5_W
IN_CUTS = (OFF_GDN_Z, OFF_GDN_B, OFF_GDN_A, OFF_ATT_Q, OFF_ATT_K, OFF_ATT_V, OFF_S5)

kernel_name = 'hybrid_gdn_gqa_s5_dit_layer'


def _layer_norm(x, g, b):
    xf = x.astype(jnp.float32)
    mu = jnp.mean(xf, axis=-1, keepdims=True)
    xc = xf - mu
    var = jnp.mean(xc * xc, axis=-1, keepdims=True)
    return (xc * lax.rsqrt(var + EPS) * g.astype(jnp.float32) + b.astype(jnp.float32)).astype(x.dtype)


def _rms_norm(x, w):
    xf = x.astype(jnp.float32)
    y = xf * lax.rsqrt(jnp.mean(xf * xf, axis=-1, keepdims=True) + EPS) * w.astype(jnp.float32)
    return y.astype(x.dtype)


def _l2_normalize(x):
    return x * lax.rsqrt(jnp.sum(x * x, axis=-1, keepdims=True) + EPS)


def _swiglu(h, w1, w3, w2):
    return (jax.nn.silu(h @ w1) * (h @ w3)) @ w2


def _post_norm_residual(x, y, gate, res_w, g, b):
    return _layer_norm(ALPHA * x + res_w * gate * y, g, b)


def _dwconv_centred(x, w):
    pad = CONV_K // 2
    return lax.conv_general_dilated(
        x, w[:, None, :].astype(x.dtype), window_strides=(1,), padding=[(pad, pad)],
        dimension_numbers=('NWC', 'WIO', 'NWC'), feature_group_count=x.shape[-1])


def _gated_delta_chunked(q, k, v, g, beta, s0):
    Bn, H, T, dk = q.shape
    dv = v.shape[-1]
    n = T // CHUNK
    q = (q * dk ** -0.5).reshape(Bn, H, n, CHUNK, dk)
    k = k.reshape(Bn, H, n, CHUNK, dk)
    v = v.reshape(Bn, H, n, CHUNK, dv)
    g = jnp.cumsum(g.reshape(Bn, H, n, CHUNK), axis=-1)
    beta = beta.reshape(Bn, H, n, CHUNK)[..., None]
    incl = jnp.tril(jnp.ones((CHUNK, CHUNK), dtype=bool))
    strict = jnp.tril(jnp.ones((CHUNK, CHUNK), dtype=bool), -1)
    diff = g[..., :, None] - g[..., None, :]
    decay = jnp.where(incl, jnp.exp(jnp.where(incl, diff, 0.0)), 0.0)
    kb = k * beta
    lower = jnp.where(strict, jnp.einsum('bhnid,bhnjd->bhnij', kb, k) * decay, 0.0)
    eye = jnp.eye(CHUNK, dtype=q.dtype)
    rhs = jnp.concatenate([v * beta, kb * jnp.exp(g)[..., None]], axis=-1)
    sol = lax.linalg.triangular_solve(eye + lower, rhs, left_side=True, lower=True, unit_diagonal=True)
    u, w = sol[..., :dv], sol[..., dv:]
    intra = jnp.where(incl, jnp.einsum('bhnid,bhnjd->bhnij', q, k) * decay, 0.0)

    def step(state, inp):
        q_i, k_i, u_i, w_i, g_i, a_i = inp
        v_new = u_i - jnp.einsum('bhcd,bhde->bhce', w_i, state)
        o_i = (jnp.einsum('bhcd,bhde->bhce', q_i * jnp.exp(g_i)[..., None], state)
               + jnp.einsum('bhij,bhje->bhie', a_i, v_new))
        g_last = g_i[..., -1:]
        state = (state * jnp.exp(g_last)[..., None]
                 + jnp.einsum('bhcd,bhce->bhde', k_i * jnp.exp(g_last - g_i)[..., None], v_new))
        return state, o_i

    xs = tuple(jnp.moveaxis(t, 2, 0) for t in (q, k, u, w, g, intra))
    s_final, o = lax.scan(step, s0, xs)
    return jnp.moveaxis(o, 0, 2).reshape(Bn, H, T, dv), s_final


def _gdn_inputs(qkv, b, a, conv_w, a_log, dt_bias):
    Bn, T, _ = qkv.shape
    qkv = jax.nn.silu(_dwconv_centred(qkv, conv_w))
    q, k, v = jnp.split(qkv, [GDN_HEADS * GDN_DK, 2 * GDN_HEADS * GDN_DK], axis=-1)

    def to_heads(t, d):
        return t.reshape(Bn, T, GDN_HEADS, d).transpose(0, 2, 1, 3).astype(jnp.float32)

    def dir_heads(t):
        return t.reshape(Bn, T, 2, GDN_HEADS).transpose(2, 0, 3, 1).astype(jnp.float32)

    q = _l2_normalize(to_heads(q, GDN_DK))
    k = _l2_normalize(to_heads(k, GDN_DK))
    v = to_heads(v, GDN_DV)
    beta = jax.nn.sigmoid(dir_heads(b))
    g = (-jnp.exp(a_log.astype(jnp.float32))[:, None, :, None]
         * jax.nn.softplus(dir_heads(a) + dt_bias.astype(jnp.float32)[:, None, :, None]))
    return q, k, v, g, beta


def _gdn_gated_out(o, z, norm_w):
    Bn, H, T, dv = o.shape
    o = _rms_norm(o.transpose(0, 2, 1, 3), norm_w)
    o = o * jax.nn.silu(z.reshape(Bn, T, H, dv).astype(jnp.float32))
    return o.reshape(Bn, T, H * dv).astype(z.dtype)


def _gdn_group(qkv, z, b, a, qkv_c, z_c, b_c, a_c, conv_w, a_log, dt_bias, norm_w, with_ctx_out):
    q, k, v, g, beta = _gdn_inputs(qkv, b, a, conv_w, a_log, dt_bias)
    qc, kc, vc, gc, betac = _gdn_inputs(qkv_c, b_c, a_c, conv_w, a_log, dt_bias)
    s0 = jnp.zeros((q.shape[0], GDN_HEADS, GDN_DK, GDN_DV), jnp.float32)

    def flip(t):
        return jnp.flip(t, axis=2)

    oc_f, sc_f = _gated_delta_chunked(qc, kc, vc, gc[0], betac[0], s0)
    o_f, _ = _gated_delta_chunked(q, k, v, g[0], beta[0], sc_f)
    oc_b, sc_b = _gated_delta_chunked(flip(qc), flip(kc), flip(vc), flip(gc[1]), flip(betac[1]), s0)
    o_b, _ = _gated_delta_chunked(flip(q), flip(k), flip(v), flip(g[1]), flip(beta[1]), sc_b)
    out = _gdn_gated_out(o_f + flip(o_b), z, norm_w)
    out_c = _gdn_gated_out(oc_f + flip(oc_b), z_c, norm_w) if with_ctx_out else None
    return out, out_c


def _axial_rope_tables(rows):
    row = jnp.repeat(jnp.arange(rows), GRID_W)
    col = jnp.tile(jnp.arange(GRID_W), rows)
    inv_freq = ROPE_THETA ** (-jnp.arange(ROPE_PAIRS, dtype=jnp.float32) / ROPE_PAIRS)
    ang = jnp.stack([row, col], axis=-1).astype(jnp.float32)[..., None] * inv_freq
    return jnp.cos(ang), jnp.sin(ang)


def _rope_2d(x, cos, sin):
    shp = x.shape
    xr = x.reshape(*shp[:-1], 2, 2, ROPE_PAIRS).astype(jnp.float32)
    x1, x2 = xr[..., 0, :], xr[..., 1, :]
    bshape = (shp[1],) + (1,) * (x.ndim - 3) + (2, ROPE_PAIRS)
    c, s = cos.reshape(bshape), sin.reshape(bshape)
    out = jnp.stack([x1 * c - x2 * s, x2 * c + x1 * s], axis=-2)
    return out.reshape(shp).astype(x.dtype)


def _attend(qb, keys, vals):
    s = jnp.einsum('bqhgd,bkhd->bhgqk', qb, keys).astype(jnp.float32) * (ATT_DH ** -0.5)
    p = jax.nn.softmax(s, axis=-1).astype(vals.dtype)
    return jnp.einsum('bhgqk,bkhd->bqhgd', p, vals)


def _attention_group(q, k, v, q_c, k_c, v_c, q_norm_w, k_norm_w, cos, sin, with_ctx_out):
    Bn, T, _ = q.shape
    Tc = q_c.shape[1]
    q = _rope_2d(_rms_norm(q.reshape(Bn, T, ATT_KV_HEADS, ATT_GROUP, ATT_DH), q_norm_w), cos, sin)
    k = _rope_2d(_rms_norm(k.reshape(Bn, T, ATT_KV_HEADS, ATT_DH), k_norm_w), cos, sin)
    v = v.reshape(Bn, T, ATT_KV_HEADS, ATT_DH)
    qc = _rms_norm(q_c.reshape(Bn, Tc, ATT_KV_HEADS, ATT_GROUP, ATT_DH), q_norm_w)
    kc = _rms_norm(k_c.reshape(Bn, Tc, ATT_KV_HEADS, ATT_DH), k_norm_w)
    vc = v_c.reshape(Bn, Tc, ATT_KV_HEADS, ATT_DH)
    keys = jnp.concatenate([kc, k], axis=1)
    vals = jnp.concatenate([vc, v], axis=1)
    nb = T // Q_BLOCK
    q_blocks = q.reshape(Bn, nb, Q_BLOCK, ATT_KV_HEADS, ATT_GROUP, ATT_DH).transpose(1, 0, 2, 3, 4, 5)
    o = lax.map(lambda qb: _attend(qb, keys, vals), q_blocks)
    o = o.transpose(1, 0, 2, 3, 4, 5).reshape(Bn, T, ATT_W)
    o_c = _attend(qc, kc, vc).reshape(Bn, Tc, ATT_W) if with_ctx_out else None
    return o, o_c


def _s5_discretize(lam_re, lam_im, log_dt, b_re, b_im):
    lam = lax.complex(lam_re.astype(jnp.float32), lam_im.astype(jnp.float32))
    dt = jnp.exp(log_dt.astype(jnp.float32))[:, None]
    lam_bar = jnp.exp(lam * dt)
    b = lax.complex(b_re.astype(jnp.float32), b_im.astype(jnp.float32))
    b_bar = ((lam_bar - 1.0) / lam)[..., None] * b
    return lam_bar, b_bar


def _s5_scan(u, lam_bar, b_bar, h0):
    bu = jnp.einsum('gph,btgh->tbgp', b_bar, u.astype(jnp.complex64))
    if h0 is not None:
        bu = bu.at[0].add(lam_bar * h0)
    a = jnp.broadcast_to(lam_bar, (bu.shape[0], 1) + lam_bar.shape)

    def combine(e1, e2):
        a1, b1 = e1
        a2, b2 = e2
        return a2 * a1, a2 * b1 + b2

    _, h = lax.associative_scan(combine, (a, bu), axis=0)
    return h


def _s5_readout(h, c_re, c_im):
    return (jnp.einsum('ghp,tbgp->btgh', c_re.astype(jnp.float32), h.real)
            - jnp.einsum('ghp,tbgp->btgh', c_im.astype(jnp.float32), h.imag))


def _s5_group(u, u_c, lam_re, lam_im, log_dt, b_re, b_im, c_re, c_im, d_skip, glu_w, glu_b, with_ctx_out):
    def grp(t):
        return t.reshape(t.shape[0], t.shape[1], S5_GROUPS, S5_GH).astype(jnp.float32)

    ul, uc = grp(u), grp(u_c)
    dg = d_skip.astype(jnp.float32).reshape(S5_GROUPS, S5_GH)
    yl, yc = dg * ul, dg * uc
    for d in range(2):
        lam_bar, b_bar = _s5_discretize(lam_re[d], lam_im[d], log_dt[d], b_re[d], b_im[d])
        if d == 1:
            fl = lambda t: jnp.flip(t, axis=1)
        else:
            fl = lambda t: t
        hc = _s5_scan(fl(uc), lam_bar, b_bar, None)
        hl = _s5_scan(fl(ul), lam_bar, b_bar, hc[-1])
        yl = yl + fl(_s5_readout(hl, c_re[d], c_im[d]))
        if with_ctx_out:
            yc = yc + fl(_s5_readout(hc, c_re[d], c_im[d]))

    def glu(y):
        zz = jax.nn.gelu(y.reshape(y.shape[0], y.shape[1], S5_W))
        return (zz * jax.nn.sigmoid(zz @ glu_w.astype(jnp.float32) + glu_b.astype(jnp.float32))).astype(u.dtype)

    return glu(yl), (glu(yc) if with_ctx_out else None)


def _modulate(t, m, i):
    return t * (1.0 + m[:, 3 * i + 1]) + m[:, 3 * i]


def _trunk_layer(x, xc, mod, mod_c, cos, sin, ln_g, ln_b, ffn_w1, ffn_w3, ffn_w2, w_in, w_out,
                 gdn_conv_w, gdn_a_log, gdn_dt_bias, gdn_norm_w, q_norm_w, k_norm_w,
                 s5_lam_re, s5_lam_im, s5_log_dt, s5_b_re, s5_b_im, s5_c_re, s5_c_im, s5_d, glu_w, glu_b,
                 with_ctx_out):
    x = _post_norm_residual(x, _swiglu(_modulate(x, mod, 0), ffn_w1[0], ffn_w3[0], ffn_w2[0]),
                            mod[:, 2], 0.5, ln_g[0], ln_b[0])
    xc = _post_norm_residual(xc, _swiglu(_modulate(xc, mod_c, 0), ffn_w1[0], ffn_w3[0], ffn_w2[0]),
                             mod_c[:, 2], 0.5, ln_g[0], ln_b[0])
    h, hc = _modulate(x, mod, 1), _modulate(xc, mod_c, 1)
    g_qkv, g_z, g_b, g_a, a_q, a_k, a_v, s_u = jnp.split(h @ w_in, list(IN_CUTS), axis=-1)
    c_qkv, c_z, c_b, c_a, c_q, c_k, c_v, c_u = jnp.split(hc @ w_in, list(IN_CUTS), axis=-1)
    o_gdn, oc_gdn = _gdn_group(g_qkv, g_z, g_b, g_a, c_qkv, c_z, c_b, c_a,
                               gdn_conv_w, gdn_a_log, gdn_dt_bias, gdn_norm_w, with_ctx_out)
    o_att, oc_att = _attention_group(a_q, a_k, a_v, c_q, c_k, c_v, q_norm_w, k_norm_w, cos, sin, with_ctx_out)
    o_s5, oc_s5 = _s5_group(s_u, c_u, s5_lam_re, s5_lam_im, s5_log_dt, s5_b_re, s5_b_im,
                            s5_c_re, s5_c_im, s5_d, glu_w, glu_b, with_ctx_out)
    y = jnp.concatenate([o_gdn, o_att, o_s5], axis=-1) @ w_out
    x = _post_norm_residual(x, y, mod[:, 5], 1.0, ln_g[1], ln_b[1])
    x = _post_norm_residual(x, _swiglu(_modulate(x, mod, 2), ffn_w1[1], ffn_w3[1], ffn_w2[1]),
                            mod[:, 8], 0.5, ln_g[2], ln_b[2])
    if not with_ctx_out:
        return x, None
    yc = jnp.concatenate([oc_gdn, oc_att, oc_s5], axis=-1) @ w_out
    xc = _post_norm_residual(xc, yc, mod_c[:, 5], 1.0, ln_g[1], ln_b[1])
    xc = _post_norm_residual(xc, _swiglu(_modulate(xc, mod_c, 2), ffn_w1[1], ffn_w3[1], ffn_w2[1]),
                             mod_c[:, 8], 0.5, ln_g[2], ln_b[2])
    return x, xc


def _fwd_setup_inputs(seed: int = 0) -> dict:
    key = jax.random.key(seed)
    ks = jax.random.split(key, 32)
    f32 = jnp.float32
    L, D = DEPTH, D_MODEL

    def nrm(k, shape, s):
        return jax.random.normal(k, shape, f32) * s

    def unif(k, shape, lo, hi):
        return jax.random.uniform(k, shape, f32, lo, hi)

    dt_g = jnp.exp(unif(ks[15], (L, 2, GDN_HEADS), math.log(1e-3), math.log(1e-1)))
    return {
        'x': nrm(ks[0], (BATCH, SEQ, D), 1.0),
        'c': nrm(ks[1], (BATCH, D), 1.0),
        'ctx': nrm(ks[2], (BATCH, CTX_LEN, D), 1.0),
        'c_ctx': nrm(ks[3], (D,), 1.0),
        'w_ada': nrm(ks[4], (L, D, N_MOD * D), 0.5 * D ** -0.5),
        'b_ada': nrm(ks[5], (L, N_MOD * D), 0.02),
        'ln_g': 1.0 + nrm(ks[6], (L, 3, D), 0.05),
        'ln_b': nrm(ks[7], (L, 3, D), 0.02),
        'ffn_w1': nrm(ks[8], (L, 2, D, D_FF), D ** -0.5),
        'ffn_w3': nrm(ks[9], (L, 2, D, D_FF), D ** -0.5),
        'ffn_w2': nrm(ks[10], (L, 2, D_FF, D), D_FF ** -0.5 * BETA_INIT),
        'w_in': nrm(ks[11], (L, D, IN_COLS), D ** -0.5),
        'w_out': nrm(ks[12], (L, D_MIX, D), D_MIX ** -0.5 * BETA_INIT),
        'gdn_conv_w': nrm(ks[13], (L, CONV_K, GDN_QKV), CONV_K ** -0.5),
        'gdn_a_log': jnp.log(unif(ks[14], (L, 2, GDN_HEADS), 1.0, 16.0)),
        'gdn_dt_bias': dt_g + jnp.log(-jnp.expm1(-dt_g)),
        'gdn_norm_w': 1.0 + nrm(ks[16], (L, GDN_DV), 0.05),
        'q_norm_w': 1.0 + nrm(ks[17], (L, ATT_DH), 0.05),
        'k_norm_w': 1.0 + nrm(ks[18], (L, ATT_DH), 0.05),
        's5_lam_re': -0.5 * (1.0 + nrm(ks[19], (L, 2, S5_GROUPS, S5_P), 0.01)),
        's5_lam_im': jnp.pi * jnp.arange(S5_P, dtype=f32) + nrm(ks[20], (L, 2, S5_GROUPS, S5_P), 0.01),
        's5_log_dt': unif(ks[21], (L, 2, S5_GROUPS), math.log(1e-3), math.log(1e-1)),
        's5_b_re': nrm(ks[22], (L, 2, S5_GROUPS, S5_P, S5_GH), (2.0 * S5_GH) ** -0.5),
        's5_b_im': nrm(ks[23], (L, 2, S5_GROUPS, S5_P, S5_GH), (2.0 * S5_GH) ** -0.5),
        's5_c_re': nrm(ks[24], (L, 2, S5_GROUPS, S5_GH, S5_P), S5_P ** -0.5),
        's5_c_im': nrm(ks[25], (L, 2, S5_GROUPS, S5_GH, S5_P), S5_P ** -0.5),
        's5_d': nrm(ks[26], (L, S5_W), 1.0),
        'glu_w': nrm(ks[27], (L, S5_W, S5_W), S5_W ** -0.5),
        'glu_b': nrm(ks[28], (L, S5_W), 0.02),
    }


def _fwd_reference(x, c, ctx, c_ctx, w_ada, b_ada, ln_g, ln_b, ffn_w1, ffn_w3, ffn_w2, w_in, w_out,
              gdn_conv_w, gdn_a_log, gdn_dt_bias, gdn_norm_w, q_norm_w, k_norm_w,
              s5_lam_re, s5_lam_im, s5_log_dt, s5_b_re, s5_b_im, s5_c_re, s5_c_im, s5_d, glu_w, glu_b):
    Bn, T, D = x.shape
    ROWS = T // GRID_W
    cos, sin = _axial_rope_tables(ROWS)
    sc = jax.nn.silu(c)
    scc = jax.nn.silu(c_ctx)
    xc = ctx
    for layer in range(DEPTH):
        mod = (sc @ w_ada[layer] + b_ada[layer]).reshape(Bn, N_MOD, 1, D)
        mod_c = (scc @ w_ada[layer] + b_ada[layer]).reshape(1, N_MOD, 1, D)
        x, xc = _trunk_layer(
            x, xc, mod, mod_c, cos, sin, ln_g[layer], ln_b[layer],
            ffn_w1[layer], ffn_w3[layer], ffn_w2[layer], w_in[layer], w_out[layer],
            gdn_conv_w[layer], gdn_a_log[layer], gdn_dt_bias[layer], gdn_norm_w[layer],
            q_norm_w[layer], k_norm_w[layer],
            s5_lam_re[layer], s5_lam_im[layer], s5_log_dt[layer], s5_b_re[layer], s5_b_im[layer],
            s5_c_re[layer], s5_c_im[layer], s5_d[layer], glu_w[layer], glu_b[layer],
            with_ctx_out=layer < DEPTH - 1)
    return x


import jax as _jax
import jax.numpy as _jnp

TWIN_FORMAT = 'train_step'
FWD_PARAMS = ['x', 'c', 'ctx', 'c_ctx', 'w_ada', 'b_ada', 'ln_g', 'ln_b', 'ffn_w1', 'ffn_w3', 'ffn_w2', 'w_in', 'w_out', 'gdn_conv_w', 'gdn_a_log', 'gdn_dt_bias', 'gdn_norm_w', 'q_norm_w', 'k_norm_w', 's5_lam_re', 's5_lam_im', 's5_log_dt', 's5_b_re', 's5_b_im', 's5_c_re', 's5_c_im', 's5_d', 'glu_w', 'glu_b']
TWIN_WEIGHTS = ['c_ctx', 'w_ada', 'b_ada', 'ln_g', 'ln_b', 'ffn_w1', 'ffn_w3', 'ffn_w2', 'w_in', 'w_out', 'gdn_conv_w', 'gdn_a_log', 'gdn_dt_bias', 'gdn_norm_w', 'q_norm_w', 'k_norm_w', 's5_lam_re', 's5_lam_im', 's5_log_dt', 's5_b_re', 's5_b_im', 's5_c_re', 's5_c_im', 's5_d', 'glu_w', 'glu_b']
TWIN_DIFF_INPUT = 'x'
TWIN_INPUTS = ['x', 'c', 'ctx', 'c_ctx', 'w_ada', 'b_ada', 'ln_g', 'ln_b', 'ffn_w1', 'ffn_w3', 'ffn_w2', 'w_in', 'w_out', 'gdn_conv_w', 'gdn_a_log', 'gdn_dt_bias', 'gdn_norm_w', 'q_norm_w', 'k_norm_w', 's5_lam_re', 's5_lam_im', 's5_log_dt', 's5_b_re', 's5_b_im', 's5_c_re', 's5_c_im', 's5_d', 'glu_w', 'glu_b', 'loss_target', 'm_c_ctx', 'm_w_ada', 'm_b_ada', 'm_ln_g', 'm_ln_b', 'm_ffn_w1', 'm_ffn_w3', 'm_ffn_w2', 'm_w_in', 'm_w_out', 'm_gdn_conv_w', 'm_gdn_a_log', 'm_gdn_dt_bias', 'm_gdn_norm_w', 'm_q_norm_w', 'm_k_norm_w', 'm_s5_lam_re', 'm_s5_lam_im', 'm_s5_log_dt', 'm_s5_b_re', 'm_s5_b_im', 'm_s5_c_re', 'm_s5_c_im', 'm_s5_d', 'm_glu_w', 'm_glu_b', 'v_c_ctx', 'v_w_ada', 'v_b_ada', 'v_ln_g', 'v_ln_b', 'v_ffn_w1', 'v_ffn_w3', 'v_ffn_w2', 'v_w_in', 'v_w_out', 'v_gdn_conv_w', 'v_gdn_a_log', 'v_gdn_dt_bias', 'v_gdn_norm_w', 'v_q_norm_w', 'v_k_norm_w', 'v_s5_lam_re', 'v_s5_lam_im', 'v_s5_log_dt', 'v_s5_b_re', 'v_s5_b_im', 'v_s5_c_re', 'v_s5_c_im', 'v_s5_d', 'v_glu_w', 'v_glu_b']
TWIN_OUTPUTS = ['loss', 'grad_x', 'grad_c_ctx', 'grad_w_ada', 'grad_b_ada', 'grad_ln_g', 'grad_ln_b', 'grad_ffn_w1', 'grad_ffn_w3', 'grad_ffn_w2', 'grad_w_in', 'grad_w_out', 'grad_gdn_conv_w', 'grad_gdn_a_log', 'grad_gdn_dt_bias', 'grad_gdn_norm_w', 'grad_q_norm_w', 'grad_k_norm_w', 'grad_s5_lam_re', 'grad_s5_lam_im', 'grad_s5_log_dt', 'grad_s5_b_re', 'grad_s5_b_im', 'grad_s5_c_re', 'grad_s5_c_im', 'grad_s5_d', 'grad_glu_w', 'grad_glu_b', 'delta_c_ctx', 'delta_w_ada', 'delta_b_ada', 'delta_ln_g', 'delta_ln_b', 'delta_ffn_w1', 'delta_ffn_w3', 'delta_ffn_w2', 'delta_w_in', 'delta_w_out', 'delta_gdn_conv_w', 'delta_gdn_a_log', 'delta_gdn_dt_bias', 'delta_gdn_norm_w', 'delta_q_norm_w', 'delta_k_norm_w', 'delta_s5_lam_re', 'delta_s5_lam_im', 'delta_s5_log_dt', 'delta_s5_b_re', 'delta_s5_b_im', 'delta_s5_c_re', 'delta_s5_c_im', 'delta_s5_d', 'delta_glu_w', 'delta_glu_b', 'new_m_c_ctx', 'new_m_w_ada', 'new_m_b_ada', 'new_m_ln_g', 'new_m_ln_b', 'new_m_ffn_w1', 'new_m_ffn_w3', 'new_m_ffn_w2', 'new_m_w_in', 'new_m_w_out', 'new_m_gdn_conv_w', 'new_m_gdn_a_log', 'new_m_gdn_dt_bias', 'new_m_gdn_norm_w', 'new_m_q_norm_w', 'new_m_k_norm_w', 'new_m_s5_lam_re', 'new_m_s5_lam_im', 'new_m_s5_log_dt', 'new_m_s5_b_re', 'new_m_s5_b_im', 'new_m_s5_c_re', 'new_m_s5_c_im', 'new_m_s5_d', 'new_m_glu_w', 'new_m_glu_b', 'new_v_c_ctx', 'new_v_w_ada', 'new_v_b_ada', 'new_v_ln_g', 'new_v_ln_b', 'new_v_ffn_w1', 'new_v_ffn_w3', 'new_v_ffn_w2', 'new_v_w_in', 'new_v_w_out', 'new_v_gdn_conv_w', 'new_v_gdn_a_log', 'new_v_gdn_dt_bias', 'new_v_gdn_norm_w', 'new_v_q_norm_w', 'new_v_k_norm_w', 'new_v_s5_lam_re', 'new_v_s5_lam_im', 'new_v_s5_log_dt', 'new_v_s5_b_re', 'new_v_s5_b_im', 'new_v_s5_c_re', 'new_v_s5_c_im', 'new_v_s5_d', 'new_v_glu_w', 'new_v_glu_b']
TWIN_LEAF_KINDS = {'loss': 'loss', 'grad_x': 'grad_x', 'grad_c_ctx': 'grad_w', 'grad_w_ada': 'grad_w', 'grad_b_ada': 'grad_w', 'grad_ln_g': 'grad_w', 'grad_ln_b': 'grad_w', 'grad_ffn_w1': 'grad_w', 'grad_ffn_w3': 'grad_w', 'grad_ffn_w2': 'grad_w', 'grad_w_in': 'grad_w', 'grad_w_out': 'grad_w', 'grad_gdn_conv_w': 'grad_w', 'grad_gdn_a_log': 'grad_w', 'grad_gdn_dt_bias': 'grad_w', 'grad_gdn_norm_w': 'grad_w', 'grad_q_norm_w': 'grad_w', 'grad_k_norm_w': 'grad_w', 'grad_s5_lam_re': 'grad_w', 'grad_s5_lam_im': 'grad_w', 'grad_s5_log_dt': 'grad_w', 'grad_s5_b_re': 'grad_w', 'grad_s5_b_im': 'grad_w', 'grad_s5_c_re': 'grad_w', 'grad_s5_c_im': 'grad_w', 'grad_s5_d': 'grad_w', 'grad_glu_w': 'grad_w', 'grad_glu_b': 'grad_w', 'delta_c_ctx': 'delta_w', 'delta_w_ada': 'delta_w', 'delta_b_ada': 'delta_w', 'delta_ln_g': 'delta_w', 'delta_ln_b': 'delta_w', 'delta_ffn_w1': 'delta_w', 'delta_ffn_w3': 'delta_w', 'delta_ffn_w2': 'delta_w', 'delta_w_in': 'delta_w', 'delta_w_out': 'delta_w', 'delta_gdn_conv_w': 'delta_w', 'delta_gdn_a_log': 'delta_w', 'delta_gdn_dt_bias': 'delta_w', 'delta_gdn_norm_w': 'delta_w', 'delta_q_norm_w': 'delta_w', 'delta_k_norm_w': 'delta_w', 'delta_s5_lam_re': 'delta_w', 'delta_s5_lam_im': 'delta_w', 'delta_s5_log_dt': 'delta_w', 'delta_s5_b_re': 'delta_w', 'delta_s5_b_im': 'delta_w', 'delta_s5_c_re': 'delta_w', 'delta_s5_c_im': 'delta_w', 'delta_s5_d': 'delta_w', 'delta_glu_w': 'delta_w', 'delta_glu_b': 'delta_w', 'new_m_c_ctx': 'new_m', 'new_m_w_ada': 'new_m', 'new_m_b_ada': 'new_m', 'new_m_ln_g': 'new_m', 'new_m_ln_b': 'new_m', 'new_m_ffn_w1': 'new_m', 'new_m_ffn_w3': 'new_m', 'new_m_ffn_w2': 'new_m', 'new_m_w_in': 'new_m', 'new_m_w_out': 'new_m', 'new_m_gdn_conv_w': 'new_m', 'new_m_gdn_a_log': 'new_m', 'new_m_gdn_dt_bias': 'new_m', 'new_m_gdn_norm_w': 'new_m', 'new_m_q_norm_w': 'new_m', 'new_m_k_norm_w': 'new_m', 'new_m_s5_lam_re': 'new_m', 'new_m_s5_lam_im': 'new_m', 'new_m_s5_log_dt': 'new_m', 'new_m_s5_b_re': 'new_m', 'new_m_s5_b_im': 'new_m', 'new_m_s5_c_re': 'new_m', 'new_m_s5_c_im': 'new_m', 'new_m_s5_d': 'new_m', 'new_m_glu_w': 'new_m', 'new_m_glu_b': 'new_m', 'new_v_c_ctx': 'new_v', 'new_v_w_ada': 'new_v', 'new_v_b_ada': 'new_v', 'new_v_ln_g': 'new_v', 'new_v_ln_b': 'new_v', 'new_v_ffn_w1': 'new_v', 'new_v_ffn_w3': 'new_v', 'new_v_ffn_w2': 'new_v', 'new_v_w_in': 'new_v', 'new_v_w_out': 'new_v', 'new_v_gdn_conv_w': 'new_v', 'new_v_gdn_a_log': 'new_v', 'new_v_gdn_dt_bias': 'new_v', 'new_v_gdn_norm_w': 'new_v', 'new_v_q_norm_w': 'new_v', 'new_v_k_norm_w': 'new_v', 'new_v_s5_lam_re': 'new_v', 'new_v_s5_lam_im': 'new_v', 'new_v_s5_log_dt': 'new_v', 'new_v_s5_b_re': 'new_v', 'new_v_s5_b_im': 'new_v', 'new_v_s5_c_re': 'new_v', 'new_v_s5_c_im': 'new_v', 'new_v_s5_d': 'new_v', 'new_v_glu_w': 'new_v', 'new_v_glu_b': 'new_v'}


def _forward(args):
    return _fwd_reference(*[args[k] for k in FWD_PARAMS])


def _output_shape():
    out = _jax.eval_shape(lambda: _forward(_fwd_setup_inputs(0)))
    return out.shape, out.dtype

N_MICROBATCH = 1
ADAM_LR = 0.001
ADAM_B1 = 0.9
ADAM_B2 = 0.999
ADAM_EPS = 1e-08
ADAM_WD = 0.01
ADAM_STEP = 10
PER_EXAMPLE_BATCH_AXIS = {'x': 0, 'c': 0, 'ctx': 0, 'loss_target': 0}
SHARED_INPUTS = []
_WEIGHT_DTYPES = {'c_ctx': _jnp.float32, 'w_ada': _jnp.float32, 'b_ada': _jnp.float32, 'ln_g': _jnp.float32, 'ln_b': _jnp.float32, 'ffn_w1': _jnp.float32, 'ffn_w3': _jnp.float32, 'ffn_w2': _jnp.float32, 'w_in': _jnp.float32, 'w_out': _jnp.float32, 'gdn_conv_w': _jnp.float32, 'gdn_a_log': _jnp.float32, 'gdn_dt_bias': _jnp.float32, 'gdn_norm_w': _jnp.float32, 'q_norm_w': _jnp.float32, 'k_norm_w': _jnp.float32, 's5_lam_re': _jnp.float32, 's5_lam_im': _jnp.float32, 's5_log_dt': _jnp.float32, 's5_b_re': _jnp.float32, 's5_b_im': _jnp.float32, 's5_c_re': _jnp.float32, 's5_c_im': _jnp.float32, 's5_d': _jnp.float32, 'glu_w': _jnp.float32, 'glu_b': _jnp.float32}
MOMENT_SCALE = {'c_ctx': 4.345942e-03, 'w_ada': 1.198527e-02, 'b_ada': 2.060772e-02, 'ln_g': 2.021442e+01, 'ln_b': 1.211428e+00, 'ffn_w1': 4.364951e-03, 'ffn_w3': 4.269385e-03, 'ffn_w2': 1.684248e-02, 'w_in': 1.031934e-02, 'w_out': 2.594863e-02, 'gdn_conv_w': 1.121381e-02, 'gdn_a_log': 2.545525e-02, 'gdn_dt_bias': 2.512555e-02, 'gdn_norm_w': 3.336662e-02, 'q_norm_w': 3.845689e-03, 'k_norm_w': 3.947773e-03, 's5_lam_re': 8.699366e-04, 's5_lam_im': 9.205931e-04, 's5_log_dt': 5.061282e-01, 's5_b_re': 4.831868e-04, 's5_b_im': 4.918247e-04, 's5_c_re': 7.060039e-04, 's5_c_im': 6.441343e-04, 's5_d': 8.969455e-03, 'glu_w': 2.621932e-03, 'glu_b': 3.816913e-03}


def _to_microbatches(a, axis):
    t = _jnp.moveaxis(a, axis, 0)
    t = t.reshape((N_MICROBATCH, t.shape[0] // N_MICROBATCH) + t.shape[1:])
    return _jnp.moveaxis(t, 1, axis + 1)


def setup_inputs(seed: int = 0) -> dict:
    inp = _fwd_setup_inputs(seed)
    key = _jax.random.fold_in(_jax.random.key(seed), 7919)
    shape, _ = _output_shape()
    out = dict(inp)
    out["loss_target"] = _jax.random.normal(_jax.random.fold_in(key, 0), shape, _jnp.float32)
    for i, name in enumerate(TWIN_WEIGHTS):
        w = inp[name].astype(_jnp.float32)
        if MOMENT_SCALE is None:
            s = _jnp.sqrt(_jnp.mean(_jnp.square(w)) + 1e-30)
        else:
            s = MOMENT_SCALE[name]
        km, kv = _jax.random.split(_jax.random.fold_in(key, i + 1))
        out[name] = w
        out["m_" + name] = s * _jax.random.normal(km, w.shape, _jnp.float32)
        out["v_" + name] = (s * s) * _jax.random.uniform(kv, w.shape, _jnp.float32, 0.5, 1.5)
    if N_MICROBATCH > 1:
        for name, axis in PER_EXAMPLE_BATCH_AXIS.items():
            out[name] = _to_microbatches(out[name], axis)
    return {'x': out['x'], 'c': out['c'], 'ctx': out['ctx'], 'c_ctx': out['c_ctx'], 'w_ada': out['w_ada'], 'b_ada': out['b_ada'], 'ln_g': out['ln_g'], 'ln_b': out['ln_b'], 'ffn_w1': out['ffn_w1'], 'ffn_w3': out['ffn_w3'], 'ffn_w2': out['ffn_w2'], 'w_in': out['w_in'], 'w_out': out['w_out'], 'gdn_conv_w': out['gdn_conv_w'], 'gdn_a_log': out['gdn_a_log'], 'gdn_dt_bias': out['gdn_dt_bias'], 'gdn_norm_w': out['gdn_norm_w'], 'q_norm_w': out['q_norm_w'], 'k_norm_w': out['k_norm_w'], 's5_lam_re': out['s5_lam_re'], 's5_lam_im': out['s5_lam_im'], 's5_log_dt': out['s5_log_dt'], 's5_b_re': out['s5_b_re'], 's5_b_im': out['s5_b_im'], 's5_c_re': out['s5_c_re'], 's5_c_im': out['s5_c_im'], 's5_d': out['s5_d'], 'glu_w': out['glu_w'], 'glu_b': out['glu_b'], 'loss_target': out['loss_target'], 'm_c_ctx': out['m_c_ctx'], 'm_w_ada': out['m_w_ada'], 'm_b_ada': out['m_b_ada'], 'm_ln_g': out['m_ln_g'], 'm_ln_b': out['m_ln_b'], 'm_ffn_w1': out['m_ffn_w1'], 'm_ffn_w3': out['m_ffn_w3'], 'm_ffn_w2': out['m_ffn_w2'], 'm_w_in': out['m_w_in'], 'm_w_out': out['m_w_out'], 'm_gdn_conv_w': out['m_gdn_conv_w'], 'm_gdn_a_log': out['m_gdn_a_log'], 'm_gdn_dt_bias': out['m_gdn_dt_bias'], 'm_gdn_norm_w': out['m_gdn_norm_w'], 'm_q_norm_w': out['m_q_norm_w'], 'm_k_norm_w': out['m_k_norm_w'], 'm_s5_lam_re': out['m_s5_lam_re'], 'm_s5_lam_im': out['m_s5_lam_im'], 'm_s5_log_dt': out['m_s5_log_dt'], 'm_s5_b_re': out['m_s5_b_re'], 'm_s5_b_im': out['m_s5_b_im'], 'm_s5_c_re': out['m_s5_c_re'], 'm_s5_c_im': out['m_s5_c_im'], 'm_s5_d': out['m_s5_d'], 'm_glu_w': out['m_glu_w'], 'm_glu_b': out['m_glu_b'], 'v_c_ctx': out['v_c_ctx'], 'v_w_ada': out['v_w_ada'], 'v_b_ada': out['v_b_ada'], 'v_ln_g': out['v_ln_g'], 'v_ln_b': out['v_ln_b'], 'v_ffn_w1': out['v_ffn_w1'], 'v_ffn_w3': out['v_ffn_w3'], 'v_ffn_w2': out['v_ffn_w2'], 'v_w_in': out['v_w_in'], 'v_w_out': out['v_w_out'], 'v_gdn_conv_w': out['v_gdn_conv_w'], 'v_gdn_a_log': out['v_gdn_a_log'], 'v_gdn_dt_bias': out['v_gdn_dt_bias'], 'v_gdn_norm_w': out['v_gdn_norm_w'], 'v_q_norm_w': out['v_q_norm_w'], 'v_k_norm_w': out['v_k_norm_w'], 'v_s5_lam_re': out['v_s5_lam_re'], 'v_s5_lam_im': out['v_s5_lam_im'], 'v_s5_log_dt': out['v_s5_log_dt'], 'v_s5_b_re': out['v_s5_b_re'], 'v_s5_b_im': out['v_s5_b_im'], 'v_s5_c_re': out['v_s5_c_re'], 'v_s5_c_im': out['v_s5_c_im'], 'v_s5_d': out['v_s5_d'], 'v_glu_w': out['v_glu_w'], 'v_glu_b': out['v_glu_b']}


def _loss(weights, diff, rest, loss_target):
    with _jax.named_scope("forward"):
        args = {**rest, TWIN_DIFF_INPUT: diff, **{k: w.astype(_WEIGHT_DTYPES[k]) for k, w in weights.items()}}
        y = _forward(args)
    with _jax.named_scope("loss_head"):
        err = _jnp.square(y.astype(_jnp.float32) - loss_target)
        return 0.5 * _jnp.sum(_jnp.mean(err, axis=-1)) if err.ndim else 0.5 * err


def _adamw(w, g, m, v):
    m = ADAM_B1 * m + (1.0 - ADAM_B1) * g
    v = ADAM_B2 * v + (1.0 - ADAM_B2) * _jnp.square(g)
    m_hat = m / (1.0 - ADAM_B1 ** ADAM_STEP)
    v_hat = v / (1.0 - ADAM_B2 ** ADAM_STEP)
    delta = -ADAM_LR * (m_hat / (_jnp.sqrt(v_hat) + ADAM_EPS) + ADAM_WD * w)
    return delta, m, v


def reference(x, c, ctx, c_ctx, w_ada, b_ada, ln_g, ln_b, ffn_w1, ffn_w3, ffn_w2, w_in, w_out, gdn_conv_w, gdn_a_log, gdn_dt_bias, gdn_norm_w, q_norm_w, k_norm_w, s5_lam_re, s5_lam_im, s5_log_dt, s5_b_re, s5_b_im, s5_c_re, s5_c_im, s5_d, glu_w, glu_b, loss_target, m_c_ctx, m_w_ada, m_b_ada, m_ln_g, m_ln_b, m_ffn_w1, m_ffn_w3, m_ffn_w2, m_w_in, m_w_out, m_gdn_conv_w, m_gdn_a_log, m_gdn_dt_bias, m_gdn_norm_w, m_q_norm_w, m_k_norm_w, m_s5_lam_re, m_s5_lam_im, m_s5_log_dt, m_s5_b_re, m_s5_b_im, m_s5_c_re, m_s5_c_im, m_s5_d, m_glu_w, m_glu_b, v_c_ctx, v_w_ada, v_b_ada, v_ln_g, v_ln_b, v_ffn_w1, v_ffn_w3, v_ffn_w2, v_w_in, v_w_out, v_gdn_conv_w, v_gdn_a_log, v_gdn_dt_bias, v_gdn_norm_w, v_q_norm_w, v_k_norm_w, v_s5_lam_re, v_s5_lam_im, v_s5_log_dt, v_s5_b_re, v_s5_b_im, v_s5_c_re, v_s5_c_im, v_s5_d, v_glu_w, v_glu_b):
    given = dict(x=x, c=c, ctx=ctx, c_ctx=c_ctx, w_ada=w_ada, b_ada=b_ada, ln_g=ln_g, ln_b=ln_b, ffn_w1=ffn_w1, ffn_w3=ffn_w3, ffn_w2=ffn_w2, w_in=w_in, w_out=w_out, gdn_conv_w=gdn_conv_w, gdn_a_log=gdn_a_log, gdn_dt_bias=gdn_dt_bias, gdn_norm_w=gdn_norm_w, q_norm_w=q_norm_w, k_norm_w=k_norm_w, s5_lam_re=s5_lam_re, s5_lam_im=s5_lam_im, s5_log_dt=s5_log_dt, s5_b_re=s5_b_re, s5_b_im=s5_b_im, s5_c_re=s5_c_re, s5_c_im=s5_c_im, s5_d=s5_d, glu_w=glu_w, glu_b=glu_b, loss_target=loss_target, m_c_ctx=m_c_ctx, m_w_ada=m_w_ada, m_b_ada=m_b_ada, m_ln_g=m_ln_g, m_ln_b=m_ln_b, m_ffn_w1=m_ffn_w1, m_ffn_w3=m_ffn_w3, m_ffn_w2=m_ffn_w2, m_w_in=m_w_in, m_w_out=m_w_out, m_gdn_conv_w=m_gdn_conv_w, m_gdn_a_log=m_gdn_a_log, m_gdn_dt_bias=m_gdn_dt_bias, m_gdn_norm_w=m_gdn_norm_w, m_q_norm_w=m_q_norm_w, m_k_norm_w=m_k_norm_w, m_s5_lam_re=m_s5_lam_re, m_s5_lam_im=m_s5_lam_im, m_s5_log_dt=m_s5_log_dt, m_s5_b_re=m_s5_b_re, m_s5_b_im=m_s5_b_im, m_s5_c_re=m_s5_c_re, m_s5_c_im=m_s5_c_im, m_s5_d=m_s5_d, m_glu_w=m_glu_w, m_glu_b=m_glu_b, v_c_ctx=v_c_ctx, v_w_ada=v_w_ada, v_b_ada=v_b_ada, v_ln_g=v_ln_g, v_ln_b=v_ln_b, v_ffn_w1=v_ffn_w1, v_ffn_w3=v_ffn_w3, v_ffn_w2=v_ffn_w2, v_w_in=v_w_in, v_w_out=v_w_out, v_gdn_conv_w=v_gdn_conv_w, v_gdn_a_log=v_gdn_a_log, v_gdn_dt_bias=v_gdn_dt_bias, v_gdn_norm_w=v_gdn_norm_w, v_q_norm_w=v_q_norm_w, v_k_norm_w=v_k_norm_w, v_s5_lam_re=v_s5_lam_re, v_s5_lam_im=v_s5_lam_im, v_s5_log_dt=v_s5_log_dt, v_s5_b_re=v_s5_b_re, v_s5_b_im=v_s5_b_im, v_s5_c_re=v_s5_c_re, v_s5_c_im=v_s5_c_im, v_s5_d=v_s5_d, v_glu_w=v_glu_w, v_glu_b=v_glu_b)
    weights = {n: given[n] for n in TWIN_WEIGHTS}
    shared = {n: given[n] for n in SHARED_INPUTS}
    per_example = {n: given[n] for n in ['x', 'c', 'ctx']}
    grad_fn = _jax.value_and_grad(_loss, argnums=(0, 1))

    def one_microbatch(ex, loss_target):
        ex = dict(ex)
        diff = ex.pop(TWIN_DIFF_INPUT)
        return grad_fn(weights, diff, {**shared, **ex}, loss_target)

    if N_MICROBATCH == 1:
        loss, (grad_w, grad_x) = one_microbatch(per_example, given["loss_target"])
    else:
        def body(carry, xs):
            loss_sum, grad_sum = carry
            l_k, (gw_k, gx_k) = one_microbatch(xs[0], xs[1])
            with _jax.named_scope("update"):
                return (loss_sum + l_k, _jax.tree.map(_jnp.add, grad_sum, gw_k)), gx_k

        init = (_jnp.zeros((), _jnp.float32), _jax.tree.map(_jnp.zeros_like, weights))
        (loss, grad_w), grad_x = _jax.lax.scan(body, init, (per_example, given["loss_target"]))
    with _jax.named_scope("update"):
        delta_w, new_m, new_v = {}, {}, {}
        for n in TWIN_WEIGHTS:
            delta_w[n], new_m[n], new_v[n] = _adamw(weights[n], grad_w[n], given["m_" + n], given["v_" + n])
    return (loss, grad_x, *[grad_w[n] for n in TWIN_WEIGHTS], *[delta_w[n] for n in TWIN_WEIGHTS],
            *[new_m[n] for n in TWIN_WEIGHTS], *[new_v[n] for n in TWIN_WEIGHTS])
```

```python
import functools
import math

import jax
import jax.numpy as jnp
from jax import lax
from jax.experimental import pallas as pl
from jax.experimental.pallas import tpu as pltpu

F32 = jnp.float32
BF16 = jnp.bfloat16
MESH_IDS = pl.DeviceIdType.MESH
ANY = pl.BlockSpec(memory_space=pl.ANY)
N_CHIPS = 4
N_DEV = 8
LANES = 128
SUBLANES = 8
ADA_ROWS = 128
VMEM_LIMIT = 48 * 1024 * 1024

D_MODEL = 1024
GRID_W = 64
GDN_HEADS = 6
GDN_DK = 64
GDN_DV = 64
GDN_W = GDN_HEADS * GDN_DV
GDN_QKV = GDN_HEADS * (2 * GDN_DK + GDN_DV)
CONV_K = 5
CHUNK = 64
ATT_HEADS = 6
ATT_KV_HEADS = 2
ATT_DH = 64
ATT_W = ATT_HEADS * ATT_DH
ATT_GROUP = ATT_HEADS // ATT_KV_HEADS
Q_BLOCK = 128
ROPE_THETA = 10000.0
ROPE_PAIRS = ATT_DH // 4
S5_GROUPS = 16
S5_GH = 16
S5_P = 64
S5_W = S5_GROUPS * S5_GH
S5_LC = 32
S5_ROWS = SUBLANES
S5_CW = S5_LC * S5_GH
S5_SW = 2 * S5_P
GDN_PREP_CHUNKS = 6
GDN_SCAN_CHAINS = 16
HI = lax.Precision.HIGHEST
N_MOD = 9
EPS = 1e-6
OFF_GDN_Z = GDN_QKV
OFF_GDN_B = OFF_GDN_Z + GDN_W
OFF_GDN_A = OFF_GDN_B + 2 * GDN_HEADS
OFF_ATT_Q = OFF_GDN_A + 2 * GDN_HEADS
OFF_ATT_K = OFF_ATT_Q + ATT_W
OFF_ATT_V = OFF_ATT_K + ATT_KV_HEADS * ATT_DH
OFF_S5 = OFF_ATT_V + ATT_KV_HEADS * ATT_DH
IN_COLS = OFF_S5 + S5_W
IN_COLS_PAD = 2560
IN_CUTS = (OFF_GDN_Z, OFF_GDN_B, OFF_GDN_A, OFF_ATT_Q, OFF_ATT_K, OFF_ATT_V, OFF_S5)

ADAM_LR = 0.001
ADAM_B1 = 0.9
ADAM_B2 = 0.999
ADAM_EPS = 1e-08
ADAM_WD = 0.01
ADAM_STEP = 10

WEIGHTS = ['c_ctx', 'w_ada', 'b_ada', 'ln_g', 'ln_b', 'ffn_w1', 'ffn_w3', 'ffn_w2', 'w_in', 'w_out', 'gdn_conv_w',
           'gdn_a_log', 'gdn_dt_bias', 'gdn_norm_w', 'q_norm_w', 'k_norm_w', 's5_lam_re', 's5_lam_im', 's5_log_dt',
           's5_b_re', 's5_b_im', 's5_c_re', 's5_c_im', 's5_d', 'glu_w', 'glu_b']
BIG = ['w_ada', 'ffn_w1', 'ffn_w3', 'ffn_w2', 'w_in', 'w_out']
SMALL_SHARDED = ['ln_g', 'ln_b', 'gdn_conv_w', 'glu_w']
SHARD_AXIS = {'w_ada': 2, 'ffn_w1': 3, 'ffn_w3': 3, 'ffn_w2': 2, 'w_in': 2, 'w_out': 1,
              'ln_g': 2, 'ln_b': 2, 'gdn_conv_w': 2, 'glu_w': 1}
REPLICATED = [n for n in WEIGHTS if n not in BIG and n not in SMALL_SHARDED]
HEAD_PARAMS = ['gdn_conv_w', 'gdn_a_log', 'gdn_dt_bias', 'gdn_norm_w', 'q_norm_w', 'k_norm_w', 's5_lam_re',
               's5_lam_im', 's5_log_dt', 's5_b_re', 's5_b_im', 's5_c_re', 's5_c_im', 's5_d', 'glu_w', 'glu_b']


def _place():
    return lax.axis_index("x"), lax.axis_index("y"), lax.axis_index("c")


def _pick(n, cands):
    for t in cands:
        if n % t == 0:
            return t
    return n


def _remote(src, dst, send_sem, recv_sem, to):
    return pltpu.make_async_remote_copy(src_ref=src, dst_ref=dst, send_sem=send_sem, recv_sem=recv_sem,
                                        device_id=to, device_id_type=MESH_IDS)


def _gather_weight_shards(shards):
    n = len(shards)
    halves = [s.shape[0] // 2 for s in shards]

    def body(*refs):
        ins, outs = refs[:n], refs[n:2 * n]
        send_sems, recv_sems, local_sems = refs[2 * n:]
        x, y, c = _place()
        chip = 2 * x + y
        sibling = (x, y, 1 - c)
        others = [(1 - x, y), (x, 1 - y), (1 - x, 1 - y)]
        local, sends = [], []
        for i in range(n):
            h = halves[i]
            own = pltpu.make_async_copy(ins[i], outs[i].at[chip], local_sems.at[i])
            own.start()
            local.append(own)
            for j, (px, py) in enumerate(others):
                k = 6 * i + j
                cp = _remote(ins[i].at[pl.ds(c * h, h)], outs[i].at[chip, pl.ds(c * h, h)],
                             send_sems.at[k], recv_sems.at[k], (px, py, c))
                cp.start()
                sends.append(cp)
        for i in range(n):
            h = halves[i]
            for j, (px, py) in enumerate(others):
                slab = outs[i].at[2 * px + py, pl.ds(c * h, h)]
                _remote(slab, slab, send_sems.at[6 * i + j], recv_sems.at[6 * i + j], (px, py, c)).wait_recv()
                fw = _remote(slab, slab, send_sems.at[6 * i + 3 + j], recv_sems.at[6 * i + 3 + j], sibling)
                fw.start()
                sends.append(fw)
        for i in range(n):
            h = halves[i]
            for j, (px, py) in enumerate(others):
                slab = outs[i].at[2 * px + py, pl.ds((1 - c) * h, h)]
                _remote(slab, slab, send_sems.at[6 * i + 3 + j], recv_sems.at[6 * i + 3 + j], sibling).wait_recv()
        for cp in sends:
            cp.wait_send()
        for cp in local:
            cp.wait()

    return pl.pallas_call(
        body, name="gather_weight_shards",
        out_shape=[jax.ShapeDtypeStruct((N_CHIPS,) + s.shape, s.dtype) for s in shards],
        in_specs=[ANY] * n, out_specs=[ANY] * n,
        scratch_shapes=[pltpu.SemaphoreType.DMA((6 * n,)), pltpu.SemaphoreType.DMA((6 * n,)),
                        pltpu.SemaphoreType.DMA((n,))],
    )(*shards)


def _all_gather8(v):
    def body(v_ref, out_ref, send_sems, recv_sems, local_sem):
        x, y, c = _place()
        me, sibling = (x, y, c), (x, y, 1 - c)
        chips = [(1 - x, y), (x, 1 - y), (1 - x, 1 - y)]

        def slot(px, py, pc):
            return out_ref.at[4 * px + 2 * py + pc]

        def copy(k, block, to, src=None):
            return _remote(slot(*block) if src is None else src, slot(*block), send_sems.at[k], recv_sems.at[k], to)

        mine = pltpu.make_async_copy(v_ref, slot(*me), local_sem)
        mine.start()
        first = [copy(0, me, sibling, src=v_ref)]
        first += [copy(1 + j, me, (*chip, c), src=v_ref) for j, chip in enumerate(chips)]
        for cp in first:
            cp.start()
        passed = [copy(4 + j, (*chip, c), sibling) for j, chip in enumerate(chips)]
        for j, chip in enumerate(chips):
            copy(1 + j, (*chip, c), me).wait_recv()
            passed[j].start()
        copy(0, sibling, me).wait_recv()
        for j, chip in enumerate(chips):
            copy(4 + j, (*chip, 1 - c), me).wait_recv()
        for cp in first + passed:
            cp.wait_send()
        mine.wait()

    return pl.pallas_call(
        body, name="all_gather8",
        out_shape=jax.ShapeDtypeStruct((N_DEV,) + v.shape, v.dtype),
        in_specs=[ANY], out_specs=ANY,
        scratch_shapes=[pltpu.SemaphoreType.DMA((7,)), pltpu.SemaphoreType.DMA((7,)), pltpu.SemaphoreType.DMA],
    )(v)


def _swap_halves(grads):
    n = len(grads)

    def body(*refs):
        ins, outs = refs[:n], refs[n:2 * n]
        send_sems, recv_sems = refs[2 * n:]
        x, y, c = _place()
        cps = [_remote(ins[i].at[1 - c], outs[i], send_sems.at[i], recv_sems.at[i], (x, y, 1 - c)) for i in range(n)]
        for cp in cps:
            cp.start()
        for cp in cps:
            cp.wait()

    return pl.pallas_call(
        body, name="swap_halves",
        out_shape=[jax.ShapeDtypeStruct(g.shape[1:], g.dtype) for g in grads],
        in_specs=[ANY] * n, out_specs=[ANY] * n,
        scratch_shapes=[pltpu.SemaphoreType.DMA((n,)), pltpu.SemaphoreType.DMA((n,))],
    )(*grads)


def _scatter_partials(parts):
    n = len(parts)

    def body(*refs):
        ins, outs = refs[:n], refs[n:2 * n]
        send_sems, recv_sems = refs[2 * n:]
        x, y, c = _place()
        others = [(1 - x, y), (x, 1 - y), (1 - x, 1 - y)]
        cps = []
        for i in range(n):
            for j, (px, py) in enumerate(others):
                k = 3 * i + j
                cps.append(_remote(ins[i].at[2 * px + py], outs[i].at[j], send_sems.at[k], recv_sems.at[k], (px, py, c)))
        for cp in cps:
            cp.start()
        for cp in cps:
            cp.wait()

    return pl.pallas_call(
        body, name="scatter_partials",
        out_shape=[jax.ShapeDtypeStruct((3,) + p.shape[1:], p.dtype) for p in parts],
        in_specs=[ANY] * n, out_specs=[ANY] * n,
        scratch_shapes=[pltpu.SemaphoreType.DMA((3 * n,)), pltpu.SemaphoreType.DMA((3 * n,))],
    )(*parts)


def _share_halves(halves):
    n = len(halves)

    def body(*refs):
        ins, outs = refs[:n], refs[n:2 * n]
        send_sems, recv_sems, local_sems = refs[2 * n:]
        x, y, c = _place()
        loc, cps = [], []
        for i in range(n):
            h = ins[i].shape[0]
            dst = outs[i].at[pl.ds(c * h, h)]
            loc.append(pltpu.make_async_copy(ins[i], dst, local_sems.at[i]))
            cps.append(_remote(ins[i], dst, send_sems.at[i], recv_sems.at[i], (x, y, 1 - c)))
        for cp in loc + cps:
            cp.start()
        for i in range(n):
            h = ins[i].shape[0]
            theirs = outs[i].at[pl.ds((1 - c) * h, h)]
            _remote(theirs, theirs, send_sems.at[i], recv_sems.at[i], (x, y, 1 - c)).wait_recv()
        for cp in cps:
            cp.wait_send()
        for cp in loc:
            cp.wait()

    return pl.pallas_call(
        body, name="share_halves",
        out_shape=[jax.ShapeDtypeStruct((2 * p.shape[0],) + p.shape[1:], p.dtype) for p in halves],
        in_specs=[ANY] * n, out_specs=[ANY] * n,
        scratch_shapes=[pltpu.SemaphoreType.DMA((n,)), pltpu.SemaphoreType.DMA((n,)), pltpu.SemaphoreType.DMA((n,))],
    )(*halves)


def _row_tile(rows, cols, n_arrays):
    budget = (VMEM_LIMIT // 3) // (2 * n_arrays * 4 * max(cols, LANES))
    for t in (1024, 512, 256, 128, 64, 32, 16, 8):
        if t <= budget and rows % t == 0:
            return t
    return rows


def _add_sibling(grad, recv):
    cols = grad.shape[-1]
    rows = recv.size // cols
    g3 = grad.reshape(2, rows, cols)
    r2 = recv.reshape(rows, cols)
    tr = _row_tile(rows, cols, 3)

    def body(c_ref, g_ref, r_ref, o_ref):
        o_ref[...] = g_ref[...] + r_ref[...]

    out = pl.pallas_call(
        body, name="add_sibling",
        grid_spec=pltpu.PrefetchScalarGridSpec(
            num_scalar_prefetch=1, grid=(rows // tr,),
            in_specs=[pl.BlockSpec((None, tr, cols), lambda i, c_ref: (c_ref[0], i, 0)),
                      pl.BlockSpec((tr, cols), lambda i, c_ref: (i, 0))],
            out_specs=pl.BlockSpec((tr, cols), lambda i, c_ref: (i, 0))),
        out_shape=jax.ShapeDtypeStruct((rows, cols), F32),
        compiler_params=pltpu.CompilerParams(vmem_limit_bytes=VMEM_LIMIT),
    )(lax.axis_index("c").astype(jnp.int32).reshape(1), g3, r2)
    return out.reshape(recv.shape)


def _add_chips(part, recv):
    cols = part.shape[-1]
    rows = part[0].size // cols
    p3 = part.reshape(N_CHIPS, rows, cols)
    r3 = recv.reshape(3, rows, cols)
    tr = _row_tile(rows, cols, 5)

    def body(chip_ref, p_ref, r0_ref, r1_ref, r2_ref, o_ref):
        o_ref[...] = ((p_ref[...] + r0_ref[...]) + r1_ref[...]) + r2_ref[...]

    def recv_spec(j):
        return pl.BlockSpec((None, tr, cols), lambda i, chip_ref: (j, i, 0))

    chip = (2 * lax.axis_index("x") + lax.axis_index("y")).astype(jnp.int32).reshape(1)
    out = pl.pallas_call(
        body, name="add_chips",
        grid_spec=pltpu.PrefetchScalarGridSpec(
            num_scalar_prefetch=1, grid=(rows // tr,),
            in_specs=[pl.BlockSpec((None, tr, cols), lambda i, chip_ref: (chip_ref[0], i, 0)),
                      recv_spec(0), recv_spec(1), recv_spec(2)],
            out_specs=pl.BlockSpec((tr, cols), lambda i, chip_ref: (i, 0))),
        out_shape=jax.ShapeDtypeStruct((rows, cols), F32),
        compiler_params=pltpu.CompilerParams(vmem_limit_bytes=VMEM_LIMIT),
    )(chip, p3, r3, r3, r3)
    return out.reshape(part.shape[1:])


def _sum_devices(gathered):
    _, rows, cols = gathered.shape
    tr = _row_tile(rows, cols, 9)

    def body(g_ref, o_ref):
        acc = g_ref[0]
        for k in range(1, N_DEV):
            acc = acc + g_ref[k]
        o_ref[...] = acc

    return pl.pallas_call(
        body, name="sum_devices", grid=(rows // tr,),
        in_specs=[pl.BlockSpec((N_DEV, tr, cols), lambda i: (0, i, 0))],
        out_specs=pl.BlockSpec((tr, cols), lambda i: (i, 0)),
        out_shape=jax.ShapeDtypeStruct((rows, cols), F32),
        compiler_params=pltpu.CompilerParams(vmem_limit_bytes=VMEM_LIMIT),
    )(gathered)


def _adamw(w, g, m, v):
    shape = w.shape
    cols = shape[-1]
    rows = w.size // cols
    tr = _row_tile(rows, cols, 7)

    def body(w_ref, g_ref, m_ref, v_ref, d_ref, nm_ref, nv_ref):
        gv = g_ref[...]
        nm = ADAM_B1 * m_ref[...] + (1.0 - ADAM_B1) * gv
        nv = ADAM_B2 * v_ref[...] + (1.0 - ADAM_B2) * (gv * gv)
        m_hat = nm / (1.0 - ADAM_B1 ** ADAM_STEP)
        v_hat = nv / (1.0 - ADAM_B2 ** ADAM_STEP)
        d_ref[...] = -ADAM_LR * (m_hat / (jnp.sqrt(v_hat) + ADAM_EPS) + ADAM_WD * w_ref[...])
        nm_ref[...] = nm
        nv_ref[...] = nv

    spec = pl.BlockSpec((tr, cols), lambda i: (i, 0))
    outs = pl.pallas_call(
        body, name="adamw", grid=(rows // tr,),
        in_specs=[spec] * 4, out_specs=[spec] * 3,
        out_shape=[jax.ShapeDtypeStruct((rows, cols), F32)] * 3,
        compiler_params=pltpu.CompilerParams(vmem_limit_bytes=VMEM_LIMIT),
    )(*[t.reshape(rows, cols) for t in (w, g, m, v)])
    return tuple(o.reshape(shape) for o in outs)


_DOT_DIMS = {'nn': (((1,), (0,)), ((), ())), 'nt': (((1,), (1,)), ((), ())), 'tn': (((0,), (0,)), ((), ()))}


def _mm(a, b, mode='nn'):
    if mode == 'nn':
        (m, k), (_, n) = a.shape, b.shape
    elif mode == 'nt':
        (m, k), (n, _) = a.shape, b.shape
    else:
        (k, m), (_, n) = a.shape, b.shape
    tn = _pick(n, (1408, 1280, 1024, 512, 256, 128))
    if mode == 'tn':
        tm = _pick(m, (1408, 1024, 512, 256, 128))
        tk = _pick(k, (512, 256, 128, 64, 32, 16, 8))
    else:
        tm = _pick(m, (1024, 512, 256, 128, 64, 32, 16, 8))
        tk = _pick(k, (1408, 1280, 1024, 512, 256, 128))
    nk = k // tk

    def body(a_ref, b_ref, o_ref, acc_ref):
        step = pl.program_id(2)

        @pl.when(step == 0)
        def _():
            acc_ref[...] = jnp.zeros_like(acc_ref)

        acc_ref[...] += lax.dot_general(a_ref[...].astype(BF16), b_ref[...].astype(BF16), _DOT_DIMS[mode],
                                        preferred_element_type=F32)

        @pl.when(step == nk - 1)
        def _():
            o_ref[...] = acc_ref[...]

    if mode == 'tn':
        a_spec = pl.BlockSpec((tk, tm), lambda i, j, s: (s, i))
    else:
        a_spec = pl.BlockSpec((tm, tk), lambda i, j, s: (i, s))
    if mode == 'nt':
        b_spec = pl.BlockSpec((tn, tk), lambda i, j, s: (j, s))
    else:
        b_spec = pl.BlockSpec((tk, tn), lambda i, j, s: (s, j))
    return pl.pallas_call(
        body, name=f"mm_{mode}_{m}x{k}x{n}", grid=(m // tm, n // tn, nk),
        in_specs=[a_spec, b_spec], out_specs=pl.BlockSpec((tm, tn), lambda i, j, s: (i, j)),
        out_shape=jax.ShapeDtypeStruct((m, n), F32),
        scratch_shapes=[pltpu.VMEM((tm, tn), F32)],
        compiler_params=pltpu.CompilerParams(dimension_semantics=("parallel", "parallel", "arbitrary"),
                                             vmem_limit_bytes=VMEM_LIMIT),
    )(a, b)


def _layer_norm(x, g, b):
    mu = jnp.mean(x, axis=-1, keepdims=True)
    xc = x - mu
    var = jnp.mean(xc * xc, axis=-1, keepdims=True)
    return xc * lax.rsqrt(var + EPS) * g + b


def _layer_norm_bwd(z, g, dout):
    mu = jnp.mean(z, axis=-1, keepdims=True)
    xc = z - mu
    var = jnp.mean(xc * xc, axis=-1, keepdims=True)
    rstd = lax.rsqrt(var + EPS)
    xhat = xc * rstd
    red = tuple(range(z.ndim - 1))
    dg = jnp.sum(dout * xhat, axis=red)
    db = jnp.sum(dout, axis=red)
    dxhat = dout * g
    dz = rstd * (dxhat - jnp.mean(dxhat, axis=-1, keepdims=True)
                 - xhat * jnp.mean(dxhat * xhat, axis=-1, keepdims=True))
    return dz, dg, db


def _rms_norm(x, w):
    return x * lax.rsqrt(jnp.mean(x * x, axis=-1, keepdims=True) + EPS) * w


def _l2_normalize(x):
    return x * lax.rsqrt(jnp.sum(x * x, axis=-1, keepdims=True) + EPS)


def _sum_like(t, like):
    axes = tuple(i for i, (a, b) in enumerate(zip(t.shape, like.shape)) if a != b)
    return jnp.sum(t, axis=axes, keepdims=True) if axes else t


def _rows(streams):
    return jnp.concatenate([s.reshape(-1, s.shape[-1]) for s in streams], axis=0)


def _unrows(flat, streams):
    out, at = [], 0
    for s in streams:
        n = s.shape[0] * s.shape[1]
        out.append(flat[at:at + n].reshape(s.shape[0], s.shape[1], flat.shape[-1]))
        at += n
    return out


def _ffn_fwd(xs, mods, w1, w3, w2, lg, lb, rw, alpha):
    hs = [x * (1.0 + sc) + sh for x, (sh, sc, _) in zip(xs, mods)]
    h = _rows(hs)
    s = jax.nn.silu(_mm(h, w1)) * _mm(h, w3)
    ys = _unrows(_mm(s, w2), xs)
    return [_layer_norm(alpha * x + rw * gt * y, lg, lb) for x, (_, _, gt), y in zip(xs, mods, ys)]


def _ffn_bwd(xs, mods, w1, w3, w2, lg, lb, rw, alpha, douts):
    hs = [x * (1.0 + sc) + sh for x, (sh, sc, _) in zip(xs, mods)]
    h = _rows(hs)
    a, b = _mm(h, w1), _mm(h, w3)
    sig = jax.nn.sigmoid(a)
    silu = a * sig
    s = silu * b
    ys = _unrows(_mm(s, w2), xs)
    dzs, dgts, dys = [], [], []
    dlg = dlb = 0.0
    for x, (_, _, gt), y, dout in zip(xs, mods, ys, douts):
        dz, dg_, db_ = _layer_norm_bwd(alpha * x + rw * gt * y, lg, dout)
        dlg, dlb = dlg + dg_, dlb + db_
        dzs.append(dz)
        dgts.append(_sum_like(rw * y * dz, gt))
        dys.append(rw * gt * dz)
    dy = _rows(dys)
    ds = _mm(dy, w2, 'nt')
    dw2 = _mm(s, dy, 'tn')
    da = ds * b * (sig * (1.0 + a * (1.0 - sig)))
    db = ds * silu
    dw1 = _mm(h, da, 'tn')
    dw3 = _mm(h, db, 'tn')
    dhs = _unrows(_mm(da, w1, 'nt') + _mm(db, w3, 'nt'), xs)
    dxs, dmods = [], []
    for x, (sh, sc, _), dz, dgt, dh in zip(xs, mods, dzs, dgts, dhs):
        dxs.append(alpha * dz + dh * (1.0 + sc))
        dmods.append((_sum_like(dh, sh), _sum_like(dh * x, sc), dgt))
    return dxs, dmods, dw1, dw3, dw2, dlg, dlb


def _dwconv_centred(x, w):
    pad = CONV_K // 2
    return lax.conv_general_dilated(
        x, w[:, None, :].astype(x.dtype), window_strides=(1,), padding=[(pad, pad)],
        dimension_numbers=('NWC', 'WIO', 'NWC'), feature_group_count=x.shape[-1])


def _largest_divisor(n, cap):
    return max(t for t in range(1, cap + 1) if n % t == 0)


def _bmm(a, b):
    return lax.dot_general(a, b, (((2,), (1,)), ((0,), (0,))), precision=HI, preferred_element_type=F32)


def _bmm_nt(a, b):
    return lax.dot_general(a, b, (((2,), (2,)), ((0,), (0,))), precision=HI, preferred_element_type=F32)


def _bmm_tn(a, b):
    return lax.dot_general(a, b, (((1,), (1,)), ((0,), (0,))), precision=HI, preferred_element_type=F32)


def _gdn_prep(q, k, v, gc, gr, beta):
    n = q.shape[0]
    row = lax.broadcasted_iota(jnp.int32, (n, CHUNK, CHUNK), 1)
    col = lax.broadcasted_iota(jnp.int32, (n, CHUNK, CHUNK), 2)
    incl, strict = row >= col, row > col
    decay = jnp.where(incl, jnp.exp(jnp.where(incl, gc - gr, 0.0)), 0.0)
    kb = k * beta
    lower = jnp.where(strict, _bmm_nt(kb, k) * decay, 0.0)
    eye = jnp.where(row == col, 1.0, 0.0)
    inv, power = eye - lower, lower
    for _ in range(5):
        power = _bmm(power, power)
        inv = _bmm(inv, eye + power)
    u = _bmm(inv, v * beta)
    w = _bmm(inv, kb * jnp.exp(gc))
    intra = jnp.where(incl, _bmm_nt(q, k) * decay, 0.0)
    is_last = lax.broadcasted_iota(jnp.int32, (n, CHUNK, 1), 1) == CHUNK - 1
    g_last = jnp.sum(jnp.where(is_last, gc, 0.0), axis=1, keepdims=True)
    return u, w, intra, q * jnp.exp(gc), k * jnp.exp(g_last - gc), jnp.exp(g_last) * jnp.ones((n, 1, GDN_DV), F32)


def _gdn_step(state, u, w, intra, qg, kd, gl):
    v_new = u - _bmm(w, state)
    o = _bmm(qg, state) + _bmm(intra, v_new)
    return o, state * gl + _bmm_tn(kd, v_new)


def _gdn_prep_specs(cb):
    idx = lambda c, i: (c, i, 0, 0)
    mat = pl.BlockSpec((None, cb, CHUNK, GDN_DK), idx)
    colv = pl.BlockSpec((None, cb, CHUNK, 1), idx)
    rowv = pl.BlockSpec((None, cb, 1, CHUNK), idx)
    return mat, colv, rowv


def _gdn_prep_fwd_call(q, k, v, gc, gr, beta):
    chains, nchunks = q.shape[:2]
    cb = _largest_divisor(nchunks, GDN_PREP_CHUNKS)
    mat, colv, rowv = _gdn_prep_specs(cb)

    def body(q_ref, k_ref, v_ref, gc_ref, gr_ref, b_ref, *out_refs):
        outs = _gdn_prep(q_ref[...], k_ref[...], v_ref[...], gc_ref[...], gr_ref[...], b_ref[...])
        for ref, val in zip(out_refs, outs):
            ref[...] = val

    mshape = jax.ShapeDtypeStruct(q.shape, F32)
    return pl.pallas_call(
        body, name="gdn_prep_fwd", grid=(chains, nchunks // cb),
        in_specs=[mat, mat, mat, colv, rowv, colv], out_specs=[mat] * 5 + [rowv],
        out_shape=[mshape] * 5 + [jax.ShapeDtypeStruct(gr.shape, F32)],
        compiler_params=pltpu.CompilerParams(dimension_semantics=("parallel", "parallel"), vmem_limit_bytes=VMEM_LIMIT),
    )(q, k, v, gc, gr, beta)


def _gdn_prep_bwd_call(q, k, v, gc, gr, beta, cts):
    chains, nchunks = q.shape[:2]
    cb = _largest_divisor(nchunks, GDN_PREP_CHUNKS)
    mat, colv, rowv = _gdn_prep_specs(cb)

    def body(q_ref, k_ref, v_ref, gc_ref, gr_ref, b_ref, du, dw, di, dqg, dkd, dgl, *out_refs):
        _, vjp = jax.vjp(_gdn_prep, q_ref[...], k_ref[...], v_ref[...], gc_ref[...], gr_ref[...], b_ref[...])
        grads = vjp((du[...], dw[...], di[...], dqg[...], dkd[...], dgl[...]))
        for ref, val in zip(out_refs, grads):
            ref[...] = val

    return pl.pallas_call(
        body, name="gdn_prep_bwd", grid=(chains, nchunks // cb),
        in_specs=[mat, mat, mat, colv, rowv, colv] + [mat] * 5 + [rowv],
        out_specs=[mat, mat, mat, colv, rowv, colv],
        out_shape=[jax.ShapeDtypeStruct(t.shape, F32) for t in (q, k, v, gc, gr, beta)],
        compiler_params=pltpu.CompilerParams(dimension_semantics=("parallel", "parallel"), vmem_limit_bytes=VMEM_LIMIT),
    )(q, k, v, gc, gr, beta, *cts)


def _gdn_scan_fwd_call(u, w, intra, qg, kd, gl):
    chains, nchunks = u.shape[:2]
    cc = _largest_divisor(chains, GDN_SCAN_CHAINS)
    idx = lambda c, i: (c, i, 0, 0)
    mat = pl.BlockSpec((cc, None, CHUNK, GDN_DK), idx)
    rowv = pl.BlockSpec((cc, None, 1, CHUNK), idx)

    def body(u_ref, w_ref, i_ref, qg_ref, kd_ref, gl_ref, o_ref, hist_ref, state_ref):
        @pl.when(pl.program_id(1) == 0)
        def _():
            state_ref[...] = jnp.zeros_like(state_ref)

        state = state_ref[...]
        hist_ref[...] = state
        o, new = _gdn_step(state, u_ref[...], w_ref[...], i_ref[...], qg_ref[...], kd_ref[...], gl_ref[...])
        o_ref[...] = o
        state_ref[...] = new

    mshape = jax.ShapeDtypeStruct(u.shape, F32)
    return pl.pallas_call(
        body, name="gdn_scan_fwd", grid=(chains // cc, nchunks),
        in_specs=[mat] * 5 + [rowv], out_specs=[mat, mat], out_shape=[mshape, mshape],
        scratch_shapes=[pltpu.VMEM((cc, GDN_DK, GDN_DV), F32)],
        compiler_params=pltpu.CompilerParams(dimension_semantics=("parallel", "arbitrary"), vmem_limit_bytes=VMEM_LIMIT),
    )(u, w, intra, qg, kd, gl)


def _gdn_scan_bwd_call(u, w, intra, qg, kd, gl, hist, do):
    chains, nchunks = u.shape[:2]
    cc = _largest_divisor(chains, GDN_SCAN_CHAINS)
    idx = lambda c, i: (c, nchunks - 1 - i, 0, 0)
    mat = pl.BlockSpec((cc, None, CHUNK, GDN_DK), idx)
    rowv = pl.BlockSpec((cc, None, 1, CHUNK), idx)

    def body(u_ref, w_ref, i_ref, qg_ref, kd_ref, gl_ref, h_ref, do_ref, du, dw, di, dqg, dkd, dgl, dstate_ref):
        @pl.when(pl.program_id(1) == 0)
        def _():
            dstate_ref[...] = jnp.zeros_like(dstate_ref)

        _, vjp = jax.vjp(_gdn_step, h_ref[...], u_ref[...], w_ref[...], i_ref[...], qg_ref[...], kd_ref[...], gl_ref[...])
        grads = vjp((do_ref[...], dstate_ref[...]))
        dstate_ref[...] = grads[0]
        for ref, val in zip((du, dw, di, dqg, dkd, dgl), grads[1:]):
            ref[...] = val

    mshape = jax.ShapeDtypeStruct(u.shape, F32)
    return pl.pallas_call(
        body, name="gdn_scan_bwd", grid=(chains // cc, nchunks),
        in_specs=[mat] * 5 + [rowv, mat, mat], out_specs=[mat] * 5 + [rowv],
        out_shape=[mshape] * 5 + [jax.ShapeDtypeStruct(gl.shape, F32)],
        scratch_shapes=[pltpu.VMEM((cc, GDN_DK, GDN_DV), F32)],
        compiler_params=pltpu.CompilerParams(dimension_semantics=("parallel", "arbitrary"), vmem_limit_bytes=VMEM_LIMIT),
    )(u, w, intra, qg, kd, gl, hist, do)


@jax.custom_vjp
def _gdn_core(q, k, v, gc, gr, beta):
    return _gdn_scan_fwd_call(*_gdn_prep_fwd_call(q, k, v, gc, gr, beta))[0]


def _gdn_core_fwd(q, k, v, gc, gr, beta):
    prep = _gdn_prep_fwd_call(q, k, v, gc, gr, beta)
    o, hist = _gdn_scan_fwd_call(*prep)
    return o, (q, k, v, gc, gr, beta, prep, hist)


def _gdn_core_bwd(res, do):
    q, k, v, gc, gr, beta, prep, hist = res
    cts = _gdn_scan_bwd_call(*prep, hist, do)
    return tuple(_gdn_prep_bwd_call(q, k, v, gc, gr, beta, cts))


_gdn_core.defvjp(_gdn_core_fwd, _gdn_core_bwd)


def _gdn_inputs(qkv, b, a, conv_w, a_log, dt_bias):
    Bn, T, _ = qkv.shape
    qkv = jax.nn.silu(_dwconv_centred(qkv, conv_w))
    q, k, v = jnp.split(qkv, [GDN_HEADS * GDN_DK, 2 * GDN_HEADS * GDN_DK], axis=-1)

    def to_heads(t, d):
        return t.reshape(Bn, T, GDN_HEADS, d).transpose(0, 2, 1, 3)

    def dir_heads(t):
        return t.reshape(Bn, T, 2, GDN_HEADS).transpose(2, 0, 3, 1)

    q = _l2_normalize(to_heads(q, GDN_DK))
    k = _l2_normalize(to_heads(k, GDN_DK))
    v = to_heads(v, GDN_DV)
    beta = jax.nn.sigmoid(dir_heads(b))
    g = -jnp.exp(a_log)[:, None, :, None] * jax.nn.softplus(dir_heads(a) + dt_bias[:, None, :, None])
    return q, k, v, g, beta


def _gdn_gated_out(o, z, norm_w):
    Bn, H, T, dv = o.shape
    o = _rms_norm(o.transpose(0, 2, 1, 3), norm_w)
    o = o * jax.nn.silu(z.reshape(Bn, T, H, dv))
    return o.reshape(Bn, T, H * dv)


def _gdn_group(qkv, z, b, a, qkv_c, z_c, b_c, a_c, conv_w, a_log, dt_bias, norm_w, with_ctx_out):
    q, k, v, g, beta = _gdn_inputs(qkv, b, a, conv_w, a_log, dt_bias)
    qc, kc, vc, gc, betac = _gdn_inputs(qkv_c, b_c, a_c, conv_w, a_log, dt_bias)
    bn, heads, t_lat, _ = q.shape
    t_ctx = qc.shape[2]
    n = t_ctx + t_lat
    nchunks = n // CHUNK
    chains = 2 * bn * heads

    def both(tc, tl):
        return jnp.stack([jnp.concatenate([tc, tl], axis=2), jnp.concatenate([jnp.flip(tc, 2), jnp.flip(tl, 2)], axis=2)])

    def per_dir(tc, tl):
        return jnp.stack([jnp.concatenate([tc[0], tl[0]], axis=2),
                          jnp.concatenate([jnp.flip(tc[1], 2), jnp.flip(tl[1], 2)], axis=2)])

    def mats(t):
        return t.reshape(chains, nchunks, CHUNK, t.shape[-1])

    gcum = jnp.cumsum(per_dir(gc, g).reshape(chains, nchunks, CHUNK), axis=-1)
    o = _gdn_core(mats(both(qc, q) * GDN_DK ** -0.5), mats(both(kc, k)), mats(both(vc, v)),
                  gcum[..., None], gcum[:, :, None, :], per_dir(betac, beta).reshape(chains, nchunks, CHUNK, 1))
    o = o.reshape(2, bn, heads, n, GDN_DV)
    out = _gdn_gated_out(o[0, :, :, t_ctx:] + jnp.flip(o[1, :, :, t_ctx:], 2), z, norm_w)
    if not with_ctx_out:
        return out, None
    return out, _gdn_gated_out(o[0, :, :, :t_ctx] + jnp.flip(o[1, :, :, :t_ctx], 2), z_c, norm_w)


def _axial_rope_tables(rows):
    row = jnp.repeat(jnp.arange(rows), GRID_W)
    col = jnp.tile(jnp.arange(GRID_W), rows)
    inv_freq = ROPE_THETA ** (-jnp.arange(ROPE_PAIRS, dtype=F32) / ROPE_PAIRS)
    ang = jnp.stack([row, col], axis=-1).astype(F32)[..., None] * inv_freq
    return jnp.cos(ang), jnp.sin(ang)


def _rope_2d(x, cos, sin):
    shp = x.shape
    xr = x.reshape(*shp[:-1], 2, 2, ROPE_PAIRS)
    x1, x2 = xr[..., 0, :], xr[..., 1, :]
    bshape = (shp[1],) + (1,) * (x.ndim - 3) + (2, ROPE_PAIRS)
    c, s = cos.reshape(bshape), sin.reshape(bshape)
    out = jnp.stack([x1 * c - x2 * s, x2 * c + x1 * s], axis=-2)
    return out.reshape(shp)


def _attn_specs(tq, tk):
    q_spec = pl.BlockSpec((None, None, ATT_GROUP, tq, ATT_DH), lambda b, h, i: (b, h, 0, i, 0))
    kv_spec = pl.BlockSpec((None, None, tk, ATT_DH), lambda b, h, i: (b, h, 0, 0))
    return q_spec, kv_spec


def _softmax_rows(q, k):
    s = lax.dot_general(q.astype(BF16), k.astype(BF16), (((1,), (1,)), ((), ())), preferred_element_type=F32)
    s = s * (ATT_DH ** -0.5)
    e = jnp.exp(s - jnp.max(s, axis=-1, keepdims=True))
    return e / jnp.sum(e, axis=-1, keepdims=True)


def _attn_fwd_call(q, k, v):
    bn, _, _, t_q, _ = q.shape
    t_k = k.shape[2]
    tq = _pick(t_q, (Q_BLOCK, 64, 32, 16, 8))
    q_spec, kv_spec = _attn_specs(tq, t_k)

    def body(q_ref, k_ref, v_ref, o_ref):
        p = _softmax_rows(q_ref[...].reshape(ATT_GROUP * tq, ATT_DH), k_ref[...])
        o = lax.dot_general(p.astype(BF16), v_ref[...].astype(BF16), (((1,), (0,)), ((), ())), preferred_element_type=F32)
        o_ref[...] = o.reshape(ATT_GROUP, tq, ATT_DH)

    return pl.pallas_call(
        body, name=f"attention_fwd_{t_q}x{t_k}", grid=(bn, ATT_KV_HEADS, t_q // tq),
        in_specs=[q_spec, kv_spec, kv_spec], out_specs=q_spec,
        out_shape=jax.ShapeDtypeStruct(q.shape, F32),
        compiler_params=pltpu.CompilerParams(dimension_semantics=("parallel", "parallel", "parallel"),
                                             vmem_limit_bytes=VMEM_LIMIT),
    )(q, k, v)


def _attn_bwd_call(q, k, v, do):
    bn, _, _, t_q, _ = q.shape
    t_k = k.shape[2]
    tq = _pick(t_q, (Q_BLOCK, 64, 32, 16, 8))
    q_spec, kv_spec = _attn_specs(tq, t_k)

    def body(q_ref, k_ref, v_ref, do_ref, dq_ref, dk_ref, dv_ref):
        @pl.when(pl.program_id(2) == 0)
        def _():
            dk_ref[...] = jnp.zeros_like(dk_ref)
            dv_ref[...] = jnp.zeros_like(dv_ref)

        qv = q_ref[...].reshape(ATT_GROUP * tq, ATT_DH)
        dov = do_ref[...].reshape(ATT_GROUP * tq, ATT_DH).astype(BF16)
        kb, vb = k_ref[...].astype(BF16), v_ref[...].astype(BF16)
        p = _softmax_rows(qv, k_ref[...])
        dp = lax.dot_general(dov, vb, (((1,), (1,)), ((), ())), preferred_element_type=F32)
        ds = p * (dp - jnp.sum(p * dp, axis=-1, keepdims=True)) * (ATT_DH ** -0.5)
        dsb = ds.astype(BF16)
        dq = lax.dot_general(dsb, kb, (((1,), (0,)), ((), ())), preferred_element_type=F32)
        dq_ref[...] = dq.reshape(ATT_GROUP, tq, ATT_DH)
        dk_ref[...] += lax.dot_general(dsb, qv.astype(BF16), (((0,), (0,)), ((), ())), preferred_element_type=F32)
        dv_ref[...] += lax.dot_general(p.astype(BF16), dov, (((0,), (0,)), ((), ())), preferred_element_type=F32)

    return pl.pallas_call(
        body, name=f"attention_bwd_{t_q}x{t_k}", grid=(bn, ATT_KV_HEADS, t_q // tq),
        in_specs=[q_spec, kv_spec, kv_spec, q_spec], out_specs=[q_spec, kv_spec, kv_spec],
        out_shape=[jax.ShapeDtypeStruct(t.shape, F32) for t in (q, k, v)],
        compiler_params=pltpu.CompilerParams(dimension_semantics=("parallel", "parallel", "arbitrary"),
                                             vmem_limit_bytes=VMEM_LIMIT),
    )(q, k, v, do)


@jax.custom_vjp
def _attn_core(q, k, v):
    return _attn_fwd_call(q, k, v)


def _attn_core_fwd(q, k, v):
    return _attn_fwd_call(q, k, v), (q, k, v)


def _attn_core_bwd(res, do):
    return tuple(_attn_bwd_call(*res, do))


_attn_core.defvjp(_attn_core_fwd, _attn_core_bwd)


def _attention_group(q, k, v, q_c, k_c, v_c, q_norm_w, k_norm_w, cos, sin, with_ctx_out):
    Bn, T, _ = q.shape
    Tc = q_c.shape[1]
    q = _rope_2d(_rms_norm(q.reshape(Bn, T, ATT_KV_HEADS, ATT_GROUP, ATT_DH), q_norm_w), cos, sin)
    k = _rope_2d(_rms_norm(k.reshape(Bn, T, ATT_KV_HEADS, ATT_DH), k_norm_w), cos, sin)
    v = v.reshape(Bn, T, ATT_KV_HEADS, ATT_DH)
    qc = _rms_norm(q_c.reshape(Bn, Tc, ATT_KV_HEADS, ATT_GROUP, ATT_DH), q_norm_w)
    kc = _rms_norm(k_c.reshape(Bn, Tc, ATT_KV_HEADS, ATT_DH), k_norm_w)
    vc = v_c.reshape(Bn, Tc, ATT_KV_HEADS, ATT_DH)
    keys = jnp.concatenate([kc, k], axis=1).transpose(0, 2, 1, 3)
    vals = jnp.concatenate([vc, v], axis=1).transpose(0, 2, 1, 3)
    o = _attn_core(q.transpose(0, 2, 3, 1, 4), keys, vals)
    o = o.transpose(0, 3, 1, 2, 4).reshape(Bn, T, ATT_W)
    if not with_ctx_out:
        return o, None
    o_c = _attn_core(qc.transpose(0, 2, 3, 1, 4), kc.transpose(0, 2, 1, 3), vc.transpose(0, 2, 1, 3))
    return o, o_c.transpose(0, 3, 1, 2, 4).reshape(Bn, Tc, ATT_W)


def _s5_operators(lam_re, lam_im, log_dt, b_re, b_im, c_re, c_im):
    lc = S5_LC
    dt = jnp.exp(log_dt)[..., None]
    ar, ai = lam_re * dt, lam_im * dt
    mag = jnp.exp(ar)
    lbr, lbi = mag * jnp.cos(ai), mag * jnp.sin(ai)
    den = lam_re * lam_re + lam_im * lam_im
    fr = ((lbr - 1.0) * lam_re + lbi * lam_im) / den
    fi = (lbi * lam_re - (lbr - 1.0) * lam_im) / den
    bbr = fr[..., None] * b_re - fi[..., None] * b_im
    bbi = fr[..., None] * b_im + fi[..., None] * b_re
    m = jnp.arange(lc + 1, dtype=F32)[:, None, None, None]
    pmag = jnp.exp(m * ar)
    pr, pi = pmag * jnp.cos(m * ai), pmag * jnp.sin(m * ai)
    cpr = c_re[None] * pr[:, :, :, None, :] - c_im[None] * pi[:, :, :, None, :]
    cpi = c_re[None] * pi[:, :, :, None, :] + c_im[None] * pr[:, :, :, None, :]
    kern = (jnp.einsum('mdghp,dgpk->mdghk', cpr[:lc], bbr, precision=HI)
            - jnp.einsum('mdghp,dgpk->mdghk', cpi[:lc], bbi, precision=HI))
    tail = kern.shape[1:]
    lags = jnp.concatenate([jnp.zeros((lc - 1,) + tail, F32), kern, jnp.zeros((1,) + tail, F32)], axis=0)
    toep = jnp.tile(lags, (lc,) + (1,) * len(tail))[:lc * (2 * lc - 1)].reshape((lc, 2 * lc - 1) + tail)[:, lc - 1:]
    tm = toep.transpose(2, 3, 0, 5, 1, 4).reshape(2, S5_GROUPS, S5_CW, S5_CW)
    prr, pir = pr[lc - 1::-1], pi[lc - 1::-1]
    mre = prr[..., None] * bbr[None] - pir[..., None] * bbi[None]
    mim = prr[..., None] * bbi[None] + pir[..., None] * bbr[None]
    mm = jnp.concatenate([mre, mim], axis=3).transpose(1, 2, 0, 4, 3).reshape(2, S5_GROUPS, S5_CW, S5_SW)
    nm = jnp.concatenate([cpr[1:], -cpi[1:]], axis=-1)
    nm = nm.transpose(1, 2, 4, 0, 3).reshape(2, S5_GROUPS, S5_SW, S5_CW)
    a1 = jnp.concatenate([pr[lc], pr[lc]], axis=-1)
    a2 = jnp.concatenate([-pi[lc], pi[lc]], axis=-1)
    lam_rows = jnp.concatenate([a1[:, :, None], a2[:, :, None],
                                jnp.zeros((2, S5_GROUPS, SUBLANES - 2, S5_SW), F32)], axis=2)
    return tm, mm, nm, lam_rows


def _dot_hi(a, b, dims):
    return lax.dot_general(a, b, (dims, ((), ())), precision=HI, preferred_element_type=F32)


def _s5_blocks(rows):
    def blk(r, c):
        return pl.BlockSpec((None, None, r, c), lambda d, g: (d, g, 0, 0))
    return (blk(rows, S5_CW), blk(S5_CW, S5_CW), blk(S5_CW, S5_SW), blk(S5_SW, S5_CW), blk(SUBLANES, S5_SW),
            blk(rows, S5_SW))


def _s5_core_fwd_call(s, tm, mm, nm, lam_rows):
    rows = s.shape[2]
    chunks = rows // S5_ROWS
    seq, top, mop, nop, lop, sta = _s5_blocks(rows)

    def body(s_ref, t_ref, m_ref, n_ref, l_ref, y_ref, h_ref, e_ref):
        sv = s_ref[...]
        e_ref[...] = _dot_hi(sv, m_ref[...], ((1,), (0,)))
        a1, a2 = l_ref[0:1, :], l_ref[1:2, :]
        h_ref[0:S5_ROWS, :] = jnp.zeros((S5_ROWS, S5_SW), F32)

        def step(k, carry):
            at = pl.multiple_of((k - 1) * S5_ROWS, S5_ROWS)
            prev = h_ref[pl.ds(at, S5_ROWS), :]
            new = a1 * prev + a2 * pltpu.roll(prev, S5_P, 1) + e_ref[pl.ds(at, S5_ROWS), :]
            h_ref[pl.ds(pl.multiple_of(k * S5_ROWS, S5_ROWS), S5_ROWS), :] = new
            return carry

        lax.fori_loop(1, chunks, step, 0)
        y_ref[...] = _dot_hi(sv, t_ref[...], ((1,), (0,))) + _dot_hi(h_ref[...], n_ref[...], ((1,), (0,)))

    return pl.pallas_call(
        body, name="s5_chunks_fwd", grid=(2, S5_GROUPS),
        in_specs=[seq, top, mop, nop, lop], out_specs=[seq, sta],
        out_shape=[jax.ShapeDtypeStruct(s.shape, F32), jax.ShapeDtypeStruct(s.shape[:3] + (S5_SW,), F32)],
        scratch_shapes=[pltpu.VMEM((rows, S5_SW), F32)],
        compiler_params=pltpu.CompilerParams(dimension_semantics=("parallel", "parallel"), vmem_limit_bytes=VMEM_LIMIT),
    )(s, tm, mm, nm, lam_rows)


def _s5_core_bwd_call(s, dy, tm, mm, nm, lam_rows, hin):
    rows = s.shape[2]
    chunks = rows // S5_ROWS
    seq, top, mop, nop, lop, sta = _s5_blocks(rows)

    def body(s_ref, dy_ref, t_ref, m_ref, n_ref, l_ref, h_ref, ds_ref, dt_ref, dm_ref, dn_ref, dl_ref, dh_ref, de_ref):
        sv, dyv, hv = s_ref[...], dy_ref[...], h_ref[...]
        dh_ref[...] = _dot_hi(dyv, n_ref[...], ((1,), (1,)))
        a1, a2 = l_ref[0:1, :], l_ref[1:2, :]
        last = (chunks - 1) * S5_ROWS
        de_ref[last:last + S5_ROWS, :] = jnp.zeros((S5_ROWS, S5_SW), F32)

        def step(i, g):
            k = chunks - 2 - i
            at = pl.multiple_of(k * S5_ROWS, S5_ROWS)
            de_ref[pl.ds(at, S5_ROWS), :] = g
            return dh_ref[pl.ds(at, S5_ROWS), :] + a1 * g + pltpu.roll(a2 * g, S5_P, 1)

        lax.fori_loop(0, chunks - 1, step, dh_ref[last:last + S5_ROWS, :])
        dev = de_ref[...]
        ds_ref[...] = _dot_hi(dyv, t_ref[...], ((1,), (1,))) + _dot_hi(dev, m_ref[...], ((1,), (1,)))
        dt_ref[...] = _dot_hi(sv, dyv, ((0,), (0,)))
        dm_ref[...] = _dot_hi(sv, dev, ((0,), (0,)))
        dn_ref[...] = _dot_hi(hv, dyv, ((0,), (0,)))
        da1 = jnp.sum(hv * dev, axis=0, keepdims=True)
        da2 = jnp.sum(pltpu.roll(hv, S5_P, 1) * dev, axis=0, keepdims=True)
        dl_ref[...] = jnp.concatenate([da1, da2, jnp.zeros((SUBLANES - 2, S5_SW), F32)], axis=0)

    return pl.pallas_call(
        body, name="s5_chunks_bwd", grid=(2, S5_GROUPS),
        in_specs=[seq, seq, top, mop, nop, lop, sta], out_specs=[seq, top, mop, nop, lop],
        out_shape=[jax.ShapeDtypeStruct(t.shape, F32) for t in (s, tm, mm, nm, lam_rows)],
        scratch_shapes=[pltpu.VMEM((rows, S5_SW), F32), pltpu.VMEM((rows, S5_SW), F32)],
        compiler_params=pltpu.CompilerParams(dimension_semantics=("parallel", "parallel"), vmem_limit_bytes=VMEM_LIMIT),
    )(s, dy, tm, mm, nm, lam_rows, hin)


@jax.custom_vjp
def _s5_core(s, tm, mm, nm, lam_rows):
    return _s5_core_fwd_call(s, tm, mm, nm, lam_rows)[0]


def _s5_core_fwd(s, tm, mm, nm, lam_rows):
    y, hin = _s5_core_fwd_call(s, tm, mm, nm, lam_rows)
    return y, (s, tm, mm, nm, lam_rows, hin)


def _s5_core_bwd(res, dy):
    s, tm, mm, nm, lam_rows, hin = res
    return tuple(_s5_core_bwd_call(s, dy, tm, mm, nm, lam_rows, hin))


_s5_core.defvjp(_s5_core_fwd, _s5_core_bwd)


def _s5_group(u, u_c, lam_re, lam_im, log_dt, b_re, b_im, c_re, c_im, d_skip, glu_w, glu_b, with_ctx_out):
    bn, t_lat, _ = u.shape
    t_ctx = u_c.shape[1]
    n = t_ctx + t_lat
    chunks = n // S5_LC
    assert bn <= S5_ROWS and t_ctx % S5_LC == 0 and t_lat % S5_LC == 0
    seqs = jnp.stack([jnp.concatenate([u_c, u], axis=1),
                      jnp.concatenate([jnp.flip(u_c, 1), jnp.flip(u, 1)], axis=1)])
    s = seqs.reshape(2, bn, chunks, S5_LC, S5_GROUPS, S5_GH).transpose(0, 4, 2, 1, 3, 5)
    s = jnp.pad(s, ((0, 0), (0, 0), (0, 0), (0, S5_ROWS - bn), (0, 0), (0, 0)))
    s = s.reshape(2, S5_GROUPS, chunks * S5_ROWS, S5_CW)
    y = _s5_core(s, *_s5_operators(lam_re, lam_im, log_dt, b_re, b_im, c_re, c_im))
    y = y.reshape(2, S5_GROUPS, chunks, S5_ROWS, S5_LC, S5_GH)[:, :, :, :bn]
    y = y.transpose(0, 3, 2, 4, 1, 5).reshape(2, bn, n, S5_W)
    yl = d_skip * u + y[0, :, t_ctx:] + jnp.flip(y[1, :, t_ctx:], 1)

    def glu(yy):
        zz = jax.nn.gelu(yy)
        return zz * jax.nn.sigmoid(zz @ glu_w + glu_b)

    if not with_ctx_out:
        return glu(yl), None
    yc = d_skip * u_c + y[0, :, :t_ctx] + jnp.flip(y[1, :, :t_ctx], 1)
    return glu(yl), glu(yc)


def _heads(proj, proj_c, hp, cos, sin, with_ctx_out):
    g_qkv, g_z, g_b, g_a, a_q, a_k, a_v, s_u = jnp.split(proj[..., :IN_COLS], list(IN_CUTS), axis=-1)
    c_qkv, c_z, c_b, c_a, c_q, c_k, c_v, c_u = jnp.split(proj_c[..., :IN_COLS], list(IN_CUTS), axis=-1)
    o_gdn, oc_gdn = _gdn_group(g_qkv, g_z, g_b, g_a, c_qkv, c_z, c_b, c_a, hp['gdn_conv_w'], hp['gdn_a_log'],
                               hp['gdn_dt_bias'], hp['gdn_norm_w'], with_ctx_out)
    o_att, oc_att = _attention_group(a_q, a_k, a_v, c_q, c_k, c_v, hp['q_norm_w'], hp['k_norm_w'], cos, sin,
                                     with_ctx_out)
    o_s5, oc_s5 = _s5_group(s_u, c_u, hp['s5_lam_re'], hp['s5_lam_im'], hp['s5_log_dt'], hp['s5_b_re'], hp['s5_b_im'],
                            hp['s5_c_re'], hp['s5_c_im'], hp['s5_d'], hp['glu_w'], hp['glu_b'], with_ctx_out)
    o = jnp.concatenate([o_gdn, o_att, o_s5], axis=-1)
    if not with_ctx_out:
        return (o,)
    return o, jnp.concatenate([oc_gdn, oc_att, oc_s5], axis=-1)


def _mixer_fwd(xs, mods, w_in, w_out, lg, lb, hp, cos, sin, alpha, with_ctx_out):
    hs = [x * (1.0 + sc) + sh for x, (sh, sc, _) in zip(xs, mods)]
    proj, proj_c = _unrows(_mm(_rows(hs), w_in), xs)
    os_ = _heads(proj, proj_c, hp, cos, sin, with_ctx_out)
    ys = _unrows(_mm(_rows(os_), w_out), os_)
    return [_layer_norm(alpha * x + gt * y, lg, lb) for x, (_, _, gt), y in zip(xs, mods, ys)]


def _mixer_bwd(xs, mods, w_in, w_out, lg, lb, hp, cos, sin, alpha, with_ctx_out, douts):
    hs = [x * (1.0 + sc) + sh for x, (sh, sc, _) in zip(xs, mods)]
    h = _rows(hs)
    proj, proj_c = _unrows(_mm(h, w_in), xs)
    os_, heads_vjp = jax.vjp(lambda p, pc, hp_: _heads(p, pc, hp_, cos, sin, with_ctx_out), proj, proj_c, hp)
    o = _rows(os_)
    ys = _unrows(_mm(o, w_out), os_)
    dzs, dgts, dys = [], [], []
    dlg = dlb = 0.0
    for x, (_, _, gt), y, dout in zip(xs, mods, ys, douts):
        dz, dg_, db_ = _layer_norm_bwd(alpha * x + gt * y, lg, dout)
        dlg, dlb = dlg + dg_, dlb + db_
        dzs.append(dz)
        dgts.append(_sum_like(y * dz, gt))
        dys.append(gt * dz)
    dy = _rows(dys)
    dw_out = _mm(o, dy, 'tn')
    dos = _unrows(_mm(dy, w_out, 'nt'), os_)
    dproj, dproj_c, dhp = heads_vjp(tuple(dos))
    dp = _rows([dproj, dproj_c])
    dw_in = _mm(h, dp, 'tn')
    dhs = _unrows(_mm(dp, w_in, 'nt'), xs)
    dxs, dmods = [], []
    for i, (x, (sh, sc, gt), dh) in enumerate(zip(xs, mods, dhs)):
        dx = dh * (1.0 + sc)
        if i < len(dzs):
            dx = dx + alpha * dzs[i]
            dgt = dgts[i]
        else:
            dgt = jnp.zeros_like(gt)
        dxs.append(dx)
        dmods.append((_sum_like(dh, sh), _sum_like(dh * x, sc), dgt))
    return dxs, dmods, dw_in, dw_out, dlg, dlb, dhp


def _natural(name, gathered):
    ax = SHARD_AXIS[name]
    t = jnp.moveaxis(gathered, 0, ax)
    return t.reshape(t.shape[:ax] + (t.shape[ax] * t.shape[ax + 1],) + t.shape[ax + 2:])


def _to_shards(name, full):
    ax = SHARD_AXIS[name]
    t = full.reshape(full.shape[:ax] + (N_CHIPS, full.shape[ax] // N_CHIPS) + full.shape[ax + 1:])
    return jnp.moveaxis(t, ax, 0)


def _pack(arrays):
    flat = jnp.concatenate([a.reshape(-1) for a in arrays])
    pad = (-flat.size) % (SUBLANES * LANES)
    return jnp.pad(flat, (0, pad)).reshape(-1, LANES)


def _unpack(packed, like):
    flat = packed.reshape(-1)
    out, at = [], 0
    for a in like:
        out.append(flat[at:at + a.size].reshape(a.shape))
        at += a.size
    return out


def _mod_rows(mod, i):
    return tuple(mod[:, 3 * i + j][:, None, :] for j in range(3))


def kernel(x, c, ctx, c_ctx, w_ada, b_ada, ln_g, ln_b, ffn_w1, ffn_w3, ffn_w2, w_in, w_out, gdn_conv_w, gdn_a_log, gdn_dt_bias, gdn_norm_w, q_norm_w, k_norm_w, s5_lam_re, s5_lam_im, s5_log_dt, s5_b_re, s5_b_im, s5_c_re, s5_c_im, s5_d, glu_w, glu_b, loss_target, m_c_ctx, m_w_ada, m_b_ada, m_ln_g, m_ln_b, m_ffn_w1, m_ffn_w3, m_ffn_w2, m_w_in, m_w_out, m_gdn_conv_w, m_gdn_a_log, m_gdn_dt_bias, m_gdn_norm_w, m_q_norm_w, m_k_norm_w, m_s5_lam_re, m_s5_lam_im, m_s5_log_dt, m_s5_b_re, m_s5_b_im, m_s5_c_re, m_s5_c_im, m_s5_d, m_glu_w, m_glu_b, v_c_ctx, v_w_ada, v_b_ada, v_ln_g, v_ln_b, v_ffn_w1, v_ffn_w3, v_ffn_w2, v_w_in, v_w_out, v_gdn_conv_w, v_gdn_a_log, v_gdn_dt_bias, v_gdn_norm_w, v_q_norm_w, v_k_norm_w, v_s5_lam_re, v_s5_lam_im, v_s5_log_dt, v_s5_b_re, v_s5_b_im, v_s5_c_re, v_s5_c_im, v_s5_d, v_glu_w, v_glu_b):
    given = dict(locals())
    w = {n: given[n] for n in WEIGHTS}
    mom = {n: given['m_' + n] for n in WEIGHTS}
    var = {n: given['v_' + n] for n in WEIGHTS}
    depth = w_ada.shape[0]
    bn, t_lat, d = x.shape
    alpha = (2.0 * depth) ** 0.25
    chip = 2 * lax.axis_index("x") + lax.axis_index("y")

    gathered = _gather_weight_shards([w[n].astype(BF16) for n in BIG])
    full = {n: _natural(n, g) for n, g in zip(BIG, gathered)}
    full['w_in'] = jnp.pad(full['w_in'], ((0, 0), (0, 0), (0, IN_COLS_PAD - IN_COLS)))
    small_sh = [w[n] for n in SMALL_SHARDED]
    small_all = _all_gather8(_pack(small_sh))
    for n, parts in zip(SMALL_SHARDED, zip(*[_unpack(small_all[2 * j], small_sh) for j in range(N_CHIPS)])):
        full[n] = _natural(n, jnp.stack(parts))
    for n in REPLICATED:
        full[n] = w[n]

    cos, sin = _axial_rope_tables(t_lat // GRID_W)
    act = jnp.zeros((ADA_ROWS, d), F32).at[:bn].set(jax.nn.silu(c)).at[bn].set(jax.nn.silu(c_ctx))
    xl, xc = x, ctx
    saved = []
    for l in range(depth):
        last = l == depth - 1
        mod = _mm(act, full['w_ada'][l]) + full['b_ada'][l]
        ml, mc = mod[:bn].reshape(bn, N_MOD, d), mod[bn:bn + 1].reshape(1, N_MOD, d)
        hp = {n: full[n][l] for n in HEAD_PARAMS}
        lg, lb = full['ln_g'][l], full['ln_b'][l]
        f1 = (full['ffn_w1'][l], full['ffn_w3'][l], full['ffn_w2'][l])
        x0, xc0 = xl, xc
        x1, xc1 = _ffn_fwd([x0, xc0], [_mod_rows(ml, 0), _mod_rows(mc, 0)], f1[0][0], f1[1][0], f1[2][0],
                           lg[0], lb[0], 0.5, alpha)
        mixed = _mixer_fwd([x1, xc1], [_mod_rows(ml, 1), _mod_rows(mc, 1)], full['w_in'][l], full['w_out'][l],
                           lg[1], lb[1], hp, cos, sin, alpha, not last)
        x2, xc2 = (mixed[0], None) if last else mixed
        streams = [x2] if last else [x2, xc2]
        mods3 = [_mod_rows(ml, 2)] if last else [_mod_rows(ml, 2), _mod_rows(mc, 2)]
        outs = _ffn_fwd(streams, mods3, f1[0][1], f1[1][1], f1[2][1], lg[2], lb[2], 0.5, alpha)
        xl, xc = (outs[0], None) if last else outs
        saved.append((x0, xc0, x1, xc1, x2, xc2, ml, mc))

    err = xl - loss_target
    loss = lax.psum(0.5 * jnp.sum(jnp.mean(err * err, axis=-1)), ("x", "y", "c"))
    dxl, dxc = err / d, None

    grads = {n: [None] * depth for n in WEIGHTS if n != 'c_ctx'}
    dact = jnp.zeros((ADA_ROWS, d), F32)
    for l in reversed(range(depth)):
        last = l == depth - 1
        x0, xc0, x1, xc1, x2, xc2, ml, mc = saved[l]
        hp = {n: full[n][l] for n in HEAD_PARAMS}
        lg, lb = full['ln_g'][l], full['ln_b'][l]
        f1 = (full['ffn_w1'][l], full['ffn_w3'][l], full['ffn_w2'][l])
        streams = [x2] if last else [x2, xc2]
        mods3 = [_mod_rows(ml, 2)] if last else [_mod_rows(ml, 2), _mod_rows(mc, 2)]
        douts = [dxl] if last else [dxl, dxc]
        dxs3, dm3, dw1b, dw3b, dw2b, dlg2, dlb2 = _ffn_bwd(streams, mods3, f1[0][1], f1[1][1], f1[2][1],
                                                            lg[2], lb[2], 0.5, alpha, douts)
        dxs2, dm2, dw_in, dw_out, dlg1, dlb1, dhp = _mixer_bwd(
            [x1, xc1], [_mod_rows(ml, 1), _mod_rows(mc, 1)], full['w_in'][l], full['w_out'][l], lg[1], lb[1],
            hp, cos, sin, alpha, not last, dxs3)
        dxs1, dm1, dw1a, dw3a, dw2a, dlg0, dlb0 = _ffn_bwd(
            [x0, xc0], [_mod_rows(ml, 0), _mod_rows(mc, 0)], f1[0][0], f1[1][0], f1[2][0], lg[0], lb[0], 0.5, alpha,
            dxs2)
        dxl, dxc = dxs1
        zero3 = tuple(jnp.zeros((1, 1, d), F32) for _ in range(3))
        dm3c = zero3 if last else dm3[1]
        dmod_l = jnp.concatenate([t for grp in (dm1[0], dm2[0], dm3[0]) for t in grp], axis=1).reshape(bn, N_MOD * d)
        dmod_c = jnp.concatenate([t for grp in (dm1[1], dm2[1], dm3c) for t in grp], axis=1).reshape(1, N_MOD * d)
        dmod = jnp.concatenate([dmod_l, dmod_c, jnp.zeros((ADA_ROWS - bn - 1, N_MOD * d), F32)], axis=0)
        grads['w_ada'][l] = _mm(act, dmod, 'tn')
        grads['b_ada'][l] = jnp.sum(dmod, axis=0)
        dact = dact + _mm(dmod, full['w_ada'][l], 'nt')
        grads['ffn_w1'][l] = jnp.stack([dw1a, dw1b])
        grads['ffn_w3'][l] = jnp.stack([dw3a, dw3b])
        grads['ffn_w2'][l] = jnp.stack([dw2a, dw2b])
        grads['w_in'][l] = dw_in[:, :IN_COLS]
        grads['w_out'][l] = dw_out
        grads['ln_g'][l] = jnp.stack([dlg0, dlg1, dlg2])
        grads['ln_b'][l] = jnp.stack([dlb0, dlb1, dlb2])
        for n in HEAD_PARAMS:
            grads[n][l] = dhp[n]
    grad = {n: jnp.stack(g) for n, g in grads.items()}
    sig = jax.nn.sigmoid(c_ctx)
    grad['c_ctx'] = dact[bn] * (sig * (1.0 + c_ctx * (1.0 - sig)))

    half = depth // 2
    laid = []
    for n in BIG:
        s = _to_shards(n, grad[n])
        s = s.reshape((N_CHIPS, 2, half) + s.shape[2:])
        laid.append(jnp.moveaxis(s, 1, 0))
    theirs = _swap_halves(laid)
    pair = [_add_sibling(g, r) for g, r in zip(laid, theirs)]
    landed = _scatter_partials(pair)
    mine = [_add_chips(p, r) for p, r in zip(pair, landed)]
    reduced = dict(zip(BIG, _share_halves(mine)))

    small_names = REPLICATED + SMALL_SHARDED
    small_grads = [grad[n] for n in small_names]
    summed = _unpack(_sum_devices(_all_gather8(_pack(small_grads))), small_grads)
    for n, g in zip(small_names, summed):
        if n in SMALL_SHARDED:
            ax = SHARD_AXIS[n]
            width = g.shape[ax] // N_CHIPS
            g = lax.dynamic_slice_in_dim(g, chip * width, width, axis=ax)
        reduced[n] = g

    delta, new_m, new_v = {}, {}, {}
    for n in BIG:
        delta[n], new_m[n], new_v[n] = _adamw(w[n], reduced[n], mom[n], var[n])
    packs = [_pack([t[n] for n in small_names]) for t in (w, reduced, mom, var)]
    like = [w[n] for n in small_names]
    for res, packed in zip((delta, new_m, new_v), _adamw(*packs)):
        res.update(zip(small_names, _unpack(packed, like)))

    return (loss, dxl, *[reduced[n] for n in WEIGHTS], *[delta[n] for n in WEIGHTS],
            *[new_m[n] for n in WEIGHTS], *[new_v[n] for n in WEIGHTS])
```

```python
import functools
import math

import jax
import jax.numpy as jnp
from jax import lax
from jax.experimental import pallas as pl
from jax.experimental.pallas import tpu as pltpu

F32 = jnp.float32
BF16 = jnp.bfloat16
MESH_IDS = pl.DeviceIdType.MESH
ANY = pl.BlockSpec(memory_space=pl.ANY)
N_CHIPS = 4
N_DEV = 8
LANES = 128
SUBLANES = 8
ADA_ROWS = 128
VMEM_LIMIT = 48 * 1024 * 1024

D_MODEL = 1024
GRID_W = 64
GDN_HEADS = 6
GDN_DK = 64
GDN_DV = 64
GDN_W = GDN_HEADS * GDN_DV
GDN_QKV = GDN_HEADS * (2 * GDN_DK + GDN_DV)
CONV_K = 5
CHUNK = 64
ATT_HEADS = 6
ATT_KV_HEADS = 2
ATT_DH = 64
ATT_W = ATT_HEADS * ATT_DH
ATT_GROUP = ATT_HEADS // ATT_KV_HEADS
Q_BLOCK = 128
ROPE_THETA = 10000.0
ROPE_PAIRS = ATT_DH // 4
S5_GROUPS = 16
S5_GH = 16
S5_P = 64
S5_W = S5_GROUPS * S5_GH
S5_LC = 32
S5_ROWS = SUBLANES
S5_CW = S5_LC * S5_GH
S5_SW = 2 * S5_P
GDN_PREP_CHUNKS = 6
GDN_SCAN_CHAINS = 16
HI = lax.Precision.HIGHEST
BF16X3 = lax.Precision.HIGH
N_MOD = 9
EPS = 1e-6
OFF_GDN_Z = GDN_QKV
OFF_GDN_B = OFF_GDN_Z + GDN_W
OFF_GDN_A = OFF_GDN_B + 2 * GDN_HEADS
OFF_ATT_Q = OFF_GDN_A + 2 * GDN_HEADS
OFF_ATT_K = OFF_ATT_Q + ATT_W
OFF_ATT_V = OFF_ATT_K + ATT_KV_HEADS * ATT_DH
OFF_S5 = OFF_ATT_V + ATT_KV_HEADS * ATT_DH
IN_COLS = OFF_S5 + S5_W
IN_COLS_PAD = 2560
IN_CUTS = (OFF_GDN_Z, OFF_GDN_B, OFF_GDN_A, OFF_ATT_Q, OFF_ATT_K, OFF_ATT_V, OFF_S5)

ADAM_LR = 0.001
ADAM_B1 = 0.9
ADAM_B2 = 0.999
ADAM_EPS = 1e-08
ADAM_WD = 0.01
ADAM_STEP = 10

WEIGHTS = ['c_ctx', 'w_ada', 'b_ada', 'ln_g', 'ln_b', 'ffn_w1', 'ffn_w3', 'ffn_w2', 'w_in', 'w_out', 'gdn_conv_w',
           'gdn_a_log', 'gdn_dt_bias', 'gdn_norm_w', 'q_norm_w', 'k_norm_w', 's5_lam_re', 's5_lam_im', 's5_log_dt',
           's5_b_re', 's5_b_im', 's5_c_re', 's5_c_im', 's5_d', 'glu_w', 'glu_b']
BIG = ['w_ada', 'ffn_w1', 'ffn_w3', 'ffn_w2', 'w_in', 'w_out']
SMALL_SHARDED = ['ln_g', 'ln_b', 'gdn_conv_w', 'glu_w']
SHARD_AXIS = {'w_ada': 2, 'ffn_w1': 3, 'ffn_w3': 3, 'ffn_w2': 2, 'w_in': 2, 'w_out': 1,
              'ln_g': 2, 'ln_b': 2, 'gdn_conv_w': 2, 'glu_w': 1}
REPLICATED = [n for n in WEIGHTS if n not in BIG and n not in SMALL_SHARDED]
HEAD_PARAMS = ['gdn_conv_w', 'gdn_a_log', 'gdn_dt_bias', 'gdn_norm_w', 'q_norm_w', 'k_norm_w', 's5_lam_re',
               's5_lam_im', 's5_log_dt', 's5_b_re', 's5_b_im', 's5_c_re', 's5_c_im', 's5_d', 'glu_w', 'glu_b']


def _place():
    return lax.axis_index("x"), lax.axis_index("y"), lax.axis_index("c")


def _pick(n, cands):
    for t in cands:
        if n % t == 0:
            return t
    return n


def _remote(src, dst, send_sem, recv_sem, to):
    return pltpu.make_async_remote_copy(src_ref=src, dst_ref=dst, send_sem=send_sem, recv_sem=recv_sem,
                                        device_id=to, device_id_type=MESH_IDS)


def _gather_weight_shards(shards):
    n = len(shards)
    halves = [s.shape[0] // 2 for s in shards]

    def body(*refs):
        ins, outs = refs[:n], refs[n:2 * n]
        send_sems, recv_sems, local_sems = refs[2 * n:]
        x, y, c = _place()
        chip = 2 * x + y
        sibling = (x, y, 1 - c)
        others = [(1 - x, y), (x, 1 - y), (1 - x, 1 - y)]
        local, sends = [], []
        for i in range(n):
            h = halves[i]
            own = pltpu.make_async_copy(ins[i], outs[i].at[chip], local_sems.at[i])
            own.start()
            local.append(own)
            for j, (px, py) in enumerate(others):
                k = 6 * i + j
                cp = _remote(ins[i].at[pl.ds(c * h, h)], outs[i].at[chip, pl.ds(c * h, h)],
                             send_sems.at[k], recv_sems.at[k], (px, py, c))
                cp.start()
                sends.append(cp)
        for i in range(n):
            h = halves[i]
            for j, (px, py) in enumerate(others):
                slab = outs[i].at[2 * px + py, pl.ds(c * h, h)]
                _remote(slab, slab, send_sems.at[6 * i + j], recv_sems.at[6 * i + j], (px, py, c)).wait_recv()
                fw = _remote(slab, slab, send_sems.at[6 * i + 3 + j], recv_sems.at[6 * i + 3 + j], sibling)
                fw.start()
                sends.append(fw)
        for i in range(n):
            h = halves[i]
            for j, (px, py) in enumerate(others):
                slab = outs[i].at[2 * px + py, pl.ds((1 - c) * h, h)]
                _remote(slab, slab, send_sems.at[6 * i + 3 + j], recv_sems.at[6 * i + 3 + j], sibling).wait_recv()
        for cp in sends:
            cp.wait_send()
        for cp in local:
            cp.wait()

    return pl.pallas_call(
        body, name="gather_weight_shards",
        out_shape=[jax.ShapeDtypeStruct((N_CHIPS,) + s.shape, s.dtype) for s in shards],
        in_specs=[ANY] * n, out_specs=[ANY] * n,
        scratch_shapes=[pltpu.SemaphoreType.DMA((6 * n,)), pltpu.SemaphoreType.DMA((6 * n,)),
                        pltpu.SemaphoreType.DMA((n,))],
    )(*shards)


def _all_gather8(v):
    def body(v_ref, out_ref, send_sems, recv_sems, local_sem):
        x, y, c = _place()
        me, sibling = (x, y, c), (x, y, 1 - c)
        chips = [(1 - x, y), (x, 1 - y), (1 - x, 1 - y)]

        def slot(px, py, pc):
            return out_ref.at[4 * px + 2 * py + pc]

        def copy(k, block, to, src=None):
            return _remote(slot(*block) if src is None else src, slot(*block), send_sems.at[k], recv_sems.at[k], to)

        mine = pltpu.make_async_copy(v_ref, slot(*me), local_sem)
        mine.start()
        first = [copy(0, me, sibling, src=v_ref)]
        first += [copy(1 + j, me, (*chip, c), src=v_ref) for j, chip in enumerate(chips)]
        for cp in first:
            cp.start()
        passed = [copy(4 + j, (*chip, c), sibling) for j, chip in enumerate(chips)]
        for j, chip in enumerate(chips):
            copy(1 + j, (*chip, c), me).wait_recv()
            passed[j].start()
        copy(0, sibling, me).wait_recv()
        for j, chip in enumerate(chips):
            copy(4 + j, (*chip, 1 - c), me).wait_recv()
        for cp in first + passed:
            cp.wait_send()
        mine.wait()

    return pl.pallas_call(
        body, name="all_gather8",
        out_shape=jax.ShapeDtypeStruct((N_DEV,) + v.shape, v.dtype),
        in_specs=[ANY], out_specs=ANY,
        scratch_shapes=[pltpu.SemaphoreType.DMA((7,)), pltpu.SemaphoreType.DMA((7,)), pltpu.SemaphoreType.DMA],
    )(v)


def _swap_halves(grads):
    n = len(grads)

    def body(*refs):
        ins, outs = refs[:n], refs[n:2 * n]
        send_sems, recv_sems = refs[2 * n:]
        x, y, c = _place()
        cps = [_remote(ins[i].at[1 - c], outs[i], send_sems.at[i], recv_sems.at[i], (x, y, 1 - c)) for i in range(n)]
        for cp in cps:
            cp.start()
        for cp in cps:
            cp.wait()

    return pl.pallas_call(
        body, name="swap_halves",
        out_shape=[jax.ShapeDtypeStruct(g.shape[1:], g.dtype) for g in grads],
        in_specs=[ANY] * n, out_specs=[ANY] * n,
        scratch_shapes=[pltpu.SemaphoreType.DMA((n,)), pltpu.SemaphoreType.DMA((n,))],
    )(*grads)


def _scatter_partials(parts):
    n = len(parts)

    def body(*refs):
        ins, outs = refs[:n], refs[n:2 * n]
        send_sems, recv_sems = refs[2 * n:]
        x, y, c = _place()
        others = [(1 - x, y), (x, 1 - y), (1 - x, 1 - y)]
        cps = []
        for i in range(n):
            for j, (px, py) in enumerate(others):
                k = 3 * i + j
                cps.append(_remote(ins[i].at[2 * px + py], outs[i].at[j], send_sems.at[k], recv_sems.at[k], (px, py, c)))
        for cp in cps:
            cp.start()
        for cp in cps:
            cp.wait()

    return pl.pallas_call(
        body, name="scatter_partials",
        out_shape=[jax.ShapeDtypeStruct((3,) + p.shape[1:], p.dtype) for p in parts],
        in_specs=[ANY] * n, out_specs=[ANY] * n,
        scratch_shapes=[pltpu.SemaphoreType.DMA((3 * n,)), pltpu.SemaphoreType.DMA((3 * n,))],
    )(*parts)


def _share_halves(halves):
    n = len(halves)

    def body(*refs):
        ins, outs = refs[:n], refs[n:2 * n]
        send_sems, recv_sems, local_sems = refs[2 * n:]
        x, y, c = _place()
        loc, cps = [], []
        for i in range(n):
            h = ins[i].shape[0]
            dst = outs[i].at[pl.ds(c * h, h)]
            loc.append(pltpu.make_async_copy(ins[i], dst, local_sems.at[i]))
            cps.append(_remote(ins[i], dst, send_sems.at[i], recv_sems.at[i], (x, y, 1 - c)))
        for cp in loc + cps:
            cp.start()
        for i in range(n):
            h = ins[i].shape[0]
            theirs = outs[i].at[pl.ds((1 - c) * h, h)]
            _remote(theirs, theirs, send_sems.at[i], recv_sems.at[i], (x, y, 1 - c)).wait_recv()
        for cp in cps:
            cp.wait_send()
        for cp in loc:
            cp.wait()

    return pl.pallas_call(
        body, name="share_halves",
        out_shape=[jax.ShapeDtypeStruct((2 * p.shape[0],) + p.shape[1:], p.dtype) for p in halves],
        in_specs=[ANY] * n, out_specs=[ANY] * n,
        scratch_shapes=[pltpu.SemaphoreType.DMA((n,)), pltpu.SemaphoreType.DMA((n,)), pltpu.SemaphoreType.DMA((n,))],
    )(*halves)


def _row_tile(rows, cols, n_arrays):
    budget = (VMEM_LIMIT // 3) // (2 * n_arrays * 4 * max(cols, LANES))
    for t in (1024, 512, 256, 128, 64, 32, 16, 8):
        if t <= budget and rows % t == 0:
            return t
    return rows


def _add_sibling(grad, recv):
    cols = grad.shape[-1]
    rows = recv.size // cols
    g3 = grad.reshape(2, rows, cols)
    r2 = recv.reshape(rows, cols)
    tr = _row_tile(rows, cols, 4)

    def body(c_ref, g_ref, r_ref, o_ref, lo_ref):
        total = g_ref[...] + r_ref[...]
        o_ref[...] = total
        lo_ref[...] = total.astype(BF16)

    spec = pl.BlockSpec((tr, cols), lambda i, c_ref: (i, 0))
    out, lo = pl.pallas_call(
        body, name="add_sibling",
        grid_spec=pltpu.PrefetchScalarGridSpec(
            num_scalar_prefetch=1, grid=(rows // tr,),
            in_specs=[pl.BlockSpec((None, tr, cols), lambda i, c_ref: (c_ref[0], i, 0)), spec],
            out_specs=[spec, spec]),
        out_shape=[jax.ShapeDtypeStruct((rows, cols), F32), jax.ShapeDtypeStruct((rows, cols), BF16)],
        compiler_params=pltpu.CompilerParams(vmem_limit_bytes=VMEM_LIMIT),
    )(lax.axis_index("c").astype(jnp.int32).reshape(1), g3, r2)
    return out.reshape(recv.shape), lo.reshape(recv.shape)


def _add_chips(part, recv):
    cols = part.shape[-1]
    rows = part[0].size // cols
    p3 = part.reshape(N_CHIPS, rows, cols)
    r3 = recv.reshape(3, rows, cols)
    tr = _row_tile(rows, cols, 5)

    def body(chip_ref, p_ref, r0_ref, r1_ref, r2_ref, o_ref):
        o_ref[...] = ((p_ref[...] + r0_ref[...].astype(F32)) + r1_ref[...].astype(F32)) + r2_ref[...].astype(F32)

    def recv_spec(j):
        return pl.BlockSpec((None, tr, cols), lambda i, chip_ref: (j, i, 0))

    chip = (2 * lax.axis_index("x") + lax.axis_index("y")).astype(jnp.int32).reshape(1)
    out = pl.pallas_call(
        body, name="add_chips",
        grid_spec=pltpu.PrefetchScalarGridSpec(
            num_scalar_prefetch=1, grid=(rows // tr,),
            in_specs=[pl.BlockSpec((None, tr, cols), lambda i, chip_ref: (chip_ref[0], i, 0)),
                      recv_spec(0), recv_spec(1), recv_spec(2)],
            out_specs=pl.BlockSpec((tr, cols), lambda i, chip_ref: (i, 0))),
        out_shape=jax.ShapeDtypeStruct((rows, cols), F32),
        compiler_params=pltpu.CompilerParams(vmem_limit_bytes=VMEM_LIMIT),
    )(chip, p3, r3, r3, r3)
    return out.reshape(part.shape[1:])


def _sum_devices(gathered):
    _, rows, cols = gathered.shape
    tr = _row_tile(rows, cols, 9)

    def body(g_ref, o_ref):
        acc = g_ref[0]
        for k in range(1, N_DEV):
            acc = acc + g_ref[k]
        o_ref[...] = acc

    return pl.pallas_call(
        body, name="sum_devices", grid=(rows // tr,),
        in_specs=[pl.BlockSpec((N_DEV, tr, cols), lambda i: (0, i, 0))],
        out_specs=pl.BlockSpec((tr, cols), lambda i: (i, 0)),
        out_shape=jax.ShapeDtypeStruct((rows, cols), F32),
        compiler_params=pltpu.CompilerParams(vmem_limit_bytes=VMEM_LIMIT),
    )(gathered)


def _adamw(w, g, m, v):
    shape = w.shape
    cols = shape[-1]
    rows = w.size // cols
    tr = _row_tile(rows, cols, 7)

    def body(w_ref, g_ref, m_ref, v_ref, d_ref, nm_ref, nv_ref):
        gv = g_ref[...]
        nm = ADAM_B1 * m_ref[...] + (1.0 - ADAM_B1) * gv
        nv = ADAM_B2 * v_ref[...] + (1.0 - ADAM_B2) * (gv * gv)
        m_hat = nm / (1.0 - ADAM_B1 ** ADAM_STEP)
        v_hat = nv / (1.0 - ADAM_B2 ** ADAM_STEP)
        d_ref[...] = -ADAM_LR * (m_hat / (jnp.sqrt(v_hat) + ADAM_EPS) + ADAM_WD * w_ref[...])
        nm_ref[...] = nm
        nv_ref[...] = nv

    spec = pl.BlockSpec((tr, cols), lambda i: (i, 0))
    outs = pl.pallas_call(
        body, name="adamw", grid=(rows // tr,),
        in_specs=[spec] * 4, out_specs=[spec] * 3,
        out_shape=[jax.ShapeDtypeStruct((rows, cols), F32)] * 3,
        compiler_params=pltpu.CompilerParams(vmem_limit_bytes=VMEM_LIMIT),
    )(*[t.reshape(rows, cols) for t in (w, g, m, v)])
    return tuple(o.reshape(shape) for o in outs)


_DOT_DIMS = {'nn': (((1,), (0,)), ((), ())), 'nt': (((1,), (1,)), ((), ())), 'tn': (((0,), (0,)), ((), ()))}


def _mm(a, b, mode='nn', a_gate=None, init=None):
    if mode == 'nn':
        (m, k), (_, n) = a.shape, b.shape
    elif mode == 'nt':
        (m, k), (n, _) = a.shape, b.shape
    else:
        (k, m), (_, n) = a.shape, b.shape
    tn = _pick(n, (1408, 1280, 1024, 512, 256, 128))
    if mode == 'tn':
        tm = _pick(m, (1408, 1024, 512, 256, 128))
        tk = _pick(k, (512, 256, 128, 64, 32, 16, 8))
    else:
        tm = _pick(m, (1024, 512, 256, 128, 64, 32, 16, 8))
        tk = _pick(k, (1408, 1280, 1024, 512, 256, 128))
    nk = k // tk

    gated, seeded = a_gate is not None, init is not None

    def body(*refs):
        a_ref, b_ref = refs[0], refs[1 + gated]
        o_ref, acc_ref = refs[-2], refs[-1]
        step = pl.program_id(2)

        @pl.when(step == 0)
        def _():
            acc_ref[...] = refs[2 + gated][...] if seeded else jnp.zeros_like(acc_ref)

        left = a_ref[...]
        if gated:
            left = left * jax.nn.sigmoid(left) * refs[1][...]
        acc_ref[...] += lax.dot_general(left.astype(BF16), b_ref[...].astype(BF16), _DOT_DIMS[mode],
                                        preferred_element_type=F32)

        @pl.when(step == nk - 1)
        def _():
            o_ref[...] = acc_ref[...]

    if mode == 'tn':
        a_spec = pl.BlockSpec((tk, tm), lambda i, j, s: (s, i))
    else:
        a_spec = pl.BlockSpec((tm, tk), lambda i, j, s: (i, s))
    if mode == 'nt':
        b_spec = pl.BlockSpec((tn, tk), lambda i, j, s: (j, s))
    else:
        b_spec = pl.BlockSpec((tk, tn), lambda i, j, s: (s, j))
    o_spec = pl.BlockSpec((tm, tn), lambda i, j, s: (i, j))
    operands = [a] + ([a_gate] if gated else []) + [b] + ([init] if seeded else [])
    return pl.pallas_call(
        body, name=f"mm_{mode}{'_gated' if gated else ''}{'_seeded' if seeded else ''}_{m}x{k}x{n}",
        grid=(m // tm, n // tn, nk),
        in_specs=[a_spec] * (1 + gated) + [b_spec] + [o_spec] * seeded, out_specs=o_spec,
        out_shape=jax.ShapeDtypeStruct((m, n), F32),
        scratch_shapes=[pltpu.VMEM((tm, tn), F32)],
        compiler_params=pltpu.CompilerParams(dimension_semantics=("parallel", "parallel", "arbitrary"),
                                             vmem_limit_bytes=VMEM_LIMIT),
    )(*operands)


class _Stream:
    def __init__(self, bn, t_ctx, t_lat, d):
        self.bn, self.n, self.d = bn, t_ctx + t_lat, d
        self.tr = _pick(math.gcd(t_ctx, t_lat), (256, 128, 64, 32, 16, 8))
        self.ctx_tiles = t_ctx // self.tr
        self.grid = (bn, self.n // self.tr)
        nct, rows = self.ctx_tiles, bn
        self.tok = pl.BlockSpec((None, self.tr, d), lambda b, i: (b, i, 0))
        self.mod = pl.BlockSpec((None, N_MOD, d), lambda b, i: (jnp.where(i < nct, rows, b), 0, 0))
        self.vec = pl.BlockSpec((1, d), lambda b, i: (0, 0))
        self.part = pl.BlockSpec((None, None, 1, d), lambda b, i: (b, jnp.where(i < nct, 0, 1), 0, 0))
        self.per_example = pl.BlockSpec((None, 1, d), lambda b, i: (b, 0, 0))
        self.tok_shape = jax.ShapeDtypeStruct((bn, self.n, d), F32)
        self.part_shape = jax.ShapeDtypeStruct((bn, 2, 1, d), F32)
        self.example_shape = jax.ShapeDtypeStruct((bn, 1, d), F32)
        self.params = pltpu.CompilerParams(dimension_semantics=("parallel", "arbitrary"), vmem_limit_bytes=VMEM_LIMIT)

    def starts_part(self, i):
        return (i == 0) | (i == self.ctx_tiles)


def _modulate(st, x, mod, k):
    def body(x_ref, m_ref, o_ref):
        o_ref[...] = x_ref[...] * (1.0 + m_ref[3 * k + 1:3 * k + 2, :]) + m_ref[3 * k:3 * k + 1, :]

    return pl.pallas_call(body, name=f"modulate_{k}", grid=st.grid, in_specs=[st.tok, st.mod], out_specs=st.tok,
                          out_shape=st.tok_shape, compiler_params=st.params)(x, mod)


def _norm_stats(z):
    mu = jnp.mean(z, axis=-1, keepdims=True)
    zc = z - mu
    rstd = lax.rsqrt(jnp.mean(zc * zc, axis=-1, keepdims=True) + EPS)
    return zc * rstd, rstd


def _post_norm(st, x, y, mod, k, lg, lb, rw, alpha):
    def body(x_ref, y_ref, m_ref, g_ref, b_ref, o_ref):
        xhat, _ = _norm_stats(alpha * x_ref[...] + (rw * m_ref[3 * k + 2:3 * k + 3, :]) * y_ref[...])
        o_ref[...] = xhat * g_ref[...] + b_ref[...]

    return pl.pallas_call(body, name=f"post_norm_{k}", grid=st.grid,
                          in_specs=[st.tok, st.tok, st.mod, st.vec, st.vec], out_specs=st.tok,
                          out_shape=st.tok_shape, compiler_params=st.params)(x, y, mod, lg[None], lb[None])


def _post_norm_bwd(st, x, y, dout, mod, k, lg, rw, alpha):
    def body(x_ref, y_ref, do_ref, m_ref, g_ref, dxr_ref, dy_ref, dgate_ref, dlg_ref, dlb_ref):
        i = pl.program_id(1)
        gate = rw * m_ref[3 * k + 2:3 * k + 3, :]
        yv, dov = y_ref[...], do_ref[...]
        xhat, rstd = _norm_stats(alpha * x_ref[...] + gate * yv)
        dxhat = dov * g_ref[...]
        dz = rstd * (dxhat - jnp.mean(dxhat, axis=-1, keepdims=True)
                     - xhat * jnp.mean(dxhat * xhat, axis=-1, keepdims=True))
        dxr_ref[...] = alpha * dz
        dy_ref[...] = gate * dz

        @pl.when(i == 0)
        def _():
            dlg_ref[...] = jnp.zeros_like(dlg_ref)
            dlb_ref[...] = jnp.zeros_like(dlb_ref)

        @pl.when(st.starts_part(i))
        def _():
            dgate_ref[...] = jnp.zeros_like(dgate_ref)

        dlg_ref[...] += jnp.sum(dov * xhat, axis=0, keepdims=True)
        dlb_ref[...] += jnp.sum(dov, axis=0, keepdims=True)
        dgate_ref[...] += jnp.sum(rw * yv * dz, axis=0, keepdims=True)

    return pl.pallas_call(
        body, name=f"post_norm_bwd_{k}", grid=st.grid, in_specs=[st.tok, st.tok, st.tok, st.mod, st.vec],
        out_specs=[st.tok, st.tok, st.part, st.per_example, st.per_example],
        out_shape=[st.tok_shape, st.tok_shape, st.part_shape, st.example_shape, st.example_shape],
        compiler_params=st.params)(x, y, dout, mod, lg[None])


def _modulate_bwd(st, dh, x, dxr, mod, k):
    def body(dh_ref, x_ref, dxr_ref, m_ref, dx_ref, dsh_ref, dsc_ref):
        dhv = dh_ref[...]
        dx_ref[...] = dxr_ref[...] + dhv * (1.0 + m_ref[3 * k + 1:3 * k + 2, :])

        @pl.when(st.starts_part(pl.program_id(1)))
        def _():
            dsh_ref[...] = jnp.zeros_like(dsh_ref)
            dsc_ref[...] = jnp.zeros_like(dsc_ref)

        dsh_ref[...] += jnp.sum(dhv, axis=0, keepdims=True)
        dsc_ref[...] += jnp.sum(dhv * x_ref[...], axis=0, keepdims=True)

    return pl.pallas_call(
        body, name=f"modulate_bwd_{k}", grid=st.grid, in_specs=[st.tok, st.tok, st.tok, st.mod],
        out_specs=[st.tok, st.part, st.part], out_shape=[st.tok_shape, st.part_shape, st.part_shape],
        compiler_params=st.params)(dh, x, dxr, mod)


def _swiglu_bwd(ds, a, b):
    rows, cols = a.shape
    tr = _row_tile(rows, cols, 5)

    def body(ds_ref, a_ref, b_ref, da_ref, db_ref):
        av, dsv = a_ref[...], ds_ref[...]
        sig = jax.nn.sigmoid(av)
        da_ref[...] = dsv * b_ref[...] * (sig * (1.0 + av * (1.0 - sig)))
        db_ref[...] = dsv * (av * sig)

    spec = pl.BlockSpec((tr, cols), lambda i: (i, 0))
    return pl.pallas_call(body, name="swiglu_bwd", grid=(rows // tr,), in_specs=[spec] * 3, out_specs=[spec] * 2,
                          out_shape=[jax.ShapeDtypeStruct(a.shape, F32)] * 2,
                          compiler_params=pltpu.CompilerParams(vmem_limit_bytes=VMEM_LIMIT))(ds, a, b)


def _rms_norm(x, w):
    return x * lax.rsqrt(jnp.mean(x * x, axis=-1, keepdims=True) + EPS) * w


def _l2_normalize(x):
    return x * lax.rsqrt(jnp.sum(x * x, axis=-1, keepdims=True) + EPS)


def _ffn_fwd(st, x, mod, k, w1, w3, w2, lg, lb, alpha):
    rows = st.bn * st.n
    h = _modulate(st, x, mod, k).reshape(rows, st.d)
    a, b = _mm(h, w1), _mm(h, w3)
    y = _mm(a, w2, a_gate=b).reshape(st.bn, st.n, st.d)
    return _post_norm(st, x, y, mod, k, lg, lb, 0.5, alpha), (h, a, b, y)


def _ffn_bwd(st, x, mod, k, w1, w3, w2, lg, alpha, kept, dout):
    h, a, b, y = kept
    rows = st.bn * st.n
    dxr, dy, dgate, dlg, dlb = _post_norm_bwd(st, x, y, dout, mod, k, lg, 0.5, alpha)
    dy = dy.reshape(rows, st.d)
    ds = _mm(dy, w2, 'nt')
    dw2 = _mm(a, dy, 'tn', a_gate=b)
    da, db = _swiglu_bwd(ds, a, b)
    dw1 = _mm(h, da, 'tn')
    dw3 = _mm(h, db, 'tn')
    dh = _mm(db, w3, 'nt', init=_mm(da, w1, 'nt')).reshape(st.bn, st.n, st.d)
    dx, dsh, dsc = _modulate_bwd(st, dh, x, dxr, mod, k)
    return dx, (dsh, dsc, dgate), dw1, dw3, dw2, jnp.sum(dlg, axis=(0, 1)), jnp.sum(dlb, axis=(0, 1))


def _dwconv_centred(x, w):
    pad = CONV_K // 2
    return lax.conv_general_dilated(
        x, w[:, None, :].astype(x.dtype), window_strides=(1,), padding=[(pad, pad)],
        dimension_numbers=('NWC', 'WIO', 'NWC'), feature_group_count=x.shape[-1])


def _largest_divisor(n, cap):
    return max(t for t in range(1, cap + 1) if n % t == 0)


def _bmm(a, b):
    return lax.dot_general(a, b, (((2,), (1,)), ((0,), (0,))), precision=BF16X3, preferred_element_type=F32)


_BATCHED_DIMS = {'nn': (((2,), (1,)), ((0,), (0,))), 'nt': (((2,), (2,)), ((0,), (0,))),
                 'tn': (((1,), (1,)), ((0,), (0,)))}


def _bdot(a, b, mode):
    return lax.dot_general(a.astype(BF16), b.astype(BF16), _BATCHED_DIMS[mode], preferred_element_type=F32)


@functools.partial(jax.custom_vjp, nondiff_argnums=(2,))
def _bmm_lo(a, b, mode):
    return _bdot(a, b, mode)


def _bmm_lo_fwd(a, b, mode):
    return _bdot(a, b, mode), (a, b)


def _bmm_lo_bwd(mode, res, ct):
    a, b = res
    if mode == 'nn':
        return _bdot(ct, b, 'nt'), _bdot(a, ct, 'tn')
    if mode == 'nt':
        return _bdot(ct, b, 'nn'), _bdot(ct, a, 'tn')
    return _bdot(b, ct, 'nt'), _bdot(a, ct, 'nn')


_bmm_lo.defvjp(_bmm_lo_fwd, _bmm_lo_bwd)


def _gdn_prep(q, k, v, gc, gr, beta):
    n = q.shape[0]
    row = lax.broadcasted_iota(jnp.int32, (n, CHUNK, CHUNK), 1)
    col = lax.broadcasted_iota(jnp.int32, (n, CHUNK, CHUNK), 2)
    incl, strict = row >= col, row > col
    decay = jnp.where(incl, jnp.exp(jnp.where(incl, gc - gr, 0.0)), 0.0)
    kb = k * beta
    lower = jnp.where(strict, _bmm_lo(kb, k, 'nt') * decay, 0.0)
    eye = jnp.where(row == col, 1.0, 0.0)
    inv, power = eye - lower, lower
    for _ in range(5):
        power = _bmm(power, power)
        inv = _bmm(inv, eye + power)
    u = _bmm(inv, v * beta)
    w = _bmm(inv, kb * jnp.exp(gc))
    intra = jnp.where(incl, _bmm_lo(q, k, 'nt') * decay, 0.0)
    is_last = lax.broadcasted_iota(jnp.int32, (n, CHUNK, 1), 1) == CHUNK - 1
    g_last = jnp.sum(jnp.where(is_last, gc, 0.0), axis=1, keepdims=True)
    return u, w, intra, q * jnp.exp(gc), k * jnp.exp(g_last - gc), jnp.exp(g_last) * jnp.ones((n, 1, GDN_DV), F32)


def _gdn_step(state, u, w, intra, qg, kd, gl):
    v_new = u - _bmm_lo(w, state, 'nn')
    o = _bmm_lo(qg, state, 'nn') + _bmm_lo(intra, v_new, 'nn')
    return o, state * gl + _bmm_lo(kd, v_new, 'tn')


def _gdn_prep_specs(cb):
    idx = lambda c, i: (c, i, 0, 0)
    mat = pl.BlockSpec((None, cb, CHUNK, GDN_DK), idx)
    colv = pl.BlockSpec((None, cb, CHUNK, 1), idx)
    rowv = pl.BlockSpec((None, cb, 1, CHUNK), idx)
    return mat, colv, rowv


def _gdn_prep_fwd_call(q, k, v, gc, gr, beta):
    chains, nchunks = q.shape[:2]
    cb = _largest_divisor(nchunks, GDN_PREP_CHUNKS)
    mat, colv, rowv = _gdn_prep_specs(cb)

    def body(q_ref, k_ref, v_ref, gc_ref, gr_ref, b_ref, *out_refs):
        outs = _gdn_prep(q_ref[...], k_ref[...], v_ref[...], gc_ref[...], gr_ref[...], b_ref[...])
        for ref, val in zip(out_refs, outs):
            ref[...] = val

    mshape = jax.ShapeDtypeStruct(q.shape, F32)
    return pl.pallas_call(
        body, name="gdn_prep_fwd", grid=(chains, nchunks // cb),
        in_specs=[mat, mat, mat, colv, rowv, colv], out_specs=[mat] * 5 + [rowv],
        out_shape=[mshape] * 5 + [jax.ShapeDtypeStruct(gr.shape, F32)],
        compiler_params=pltpu.CompilerParams(dimension_semantics=("parallel", "parallel"), vmem_limit_bytes=VMEM_LIMIT),
    )(q, k, v, gc, gr, beta)


def _gdn_prep_bwd_call(q, k, v, gc, gr, beta, cts):
    chains, nchunks = q.shape[:2]
    cb = _largest_divisor(nchunks, GDN_PREP_CHUNKS)
    mat, colv, rowv = _gdn_prep_specs(cb)

    def body(q_ref, k_ref, v_ref, gc_ref, gr_ref, b_ref, du, dw, di, dqg, dkd, dgl, *out_refs):
        _, vjp = jax.vjp(_gdn_prep, q_ref[...], k_ref[...], v_ref[...], gc_ref[...], gr_ref[...], b_ref[...])
        grads = vjp((du[...], dw[...], di[...], dqg[...], dkd[...], dgl[...]))
        for ref, val in zip(out_refs, grads):
            ref[...] = val

    return pl.pallas_call(
        body, name="gdn_prep_bwd", grid=(chains, nchunks // cb),
        in_specs=[mat, mat, mat, colv, rowv, colv] + [mat] * 5 + [rowv],
        out_specs=[mat, mat, mat, colv, rowv, colv],
        out_shape=[jax.ShapeDtypeStruct(t.shape, F32) for t in (q, k, v, gc, gr, beta)],
        compiler_params=pltpu.CompilerParams(dimension_semantics=("parallel", "parallel"), vmem_limit_bytes=VMEM_LIMIT),
    )(q, k, v, gc, gr, beta, *cts)


def _gdn_scan_fwd_call(u, w, intra, qg, kd, gl):
    chains, nchunks = u.shape[:2]
    cc = _largest_divisor(chains, GDN_SCAN_CHAINS)
    idx = lambda c, i: (c, i, 0, 0)
    mat = pl.BlockSpec((cc, None, CHUNK, GDN_DK), idx)
    rowv = pl.BlockSpec((cc, None, 1, CHUNK), idx)

    def body(u_ref, w_ref, i_ref, qg_ref, kd_ref, gl_ref, o_ref, hist_ref, state_ref):
        @pl.when(pl.program_id(1) == 0)
        def _():
            state_ref[...] = jnp.zeros_like(state_ref)

        state = state_ref[...]
        hist_ref[...] = state
        o, new = _gdn_step(state, u_ref[...], w_ref[...], i_ref[...], qg_ref[...], kd_ref[...], gl_ref[...])
        o_ref[...] = o
        state_ref[...] = new

    mshape = jax.ShapeDtypeStruct(u.shape, F32)
    return pl.pallas_call(
        body, name="gdn_scan_fwd", grid=(chains // cc, nchunks),
        in_specs=[mat] * 5 + [rowv], out_specs=[mat, mat], out_shape=[mshape, mshape],
        scratch_shapes=[pltpu.VMEM((cc, GDN_DK, GDN_DV), F32)],
        compiler_params=pltpu.CompilerParams(dimension_semantics=("parallel", "arbitrary"), vmem_limit_bytes=VMEM_LIMIT),
    )(u, w, intra, qg, kd, gl)


def _gdn_scan_bwd_call(u, w, intra, qg, kd, gl, hist, do):
    chains, nchunks = u.shape[:2]
    cc = _largest_divisor(chains, GDN_SCAN_CHAINS)
    idx = lambda c, i: (c, nchunks - 1 - i, 0, 0)
    mat = pl.BlockSpec((cc, None, CHUNK, GDN_DK), idx)
    rowv = pl.BlockSpec((cc, None, 1, CHUNK), idx)

    def body(u_ref, w_ref, i_ref, qg_ref, kd_ref, gl_ref, h_ref, do_ref, du, dw, di, dqg, dkd, dgl, dstate_ref):
        @pl.when(pl.program_id(1) == 0)
        def _():
            dstate_ref[...] = jnp.zeros_like(dstate_ref)

        _, vjp = jax.vjp(_gdn_step, h_ref[...], u_ref[...], w_ref[...], i_ref[...], qg_ref[...], kd_ref[...], gl_ref[...])
        grads = vjp((do_ref[...], dstate_ref[...]))
        dstate_ref[...] = grads[0]
        for ref, val in zip((du, dw, di, dqg, dkd, dgl), grads[1:]):
            ref[...] = val

    mshape = jax.ShapeDtypeStruct(u.shape, F32)
    return pl.pallas_call(
        body, name="gdn_scan_bwd", grid=(chains // cc, nchunks),
        in_specs=[mat] * 5 + [rowv, mat, mat], out_specs=[mat] * 5 + [rowv],
        out_shape=[mshape] * 5 + [jax.ShapeDtypeStruct(gl.shape, F32)],
        scratch_shapes=[pltpu.VMEM((cc, GDN_DK, GDN_DV), F32)],
        compiler_params=pltpu.CompilerParams(dimension_semantics=("parallel", "arbitrary"), vmem_limit_bytes=VMEM_LIMIT),
    )(u, w, intra, qg, kd, gl, hist, do)


@jax.custom_vjp
def _gdn_core(q, k, v, gc, gr, beta):
    return _gdn_scan_fwd_call(*_gdn_prep_fwd_call(q, k, v, gc, gr, beta))[0]


def _gdn_core_fwd(q, k, v, gc, gr, beta):
    prep = _gdn_prep_fwd_call(q, k, v, gc, gr, beta)
    o, hist = _gdn_scan_fwd_call(*prep)
    return o, (q, k, v, gc, gr, beta, prep, hist)


def _gdn_core_bwd(res, do):
    q, k, v, gc, gr, beta, prep, hist = res
    cts = _gdn_scan_bwd_call(*prep, hist, do)
    return tuple(_gdn_prep_bwd_call(q, k, v, gc, gr, beta, cts))


_gdn_core.defvjp(_gdn_core_fwd, _gdn_core_bwd)


def _gdn_inputs(qkv, b, a, conv_w, a_log, dt_bias):
    Bn, T, _ = qkv.shape
    qkv = jax.nn.silu(_dwconv_centred(qkv, conv_w))
    q, k, v = jnp.split(qkv, [GDN_HEADS * GDN_DK, 2 * GDN_HEADS * GDN_DK], axis=-1)

    def to_heads(t, d):
        return t.reshape(Bn, T, GDN_HEADS, d).transpose(0, 2, 1, 3)

    def dir_heads(t):
        return t.reshape(Bn, T, 2, GDN_HEADS).transpose(2, 0, 3, 1)

    q = _l2_normalize(to_heads(q, GDN_DK))
    k = _l2_normalize(to_heads(k, GDN_DK))
    v = to_heads(v, GDN_DV)
    beta = jax.nn.sigmoid(dir_heads(b))
    g = -jnp.exp(a_log)[:, None, :, None] * jax.nn.softplus(dir_heads(a) + dt_bias[:, None, :, None])
    return q, k, v, g, beta


def _gdn_gated_out(o, z, norm_w):
    Bn, H, T, dv = o.shape
    o = _rms_norm(o.transpose(0, 2, 1, 3), norm_w)
    o = o * jax.nn.silu(z.reshape(Bn, T, H, dv))
    return o.reshape(Bn, T, H * dv)


def _gdn_group(qkv, z, b, a, qkv_c, z_c, b_c, a_c, conv_w, a_log, dt_bias, norm_w, with_ctx_out):
    q, k, v, g, beta = _gdn_inputs(qkv, b, a, conv_w, a_log, dt_bias)
    qc, kc, vc, gc, betac = _gdn_inputs(qkv_c, b_c, a_c, conv_w, a_log, dt_bias)
    bn, heads, t_lat, _ = q.shape
    t_ctx = qc.shape[2]
    n = t_ctx + t_lat
    nchunks = n // CHUNK
    chains = 2 * bn * heads

    def both(tc, tl):
        return jnp.stack([jnp.concatenate([tc, tl], axis=2), jnp.concatenate([jnp.flip(tc, 2), jnp.flip(tl, 2)], axis=2)])

    def per_dir(tc, tl):
        return jnp.stack([jnp.concatenate([tc[0], tl[0]], axis=2),
                          jnp.concatenate([jnp.flip(tc[1], 2), jnp.flip(tl[1], 2)], axis=2)])

    def mats(t):
        return t.reshape(chains, nchunks, CHUNK, t.shape[-1])

    gcum = jnp.cumsum(per_dir(gc, g).reshape(chains, nchunks, CHUNK), axis=-1)
    o = _gdn_core(mats(both(qc, q) * GDN_DK ** -0.5), mats(both(kc, k)), mats(both(vc, v)),
                  gcum[..., None], gcum[:, :, None, :], per_dir(betac, beta).reshape(chains, nchunks, CHUNK, 1))
    o = o.reshape(2, bn, heads, n, GDN_DV)
    out = _gdn_gated_out(o[0, :, :, t_ctx:] + jnp.flip(o[1, :, :, t_ctx:], 2), z, norm_w)
    if not with_ctx_out:
        return out, None
    return out, _gdn_gated_out(o[0, :, :, :t_ctx] + jnp.flip(o[1, :, :, :t_ctx], 2), z_c, norm_w)


def _axial_rope_tables(rows):
    row = jnp.repeat(jnp.arange(rows), GRID_W)
    col = jnp.tile(jnp.arange(GRID_W), rows)
    inv_freq = ROPE_THETA ** (-jnp.arange(ROPE_PAIRS, dtype=F32) / ROPE_PAIRS)
    ang = jnp.stack([row, col], axis=-1).astype(F32)[..., None] * inv_freq
    return jnp.cos(ang), jnp.sin(ang)


def _rope_2d(x, cos, sin):
    shp = x.shape
    xr = x.reshape(*shp[:-1], 2, 2, ROPE_PAIRS)
    x1, x2 = xr[..., 0, :], xr[..., 1, :]
    bshape = (shp[1],) + (1,) * (x.ndim - 3) + (2, ROPE_PAIRS)
    c, s = cos.reshape(bshape), sin.reshape(bshape)
    out = jnp.stack([x1 * c - x2 * s, x2 * c + x1 * s], axis=-2)
    return out.reshape(shp)


def _attn_specs(tq, tk):
    q_spec = pl.BlockSpec((None, None, ATT_GROUP, tq, ATT_DH), lambda b, h, i: (b, h, 0, i, 0))
    kv_spec = pl.BlockSpec((None, None, tk, ATT_DH), lambda b, h, i: (b, h, 0, 0))
    return q_spec, kv_spec


def _softmax_rows(q, k):
    s = lax.dot_general(q.astype(BF16), k.astype(BF16), (((1,), (1,)), ((), ())), preferred_element_type=F32)
    s = s * (ATT_DH ** -0.5)
    e = jnp.exp(s - jnp.max(s, axis=-1, keepdims=True))
    return e / jnp.sum(e, axis=-1, keepdims=True)


def _attn_fwd_call(q, k, v):
    bn, _, _, t_q, _ = q.shape
    t_k = k.shape[2]
    tq = _pick(t_q, (Q_BLOCK, 64, 32, 16, 8))
    q_spec, kv_spec = _attn_specs(tq, t_k)

    def body(q_ref, k_ref, v_ref, o_ref):
        p = _softmax_rows(q_ref[...].reshape(ATT_GROUP * tq, ATT_DH), k_ref[...])
        o = lax.dot_general(p.astype(BF16), v_ref[...].astype(BF16), (((1,), (0,)), ((), ())), preferred_element_type=F32)
        o_ref[...] = o.reshape(ATT_GROUP, tq, ATT_DH)

    return pl.pallas_call(
        body, name=f"attention_fwd_{t_q}x{t_k}", grid=(bn, ATT_KV_HEADS, t_q // tq),
        in_specs=[q_spec, kv_spec, kv_spec], out_specs=q_spec,
        out_shape=jax.ShapeDtypeStruct(q.shape, F32),
        compiler_params=pltpu.CompilerParams(dimension_semantics=("parallel", "parallel", "parallel"),
                                             vmem_limit_bytes=VMEM_LIMIT),
    )(q, k, v)


def _attn_bwd_call(q, k, v, do):
    bn, _, _, t_q, _ = q.shape
    t_k = k.shape[2]
    tq = _pick(t_q, (Q_BLOCK, 64, 32, 16, 8))
    q_spec, kv_spec = _attn_specs(tq, t_k)

    def body(q_ref, k_ref, v_ref, do_ref, dq_ref, dk_ref, dv_ref):
        @pl.when(pl.program_id(2) == 0)
        def _():
            dk_ref[...] = jnp.zeros_like(dk_ref)
            dv_ref[...] = jnp.zeros_like(dv_ref)

        qv = q_ref[...].reshape(ATT_GROUP * tq, ATT_DH)
        dov = do_ref[...].reshape(ATT_GROUP * tq, ATT_DH).astype(BF16)
        kb, vb = k_ref[...].astype(BF16), v_ref[...].astype(BF16)
        p = _softmax_rows(qv, k_ref[...])
        dp = lax.dot_general(dov, vb, (((1,), (1,)), ((), ())), preferred_element_type=F32)
        ds = p * (dp - jnp.sum(p * dp, axis=-1, keepdims=True)) * (ATT_DH ** -0.5)
        dsb = ds.astype(BF16)
        dq = lax.dot_general(dsb, kb, (((1,), (0,)), ((), ())), preferred_element_type=F32)
        dq_ref[...] = dq.reshape(ATT_GROUP, tq, ATT_DH)
        dk_ref[...] += lax.dot_general(dsb, qv.astype(BF16), (((0,), (0,)), ((), ())), preferred_element_type=F32)
        dv_ref[...] += lax.dot_general(p.astype(BF16), dov, (((0,), (0,)), ((), ())), preferred_element_type=F32)

    return pl.pallas_call(
        body, name=f"attention_bwd_{t_q}x{t_k}", grid=(bn, ATT_KV_HEADS, t_q // tq),
        in_specs=[q_spec, kv_spec, kv_spec, q_spec], out_specs=[q_spec, kv_spec, kv_spec],
        out_shape=[jax.ShapeDtypeStruct(t.shape, F32) for t in (q, k, v)],
        compiler_params=pltpu.CompilerParams(dimension_semantics=("parallel", "parallel", "arbitrary"),
                                             vmem_limit_bytes=VMEM_LIMIT),
    )(q, k, v, do)


@jax.custom_vjp
def _attn_core(q, k, v):
    return _attn_fwd_call(q, k, v)


def _attn_core_fwd(q, k, v):
    return _attn_fwd_call(q, k, v), (q, k, v)


def _attn_core_bwd(res, do):
    return tuple(_attn_bwd_call(*res, do))


_attn_core.defvjp(_attn_core_fwd, _attn_core_bwd)


def _attention_group(q, k, v, q_c, k_c, v_c, q_norm_w, k_norm_w, cos, sin, with_ctx_out):
    Bn, T, _ = q.shape
    Tc = q_c.shape[1]
    q = _rope_2d(_rms_norm(q.reshape(Bn, T, ATT_KV_HEADS, ATT_GROUP, ATT_DH), q_norm_w), cos, sin)
    k = _rope_2d(_rms_norm(k.reshape(Bn, T, ATT_KV_HEADS, ATT_DH), k_norm_w), cos, sin)
    v = v.reshape(Bn, T, ATT_KV_HEADS, ATT_DH)
    qc = _rms_norm(q_c.reshape(Bn, Tc, ATT_KV_HEADS, ATT_GROUP, ATT_DH), q_norm_w)
    kc = _rms_norm(k_c.reshape(Bn, Tc, ATT_KV_HEADS, ATT_DH), k_norm_w)
    vc = v_c.reshape(Bn, Tc, ATT_KV_HEADS, ATT_DH)
    keys = jnp.concatenate([kc, k], axis=1).transpose(0, 2, 1, 3)
    vals = jnp.concatenate([vc, v], axis=1).transpose(0, 2, 1, 3)
    o = _attn_core(q.transpose(0, 2, 3, 1, 4), keys, vals)
    o = o.transpose(0, 3, 1, 2, 4).reshape(Bn, T, ATT_W)
    if not with_ctx_out:
        return o, None
    o_c = _attn_core(qc.transpose(0, 2, 3, 1, 4), kc.transpose(0, 2, 1, 3), vc.transpose(0, 2, 1, 3))
    return o, o_c.transpose(0, 3, 1, 2, 4).reshape(Bn, Tc, ATT_W)


def _s5_operators(lam_re, lam_im, log_dt, b_re, b_im, c_re, c_im):
    lc = S5_LC
    dt = jnp.exp(log_dt)[..., None]
    ar, ai = lam_re * dt, lam_im * dt
    mag = jnp.exp(ar)
    lbr, lbi = mag * jnp.cos(ai), mag * jnp.sin(ai)
    den = lam_re * lam_re + lam_im * lam_im
    fr = ((lbr - 1.0) * lam_re + lbi * lam_im) / den
    fi = (lbi * lam_re - (lbr - 1.0) * lam_im) / den
    bbr = fr[..., None] * b_re - fi[..., None] * b_im
    bbi = fr[..., None] * b_im + fi[..., None] * b_re
    m = jnp.arange(lc + 1, dtype=F32)[:, None, None, None]
    pmag = jnp.exp(m * ar)
    pr, pi = pmag * jnp.cos(m * ai), pmag * jnp.sin(m * ai)
    cpr = c_re[None] * pr[:, :, :, None, :] - c_im[None] * pi[:, :, :, None, :]
    cpi = c_re[None] * pi[:, :, :, None, :] + c_im[None] * pr[:, :, :, None, :]
    kern = (jnp.einsum('mdghp,dgpk->mdghk', cpr[:lc], bbr, precision=HI)
            - jnp.einsum('mdghp,dgpk->mdghk', cpi[:lc], bbi, precision=HI))
    tail = kern.shape[1:]
    lags = jnp.concatenate([jnp.zeros((lc - 1,) + tail, F32), kern, jnp.zeros((1,) + tail, F32)], axis=0)
    toep = jnp.tile(lags, (lc,) + (1,) * len(tail))[:lc * (2 * lc - 1)].reshape((lc, 2 * lc - 1) + tail)[:, lc - 1:]
    tm = toep.transpose(2, 3, 0, 5, 1, 4).reshape(2, S5_GROUPS, S5_CW, S5_CW)
    prr, pir = pr[lc - 1::-1], pi[lc - 1::-1]
    mre = prr[..., None] * bbr[None] - pir[..., None] * bbi[None]
    mim = prr[..., None] * bbi[None] + pir[..., None] * bbr[None]
    mm = jnp.concatenate([mre, mim], axis=3).transpose(1, 2, 0, 4, 3).reshape(2, S5_GROUPS, S5_CW, S5_SW)
    nm = jnp.concatenate([cpr[1:], -cpi[1:]], axis=-1)
    nm = nm.transpose(1, 2, 4, 0, 3).reshape(2, S5_GROUPS, S5_SW, S5_CW)
    a1 = jnp.concatenate([pr[lc], pr[lc]], axis=-1)
    a2 = jnp.concatenate([-pi[lc], pi[lc]], axis=-1)
    lam_rows = jnp.concatenate([a1[:, :, None], a2[:, :, None],
                                jnp.zeros((2, S5_GROUPS, SUBLANES - 2, S5_SW), F32)], axis=2)
    return tm, mm, nm, lam_rows


def _dot_hi(a, b, dims):
    return lax.dot_general(a, b, (dims, ((), ())), precision=BF16X3, preferred_element_type=F32)


def _s5_blocks(rows):
    def blk(r, c):
        return pl.BlockSpec((None, None, r, c), lambda d, g: (d, g, 0, 0))
    return (blk(rows, S5_CW), blk(S5_CW, S5_CW), blk(S5_CW, S5_SW), blk(S5_SW, S5_CW), blk(SUBLANES, S5_SW),
            blk(rows, S5_SW))


def _s5_core_fwd_call(s, tm, mm, nm, lam_rows):
    rows = s.shape[2]
    chunks = rows // S5_ROWS
    seq, top, mop, nop, lop, sta = _s5_blocks(rows)

    def body(s_ref, t_ref, m_ref, n_ref, l_ref, y_ref, h_ref, e_ref):
        sv = s_ref[...]
        e_ref[...] = _dot_hi(sv, m_ref[...], ((1,), (0,)))
        a1, a2 = l_ref[0:1, :], l_ref[1:2, :]
        h_ref[0:S5_ROWS, :] = jnp.zeros((S5_ROWS, S5_SW), F32)

        def step(k, carry):
            at = pl.multiple_of((k - 1) * S5_ROWS, S5_ROWS)
            prev = h_ref[pl.ds(at, S5_ROWS), :]
            new = a1 * prev + a2 * pltpu.roll(prev, S5_P, 1) + e_ref[pl.ds(at, S5_ROWS), :]
            h_ref[pl.ds(pl.multiple_of(k * S5_ROWS, S5_ROWS), S5_ROWS), :] = new
            return carry

        lax.fori_loop(1, chunks, step, 0)
        y_ref[...] = _dot_hi(sv, t_ref[...], ((1,), (0,))) + _dot_hi(h_ref[...], n_ref[...], ((1,), (0,)))

    return pl.pallas_call(
        body, name="s5_chunks_fwd", grid=(2, S5_GROUPS),
        in_specs=[seq, top, mop, nop, lop], out_specs=[seq, sta],
        out_shape=[jax.ShapeDtypeStruct(s.shape, F32), jax.ShapeDtypeStruct(s.shape[:3] + (S5_SW,), F32)],
        scratch_shapes=[pltpu.VMEM((rows, S5_SW), F32)],
        compiler_params=pltpu.CompilerParams(dimension_semantics=("parallel", "parallel"), vmem_limit_bytes=VMEM_LIMIT),
    )(s, tm, mm, nm, lam_rows)


def _s5_core_bwd_call(s, dy, tm, mm, nm, lam_rows, hin):
    rows = s.shape[2]
    chunks = rows // S5_ROWS
    seq, top, mop, nop, lop, sta = _s5_blocks(rows)

    def body(s_ref, dy_ref, t_ref, m_ref, n_ref, l_ref, h_ref, ds_ref, dt_ref, dm_ref, dn_ref, dl_ref, dh_ref, de_ref):
        sv, dyv, hv = s_ref[...], dy_ref[...], h_ref[...]
        dh_ref[...] = _dot_hi(dyv, n_ref[...], ((1,), (1,)))
        a1, a2 = l_ref[0:1, :], l_ref[1:2, :]
        last = (chunks - 1) * S5_ROWS
        de_ref[last:last + S5_ROWS, :] = jnp.zeros((S5_ROWS, S5_SW), F32)

        def step(i, g):
            k = chunks - 2 - i
            at = pl.multiple_of(k * S5_ROWS, S5_ROWS)
            de_ref[pl.ds(at, S5_ROWS), :] = g
            return dh_ref[pl.ds(at, S5_ROWS), :] + a1 * g + pltpu.roll(a2 * g, S5_P, 1)

        lax.fori_loop(0, chunks - 1, step, dh_ref[last:last + S5_ROWS, :])
        dev = de_ref[...]
        ds_ref[...] = _dot_hi(dyv, t_ref[...], ((1,), (1,))) + _dot_hi(dev, m_ref[...], ((1,), (1,)))
        dt_ref[...] = _dot_hi(sv, dyv, ((0,), (0,)))
        dm_ref[...] = _dot_hi(sv, dev, ((0,), (0,)))
        dn_ref[...] = _dot_hi(hv, dyv, ((0,), (0,)))
        da1 = jnp.sum(hv * dev, axis=0, keepdims=True)
        da2 = jnp.sum(pltpu.roll(hv, S5_P, 1) * dev, axis=0, keepdims=True)
        dl_ref[...] = jnp.concatenate([da1, da2, jnp.zeros((SUBLANES - 2, S5_SW), F32)], axis=0)

    return pl.pallas_call(
        body, name="s5_chunks_bwd", grid=(2, S5_GROUPS),
        in_specs=[seq, seq, top, mop, nop, lop, sta], out_specs=[seq, top, mop, nop, lop],
        out_shape=[jax.ShapeDtypeStruct(t.shape, F32) for t in (s, tm, mm, nm, lam_rows)],
        scratch_shapes=[pltpu.VMEM((rows, S5_SW), F32), pltpu.VMEM((rows, S5_SW), F32)],
        compiler_params=pltpu.CompilerParams(dimension_semantics=("parallel", "parallel"), vmem_limit_bytes=VMEM_LIMIT),
    )(s, dy, tm, mm, nm, lam_rows, hin)


@jax.custom_vjp
def _s5_core(s, tm, mm, nm, lam_rows):
    return _s5_core_fwd_call(s, tm, mm, nm, lam_rows)[0]


def _s5_core_fwd(s, tm, mm, nm, lam_rows):
    y, hin = _s5_core_fwd_call(s, tm, mm, nm, lam_rows)
    return y, (s, tm, mm, nm, lam_rows, hin)


def _s5_core_bwd(res, dy):
    s, tm, mm, nm, lam_rows, hin = res
    return tuple(_s5_core_bwd_call(s, dy, tm, mm, nm, lam_rows, hin))


_s5_core.defvjp(_s5_core_fwd, _s5_core_bwd)


def _s5_group(u, u_c, lam_re, lam_im, log_dt, b_re, b_im, c_re, c_im, d_skip, glu_w, glu_b, with_ctx_out):
    bn, t_lat, _ = u.shape
    t_ctx = u_c.shape[1]
    n = t_ctx + t_lat
    chunks = n // S5_LC
    assert bn <= S5_ROWS and t_ctx % S5_LC == 0 and t_lat % S5_LC == 0
    seqs = jnp.stack([jnp.concatenate([u_c, u], axis=1),
                      jnp.concatenate([jnp.flip(u_c, 1), jnp.flip(u, 1)], axis=1)])
    s = seqs.reshape(2, bn, chunks, S5_LC, S5_GROUPS, S5_GH).transpose(0, 4, 2, 1, 3, 5)
    s = jnp.pad(s, ((0, 0), (0, 0), (0, 0), (0, S5_ROWS - bn), (0, 0), (0, 0)))
    s = s.reshape(2, S5_GROUPS, chunks * S5_ROWS, S5_CW)
    y = _s5_core(s, *_s5_operators(lam_re, lam_im, log_dt, b_re, b_im, c_re, c_im))
    y = y.reshape(2, S5_GROUPS, chunks, S5_ROWS, S5_LC, S5_GH)[:, :, :, :bn]
    y = y.transpose(0, 3, 2, 4, 1, 5).reshape(2, bn, n, S5_W)
    yl = d_skip * u + y[0, :, t_ctx:] + jnp.flip(y[1, :, t_ctx:], 1)

    def glu(yy):
        zz = jax.nn.gelu(yy)
        return zz * jax.nn.sigmoid(zz @ glu_w + glu_b)

    if not with_ctx_out:
        return glu(yl), None
    yc = d_skip * u_c + y[0, :, :t_ctx] + jnp.flip(y[1, :, :t_ctx], 1)
    return glu(yl), glu(yc)


def _heads(proj, proj_c, hp, cos, sin, with_ctx_out):
    g_qkv, g_z, g_b, g_a, a_q, a_k, a_v, s_u = jnp.split(proj[..., :IN_COLS], list(IN_CUTS), axis=-1)
    c_qkv, c_z, c_b, c_a, c_q, c_k, c_v, c_u = jnp.split(proj_c[..., :IN_COLS], list(IN_CUTS), axis=-1)
    o_gdn, oc_gdn = _gdn_group(g_qkv, g_z, g_b, g_a, c_qkv, c_z, c_b, c_a, hp['gdn_conv_w'], hp['gdn_a_log'],
                               hp['gdn_dt_bias'], hp['gdn_norm_w'], with_ctx_out)
    o_att, oc_att = _attention_group(a_q, a_k, a_v, c_q, c_k, c_v, hp['q_norm_w'], hp['k_norm_w'], cos, sin,
                                     with_ctx_out)
    o_s5, oc_s5 = _s5_group(s_u, c_u, hp['s5_lam_re'], hp['s5_lam_im'], hp['s5_log_dt'], hp['s5_b_re'], hp['s5_b_im'],
                            hp['s5_c_re'], hp['s5_c_im'], hp['s5_d'], hp['glu_w'], hp['glu_b'], with_ctx_out)
    o = jnp.concatenate([o_gdn, o_att, o_s5], axis=-1)
    if not with_ctx_out:
        return (o,)
    return o, jnp.concatenate([oc_gdn, oc_att, oc_s5], axis=-1)


def _mixer_fwd(st, x, mod, t_ctx, w_in, w_out, lg, lb, hp, cos, sin, alpha):
    rows = st.bn * st.n
    h = _modulate(st, x, mod, 1).reshape(rows, st.d)
    proj = _mm(h, w_in).reshape(st.bn, st.n, IN_COLS_PAD)
    (o_lat, o_ctx), heads_vjp = jax.vjp(lambda p, pc, hp_: _heads(p, pc, hp_, cos, sin, True),
                                         proj[:, t_ctx:], proj[:, :t_ctx], hp)
    o = jnp.concatenate([o_ctx, o_lat], axis=1).reshape(rows, st.d)
    y = _mm(o, w_out).reshape(st.bn, st.n, st.d)
    return _post_norm(st, x, y, mod, 1, lg, lb, 1.0, alpha), (h, heads_vjp, o, y)


def _mixer_bwd(st, x, mod, t_ctx, w_in, w_out, lg, alpha, kept, dout):
    h, heads_vjp, o, y = kept
    rows = st.bn * st.n
    dxr, dy, dgate, dlg, dlb = _post_norm_bwd(st, x, y, dout, mod, 1, lg, 1.0, alpha)
    dy = dy.reshape(rows, st.d)
    dw_out = _mm(o, dy, 'tn')
    do = _mm(dy, w_out, 'nt').reshape(st.bn, st.n, st.d)
    dproj, dproj_c, dhp = heads_vjp((do[:, t_ctx:], do[:, :t_ctx]))
    dp = jnp.concatenate([dproj_c, dproj], axis=1).reshape(rows, IN_COLS_PAD)
    dw_in = _mm(h, dp, 'tn')
    dh = _mm(dp, w_in, 'nt').reshape(st.bn, st.n, st.d)
    dx, dsh, dsc = _modulate_bwd(st, dh, x, dxr, mod, 1)
    return dx, (dsh, dsc, dgate), dw_in, dw_out, jnp.sum(dlg, axis=(0, 1)), jnp.sum(dlb, axis=(0, 1)), dhp


def _natural(name, gathered):
    ax = SHARD_AXIS[name]
    t = jnp.moveaxis(gathered, 0, ax)
    return t.reshape(t.shape[:ax] + (t.shape[ax] * t.shape[ax + 1],) + t.shape[ax + 2:])


def _to_shards(name, full):
    ax = SHARD_AXIS[name]
    t = full.reshape(full.shape[:ax] + (N_CHIPS, full.shape[ax] // N_CHIPS) + full.shape[ax + 1:])
    return jnp.moveaxis(t, ax, 0)


def _pack(arrays):
    flat = jnp.concatenate([a.reshape(-1) for a in arrays])
    pad = (-flat.size) % (SUBLANES * LANES)
    return jnp.pad(flat, (0, pad)).reshape(-1, LANES)


def _unpack(packed, like):
    flat = packed.reshape(-1)
    out, at = [], 0
    for a in like:
        out.append(flat[at:at + a.size].reshape(a.shape))
        at += a.size
    return out


def kernel(x, c, ctx, c_ctx, w_ada, b_ada, ln_g, ln_b, ffn_w1, ffn_w3, ffn_w2, w_in, w_out, gdn_conv_w, gdn_a_log, gdn_dt_bias, gdn_norm_w, q_norm_w, k_norm_w, s5_lam_re, s5_lam_im, s5_log_dt, s5_b_re, s5_b_im, s5_c_re, s5_c_im, s5_d, glu_w, glu_b, loss_target, m_c_ctx, m_w_ada, m_b_ada, m_ln_g, m_ln_b, m_ffn_w1, m_ffn_w3, m_ffn_w2, m_w_in, m_w_out, m_gdn_conv_w, m_gdn_a_log, m_gdn_dt_bias, m_gdn_norm_w, m_q_norm_w, m_k_norm_w, m_s5_lam_re, m_s5_lam_im, m_s5_log_dt, m_s5_b_re, m_s5_b_im, m_s5_c_re, m_s5_c_im, m_s5_d, m_glu_w, m_glu_b, v_c_ctx, v_w_ada, v_b_ada, v_ln_g, v_ln_b, v_ffn_w1, v_ffn_w3, v_ffn_w2, v_w_in, v_w_out, v_gdn_conv_w, v_gdn_a_log, v_gdn_dt_bias, v_gdn_norm_w, v_q_norm_w, v_k_norm_w, v_s5_lam_re, v_s5_lam_im, v_s5_log_dt, v_s5_b_re, v_s5_b_im, v_s5_c_re, v_s5_c_im, v_s5_d, v_glu_w, v_glu_b):
    given = dict(locals())
    w = {n: given[n] for n in WEIGHTS}
    mom = {n: given['m_' + n] for n in WEIGHTS}
    var = {n: given['v_' + n] for n in WEIGHTS}
    depth = w_ada.shape[0]
    bn, t_lat, d = x.shape
    alpha = (2.0 * depth) ** 0.25
    chip = 2 * lax.axis_index("x") + lax.axis_index("y")

    gathered = _gather_weight_shards([w[n].astype(BF16) for n in BIG])
    full = {n: _natural(n, g) for n, g in zip(BIG, gathered)}
    full['w_in'] = jnp.pad(full['w_in'], ((0, 0), (0, 0), (0, IN_COLS_PAD - IN_COLS)))
    small_sh = [w[n] for n in SMALL_SHARDED]
    small_all = _all_gather8(_pack(small_sh))
    for n, parts in zip(SMALL_SHARDED, zip(*[_unpack(small_all[2 * j], small_sh) for j in range(N_CHIPS)])):
        full[n] = _natural(n, jnp.stack(parts))
    for n in REPLICATED:
        full[n] = w[n]

    cos, sin = _axial_rope_tables(t_lat // GRID_W)
    act = jnp.zeros((ADA_ROWS, d), F32).at[:bn].set(jax.nn.silu(c)).at[bn].set(jax.nn.silu(c_ctx))
    t_ctx = ctx.shape[1]
    st = _Stream(bn, t_ctx, t_lat, d)
    xs = jnp.concatenate([ctx, x], axis=1)
    saved = []
    for l in range(depth):
        mod = (_mm(act, full['w_ada'][l]) + full['b_ada'][l]).reshape(ADA_ROWS, N_MOD, d)
        hp = {n: full[n][l] for n in HEAD_PARAMS}
        lg, lb = full['ln_g'][l], full['ln_b'][l]
        f1 = (full['ffn_w1'][l], full['ffn_w3'][l], full['ffn_w2'][l])
        x0 = xs
        x1, kept1 = _ffn_fwd(st, x0, mod, 0, f1[0][0], f1[1][0], f1[2][0], lg[0], lb[0], alpha)
        x2, kept2 = _mixer_fwd(st, x1, mod, t_ctx, full['w_in'][l], full['w_out'][l], lg[1], lb[1], hp, cos, sin, alpha)
        xs, kept3 = _ffn_fwd(st, x2, mod, 2, f1[0][1], f1[1][1], f1[2][1], lg[2], lb[2], alpha)
        saved.append((x0, x1, x2, mod, kept1, kept2, kept3))

    err = xs[:, t_ctx:] - loss_target
    loss = lax.psum(0.5 * jnp.sum(jnp.mean(err * err, axis=-1)), ("x", "y", "c"))
    dxs = jnp.concatenate([jnp.zeros((bn, t_ctx, d), F32), err / d], axis=1)

    grads = {n: [None] * depth for n in WEIGHTS if n != 'c_ctx'}
    dact = jnp.zeros((ADA_ROWS, d), F32)
    for l in reversed(range(depth)):
        x0, x1, x2, mod, kept1, kept2, kept3 = saved[l]
        lg = full['ln_g'][l]
        f1 = (full['ffn_w1'][l], full['ffn_w3'][l], full['ffn_w2'][l])
        dx2, dm3, dw1b, dw3b, dw2b, dlg2, dlb2 = _ffn_bwd(st, x2, mod, 2, f1[0][1], f1[1][1], f1[2][1], lg[2], alpha,
                                                           kept3, dxs)
        dx1, dm2, dw_in, dw_out, dlg1, dlb1, dhp = _mixer_bwd(st, x1, mod, t_ctx, full['w_in'][l], full['w_out'][l],
                                                              lg[1], alpha, kept2, dx2)
        dxs, dm1, dw1a, dw3a, dw2a, dlg0, dlb0 = _ffn_bwd(st, x0, mod, 0, f1[0][0], f1[1][0], f1[2][0], lg[0], alpha,
                                                          kept1, dx1)
        parts = jnp.stack([t[:, :, 0] for grp in (dm1, dm2, dm3) for t in grp], axis=2)
        dmod = jnp.concatenate([parts[:, 1], jnp.sum(parts[:, 0], axis=0, keepdims=True),
                                jnp.zeros((ADA_ROWS - bn - 1, N_MOD, d), F32)], axis=0).reshape(ADA_ROWS, N_MOD * d)
        grads['w_ada'][l] = _mm(act, dmod, 'tn')
        grads['b_ada'][l] = jnp.sum(dmod, axis=0)
        dact = dact + _mm(dmod, full['w_ada'][l], 'nt')
        grads['ffn_w1'][l] = jnp.stack([dw1a, dw1b])
        grads['ffn_w3'][l] = jnp.stack([dw3a, dw3b])
        grads['ffn_w2'][l] = jnp.stack([dw2a, dw2b])
        grads['w_in'][l] = dw_in[:, :IN_COLS]
        grads['w_out'][l] = dw_out
        grads['ln_g'][l] = jnp.stack([dlg0, dlg1, dlg2])
        grads['ln_b'][l] = jnp.stack([dlb0, dlb1, dlb2])
        for n in HEAD_PARAMS:
            grads[n][l] = dhp[n]
    grad = {n: jnp.stack(g) for n, g in grads.items()}
    sig = jax.nn.sigmoid(c_ctx)
    grad['c_ctx'] = dact[bn] * (sig * (1.0 + c_ctx * (1.0 - sig)))

    half = depth // 2
    laid = []
    for n in BIG:
        s = _to_shards(n, grad[n])
        s = s.reshape((N_CHIPS, 2, half) + s.shape[2:])
        laid.append(jnp.moveaxis(s, 1, 0))
    theirs = _swap_halves(laid)
    pair = [_add_sibling(g, r) for g, r in zip(laid, theirs)]
    landed = _scatter_partials([lo for _, lo in pair])
    mine = [_add_chips(p, r) for (p, _), r in zip(pair, landed)]
    reduced = dict(zip(BIG, _share_halves(mine)))

    small_names = REPLICATED + SMALL_SHARDED
    small_grads = [grad[n] for n in small_names]
    summed = _unpack(_sum_devices(_all_gather8(_pack(small_grads))), small_grads)
    for n, g in zip(small_names, summed):
        if n in SMALL_SHARDED:
            ax = SHARD_AXIS[n]
            width = g.shape[ax] // N_CHIPS
            g = lax.dynamic_slice_in_dim(g, chip * width, width, axis=ax)
        reduced[n] = g

    delta, new_m, new_v = {}, {}, {}
    for n in BIG:
        delta[n], new_m[n], new_v[n] = _adamw(w[n], reduced[n], mom[n], var[n])
    packs = [_pack([t[n] for n in small_names]) for t in (w, reduced, mom, var)]
    like = [w[n] for n in small_names]
    for res, packed in zip((delta, new_m, new_v), _adamw(*packs)):
        res.update(zip(small_names, _unpack(packed, like)))

    return (loss, dxs[:, t_ctx:], *[reduced[n] for n in WEIGHTS], *[delta[n] for n in WEIGHTS],
            *[new_m[n] for n in WEIGHTS], *[new_v[n] for n in WEIGHTS])
```

```python
import functools
import math

import jax
import jax.numpy as jnp
from jax import lax
from jax.experimental import pallas as pl
from jax.experimental.pallas import tpu as pltpu

F32 = jnp.float32
BF16 = jnp.bfloat16
MESH_IDS = pl.DeviceIdType.MESH
ANY = pl.BlockSpec(memory_space=pl.ANY)
N_CHIPS = 4
N_DEV = 8
LANES = 128
SUBLANES = 8
ADA_ROWS = 128
VMEM_LIMIT = 48 * 1024 * 1024

D_MODEL = 1024
GRID_W = 64
GDN_HEADS = 6
GDN_DK = 64
GDN_DV = 64
GDN_W = GDN_HEADS * GDN_DV
GDN_QKV = GDN_HEADS * (2 * GDN_DK + GDN_DV)
CONV_K = 5
CHUNK = 64
ATT_HEADS = 6
ATT_KV_HEADS = 2
ATT_DH = 64
ATT_W = ATT_HEADS * ATT_DH
ATT_GROUP = ATT_HEADS // ATT_KV_HEADS
Q_BLOCK = 128
ROPE_THETA = 10000.0
ROPE_PAIRS = ATT_DH // 4
S5_GROUPS = 16
S5_GH = 16
S5_P = 64
S5_W = S5_GROUPS * S5_GH
S5_LC = 32
S5_ROWS = SUBLANES
S5_CW = S5_LC * S5_GH
S5_SW = 2 * S5_P
S5W_LC = 8
S5W_RW = S5W_LC * S5_W
S5W_SW = S5_GROUPS * S5_SW
S5W_TILE = 4 * S5_SW
GDN_PREP_CHUNKS = 6
GDN_SCAN_CHAINS = 16
HI = lax.Precision.HIGHEST
BF16X3 = lax.Precision.HIGH
N_MOD = 9
EPS = 1e-6
OFF_GDN_Z = GDN_QKV
OFF_GDN_B = OFF_GDN_Z + GDN_W
OFF_GDN_A = OFF_GDN_B + 2 * GDN_HEADS
OFF_ATT_Q = OFF_GDN_A + 2 * GDN_HEADS
OFF_ATT_K = OFF_ATT_Q + ATT_W
OFF_ATT_V = OFF_ATT_K + ATT_KV_HEADS * ATT_DH
OFF_S5 = OFF_ATT_V + ATT_KV_HEADS * ATT_DH
IN_COLS = OFF_S5 + S5_W
IN_COLS_PAD = 2560
IN_CUTS = (OFF_GDN_Z, OFF_GDN_B, OFF_GDN_A, OFF_ATT_Q, OFF_ATT_K, OFF_ATT_V, OFF_S5)

ADAM_LR = 0.001
ADAM_B1 = 0.9
ADAM_B2 = 0.999
ADAM_EPS = 1e-08
ADAM_WD = 0.01
ADAM_STEP = 10

WEIGHTS = ['c_ctx', 'w_ada', 'b_ada', 'ln_g', 'ln_b', 'ffn_w1', 'ffn_w3', 'ffn_w2', 'w_in', 'w_out', 'gdn_conv_w',
           'gdn_a_log', 'gdn_dt_bias', 'gdn_norm_w', 'q_norm_w', 'k_norm_w', 's5_lam_re', 's5_lam_im', 's5_log_dt',
           's5_b_re', 's5_b_im', 's5_c_re', 's5_c_im', 's5_d', 'glu_w', 'glu_b']
BIG = ['w_ada', 'ffn_w1', 'ffn_w3', 'ffn_w2', 'w_in', 'w_out']
SMALL_SHARDED = ['ln_g', 'ln_b', 'gdn_conv_w', 'glu_w']
SHARD_AXIS = {'w_ada': 2, 'ffn_w1': 3, 'ffn_w3': 3, 'ffn_w2': 2, 'w_in': 2, 'w_out': 1,
              'ln_g': 2, 'ln_b': 2, 'gdn_conv_w': 2, 'glu_w': 1}
REPLICATED = [n for n in WEIGHTS if n not in BIG and n not in SMALL_SHARDED]
HEAD_PARAMS = ['gdn_conv_w', 'gdn_a_log', 'gdn_dt_bias', 'gdn_norm_w', 'q_norm_w', 'k_norm_w', 's5_lam_re',
               's5_lam_im', 's5_log_dt', 's5_b_re', 's5_b_im', 's5_c_re', 's5_c_im', 's5_d', 'glu_w', 'glu_b']


def _place():
    return lax.axis_index("x"), lax.axis_index("y"), lax.axis_index("c")


def _pick(n, cands):
    for t in cands:
        if n % t == 0:
            return t
    return n


def _remote(src, dst, send_sem, recv_sem, to):
    return pltpu.make_async_remote_copy(src_ref=src, dst_ref=dst, send_sem=send_sem, recv_sem=recv_sem,
                                        device_id=to, device_id_type=MESH_IDS)


def _gather_weight_shards(shards):
    n = len(shards)
    halves = [s.shape[0] // 2 for s in shards]

    def body(*refs):
        ins, outs = refs[:n], refs[n:2 * n]
        send_sems, recv_sems = refs[2 * n:]
        x, y, c = _place()
        chip = 2 * x + y
        sibling = (x, y, 1 - c)
        others = [(1 - x, y), (x, 1 - y), (1 - x, 1 - y)]
        sends = []
        for i in range(n):
            h = halves[i]
            for j, (px, py) in enumerate(others):
                k = 6 * i + j
                cp = _remote(ins[i].at[pl.ds(c * h, h)], outs[i].at[chip, pl.ds(c * h, h)],
                             send_sems.at[k], recv_sems.at[k], (px, py, c))
                cp.start()
                sends.append(cp)
        for i in range(n):
            h = halves[i]
            for j, (px, py) in enumerate(others):
                slab = outs[i].at[2 * px + py, pl.ds(c * h, h)]
                _remote(slab, slab, send_sems.at[6 * i + j], recv_sems.at[6 * i + j], (px, py, c)).wait_recv()
                fw = _remote(slab, slab, send_sems.at[6 * i + 3 + j], recv_sems.at[6 * i + 3 + j], sibling)
                fw.start()
                sends.append(fw)
        for i in range(n):
            h = halves[i]
            for j, (px, py) in enumerate(others):
                slab = outs[i].at[2 * px + py, pl.ds((1 - c) * h, h)]
                _remote(slab, slab, send_sems.at[6 * i + 3 + j], recv_sems.at[6 * i + 3 + j], sibling).wait_recv()
        for cp in sends:
            cp.wait_send()

    gathered = pl.pallas_call(
        body, name="gather_weight_shards",
        out_shape=[jax.ShapeDtypeStruct((N_CHIPS,) + s.shape, s.dtype) for s in shards],
        in_specs=[ANY] * n, out_specs=[ANY] * n,
        scratch_shapes=[pltpu.SemaphoreType.DMA((6 * n,)), pltpu.SemaphoreType.DMA((6 * n,))],
    )(*shards)
    chip = 2 * lax.axis_index("x") + lax.axis_index("y")
    return [lax.dynamic_update_slice_in_dim(g, s[None], chip, axis=0) for g, s in zip(gathered, shards)]


def _all_gather8(v):
    def body(v_ref, out_ref, send_sems, recv_sems, local_sem):
        x, y, c = _place()
        me, sibling = (x, y, c), (x, y, 1 - c)
        chips = [(1 - x, y), (x, 1 - y), (1 - x, 1 - y)]

        def slot(px, py, pc):
            return out_ref.at[4 * px + 2 * py + pc]

        def copy(k, block, to, src=None):
            return _remote(slot(*block) if src is None else src, slot(*block), send_sems.at[k], recv_sems.at[k], to)

        mine = pltpu.make_async_copy(v_ref, slot(*me), local_sem)
        mine.start()
        first = [copy(0, me, sibling, src=v_ref)]
        first += [copy(1 + j, me, (*chip, c), src=v_ref) for j, chip in enumerate(chips)]
        for cp in first:
            cp.start()
        passed = [copy(4 + j, (*chip, c), sibling) for j, chip in enumerate(chips)]
        for j, chip in enumerate(chips):
            copy(1 + j, (*chip, c), me).wait_recv()
            passed[j].start()
        copy(0, sibling, me).wait_recv()
        for j, chip in enumerate(chips):
            copy(4 + j, (*chip, 1 - c), me).wait_recv()
        for cp in first + passed:
            cp.wait_send()
        mine.wait()

    return pl.pallas_call(
        body, name="all_gather8",
        out_shape=jax.ShapeDtypeStruct((N_DEV,) + v.shape, v.dtype),
        in_specs=[ANY], out_specs=ANY,
        scratch_shapes=[pltpu.SemaphoreType.DMA((7,)), pltpu.SemaphoreType.DMA((7,)), pltpu.SemaphoreType.DMA],
    )(v)


def _swap_halves(grads):
    n = len(grads)

    def body(*refs):
        ins, outs = refs[:n], refs[n:2 * n]
        send_sems, recv_sems = refs[2 * n:]
        x, y, c = _place()
        cps = [_remote(ins[i].at[1 - c], outs[i], send_sems.at[i], recv_sems.at[i], (x, y, 1 - c)) for i in range(n)]
        for cp in cps:
            cp.start()
        for cp in cps:
            cp.wait()

    return pl.pallas_call(
        body, name="swap_halves",
        out_shape=[jax.ShapeDtypeStruct(g.shape[1:], g.dtype) for g in grads],
        in_specs=[ANY] * n, out_specs=[ANY] * n,
        scratch_shapes=[pltpu.SemaphoreType.DMA((n,)), pltpu.SemaphoreType.DMA((n,))],
    )(*grads)


def _scatter_partials(parts):
    n = len(parts)

    def body(*refs):
        ins, outs = refs[:n], refs[n:2 * n]
        send_sems, recv_sems = refs[2 * n:]
        x, y, c = _place()
        others = [(1 - x, y), (x, 1 - y), (1 - x, 1 - y)]
        cps = []
        for i in range(n):
            for j, (px, py) in enumerate(others):
                k = 3 * i + j
                cps.append(_remote(ins[i].at[2 * px + py], outs[i].at[j], send_sems.at[k], recv_sems.at[k], (px, py, c)))
        for cp in cps:
            cp.start()
        for cp in cps:
            cp.wait()

    return pl.pallas_call(
        body, name="scatter_partials",
        out_shape=[jax.ShapeDtypeStruct((3,) + p.shape[1:], p.dtype) for p in parts],
        in_specs=[ANY] * n, out_specs=[ANY] * n,
        scratch_shapes=[pltpu.SemaphoreType.DMA((3 * n,)), pltpu.SemaphoreType.DMA((3 * n,))],
    )(*parts)


def _share_halves(halves):
    n = len(halves)

    def body(*refs):
        ins, outs = refs[:n], refs[n:2 * n]
        send_sems, recv_sems = refs[2 * n:]
        x, y, c = _place()
        cps = [_remote(ins[i], outs[i], send_sems.at[i], recv_sems.at[i], (x, y, 1 - c)) for i in range(n)]
        for cp in cps:
            cp.start()
        for cp in cps:
            cp.wait()

    theirs = pl.pallas_call(
        body, name="share_halves",
        out_shape=[jax.ShapeDtypeStruct(p.shape, p.dtype) for p in halves],
        in_specs=[ANY] * n, out_specs=[ANY] * n,
        scratch_shapes=[pltpu.SemaphoreType.DMA((n,)), pltpu.SemaphoreType.DMA((n,))],
    )(*halves)
    south = lax.axis_index("c") == 0
    return [jnp.concatenate([jnp.where(south, a, b), jnp.where(south, b, a)], axis=0) for a, b in zip(halves, theirs)]


def _row_tile(rows, cols, n_arrays):
    budget = (VMEM_LIMIT // 3) // (2 * n_arrays * 4 * max(cols, LANES))
    for t in (1024, 512, 256, 128, 64, 32, 16, 8):
        if t <= budget and rows % t == 0:
            return t
    return rows


def _add_sibling(grad, recv):
    cols = grad.shape[-1]
    rows = recv.size // cols
    g3 = grad.reshape(2, rows, cols)
    r2 = recv.reshape(rows, cols)
    tr = _row_tile(rows, cols, 4)

    def body(c_ref, g_ref, r_ref, o_ref, lo_ref):
        total = g_ref[...] + r_ref[...]
        o_ref[...] = total
        lo_ref[...] = total.astype(BF16)

    spec = pl.BlockSpec((tr, cols), lambda i, c_ref: (i, 0))
    out, lo = pl.pallas_call(
        body, name="add_sibling",
        grid_spec=pltpu.PrefetchScalarGridSpec(
            num_scalar_prefetch=1, grid=(rows // tr,),
            in_specs=[pl.BlockSpec((None, tr, cols), lambda i, c_ref: (c_ref[0], i, 0)), spec],
            out_specs=[spec, spec]),
        out_shape=[jax.ShapeDtypeStruct((rows, cols), F32), jax.ShapeDtypeStruct((rows, cols), BF16)],
        compiler_params=pltpu.CompilerParams(vmem_limit_bytes=VMEM_LIMIT),
    )(lax.axis_index("c").astype(jnp.int32).reshape(1), g3, r2)
    return out.reshape(recv.shape), lo.reshape(recv.shape)


def _add_chips(part, recv):
    cols = part.shape[-1]
    rows = part[0].size // cols
    p3 = part.reshape(N_CHIPS, rows, cols)
    r3 = recv.reshape(3, rows, cols)
    tr = _row_tile(rows, cols, 5)

    def body(chip_ref, p_ref, r0_ref, r1_ref, r2_ref, o_ref):
        o_ref[...] = ((p_ref[...] + r0_ref[...].astype(F32)) + r1_ref[...].astype(F32)) + r2_ref[...].astype(F32)

    def recv_spec(j):
        return pl.BlockSpec((None, tr, cols), lambda i, chip_ref: (j, i, 0))

    chip = (2 * lax.axis_index("x") + lax.axis_index("y")).astype(jnp.int32).reshape(1)
    out = pl.pallas_call(
        body, name="add_chips",
        grid_spec=pltpu.PrefetchScalarGridSpec(
            num_scalar_prefetch=1, grid=(rows // tr,),
            in_specs=[pl.BlockSpec((None, tr, cols), lambda i, chip_ref: (chip_ref[0], i, 0)),
                      recv_spec(0), recv_spec(1), recv_spec(2)],
            out_specs=pl.BlockSpec((tr, cols), lambda i, chip_ref: (i, 0))),
        out_shape=jax.ShapeDtypeStruct((rows, cols), F32),
        compiler_params=pltpu.CompilerParams(vmem_limit_bytes=VMEM_LIMIT),
    )(chip, p3, r3, r3, r3)
    return out.reshape(part.shape[1:])


def _sum_devices(gathered):
    _, rows, cols = gathered.shape
    tr = _row_tile(rows, cols, 9)

    def body(g_ref, o_ref):
        acc = g_ref[0]
        for k in range(1, N_DEV):
            acc = acc + g_ref[k]
        o_ref[...] = acc

    return pl.pallas_call(
        body, name="sum_devices", grid=(rows // tr,),
        in_specs=[pl.BlockSpec((N_DEV, tr, cols), lambda i: (0, i, 0))],
        out_specs=pl.BlockSpec((tr, cols), lambda i: (i, 0)),
        out_shape=jax.ShapeDtypeStruct((rows, cols), F32),
        compiler_params=pltpu.CompilerParams(vmem_limit_bytes=VMEM_LIMIT),
    )(gathered)


def _adamw(w, g, m, v):
    shape = w.shape
    cols = shape[-1]
    rows = w.size // cols
    tr = _row_tile(rows, cols, 7)

    def body(w_ref, g_ref, m_ref, v_ref, d_ref, nm_ref, nv_ref):
        gv = g_ref[...]
        nm = ADAM_B1 * m_ref[...] + (1.0 - ADAM_B1) * gv
        nv = ADAM_B2 * v_ref[...] + (1.0 - ADAM_B2) * (gv * gv)
        m_hat = nm / (1.0 - ADAM_B1 ** ADAM_STEP)
        v_hat = nv / (1.0 - ADAM_B2 ** ADAM_STEP)
        d_ref[...] = -ADAM_LR * (m_hat / (jnp.sqrt(v_hat) + ADAM_EPS) + ADAM_WD * w_ref[...])
        nm_ref[...] = nm
        nv_ref[...] = nv

    spec = pl.BlockSpec((tr, cols), lambda i: (i, 0))
    outs = pl.pallas_call(
        body, name="adamw", grid=(rows // tr,),
        in_specs=[spec] * 4, out_specs=[spec] * 3,
        out_shape=[jax.ShapeDtypeStruct((rows, cols), F32)] * 3,
        compiler_params=pltpu.CompilerParams(vmem_limit_bytes=VMEM_LIMIT),
    )(*[t.reshape(rows, cols) for t in (w, g, m, v)])
    return tuple(o.reshape(shape) for o in outs)


_DOT_DIMS = {'nn': (((1,), (0,)), ((), ())), 'nt': (((1,), (1,)), ((), ())), 'tn': (((0,), (0,)), ((), ()))}


def _mm(a, b, mode='nn', a_gate=None, init=None, split=False):
    if mode == 'nn':
        (m, k), (_, n) = a.shape, b.shape
    elif mode == 'nt':
        (m, k), (n, _) = a.shape, b.shape
    else:
        (k, m), (_, n) = a.shape, b.shape
    tn = _pick(n, (1408, 1280, 1024, 512, 256, 128))
    if mode == 'tn':
        tm = _pick(m, (1408, 1024, 512, 256, 128))
        tk = _pick(k, (512, 256, 128, 64, 32, 16, 8))
    else:
        tm = _pick(m, (1024, 768, 512, 256, 128, 64, 32, 16, 8))
        tk = _pick(k, (1408, 1280, 1024, 512, 256, 128))
    nk = k // tk

    gated, seeded = a_gate is not None, init is not None

    def body(*refs):
        a_ref, b_ref = refs[0], refs[1 + gated]
        o_ref, acc_ref = refs[-2], refs[-1]
        step = pl.program_id(2)

        @pl.when(step == 0)
        def _():
            acc_ref[...] = refs[2 + gated][...] if seeded else jnp.zeros_like(acc_ref)

        left = a_ref[...]
        if gated:
            left = left * jax.nn.sigmoid(left) * refs[1][...]
        if split:
            acc_ref[...] += lax.dot_general(left, b_ref[...], _DOT_DIMS[mode], precision=BF16X3,
                                            preferred_element_type=F32)
        else:
            acc_ref[...] += lax.dot_general(left.astype(BF16), b_ref[...].astype(BF16), _DOT_DIMS[mode],
                                            preferred_element_type=F32)

        @pl.when(step == nk - 1)
        def _():
            o_ref[...] = acc_ref[...]

    if mode == 'tn':
        a_spec = pl.BlockSpec((tk, tm), lambda i, j, s: (s, i))
    else:
        a_spec = pl.BlockSpec((tm, tk), lambda i, j, s: (i, s))
    if mode == 'nt':
        b_spec = pl.BlockSpec((tn, tk), lambda i, j, s: (j, s))
    else:
        b_spec = pl.BlockSpec((tk, tn), lambda i, j, s: (s, j))
    o_spec = pl.BlockSpec((tm, tn), lambda i, j, s: (i, j))
    operands = [a] + ([a_gate] if gated else []) + [b] + ([init] if seeded else [])
    return pl.pallas_call(
        body, name=f"mm_{mode}{'_gated' if gated else ''}{'_seeded' if seeded else ''}{'_split' if split else ''}_{m}x{k}x{n}",
        grid=(m // tm, n // tn, nk),
        in_specs=[a_spec] * (1 + gated) + [b_spec] + [o_spec] * seeded, out_specs=o_spec,
        out_shape=jax.ShapeDtypeStruct((m, n), F32),
        scratch_shapes=[pltpu.VMEM((tm, tn), F32)],
        compiler_params=pltpu.CompilerParams(dimension_semantics=("parallel", "parallel", "arbitrary"),
                                             vmem_limit_bytes=VMEM_LIMIT),
    )(*operands)


class _Stream:
    def __init__(self, bn, t_ctx, t_lat, d):
        self.bn, self.n, self.d = bn, t_ctx + t_lat, d
        self.tr = _pick(math.gcd(t_ctx, t_lat), (256, 128, 64, 32, 16, 8))
        self.ctx_tiles = t_ctx // self.tr
        self.grid = (bn, self.n // self.tr)
        nct, rows = self.ctx_tiles, bn
        self.tok = pl.BlockSpec((None, self.tr, d), lambda b, i: (b, i, 0))
        self.mod = pl.BlockSpec((None, N_MOD, d), lambda b, i: (jnp.where(i < nct, rows, b), 0, 0))
        self.vec = pl.BlockSpec((1, d), lambda b, i: (0, 0))
        self.part = pl.BlockSpec((None, None, 1, d), lambda b, i: (b, jnp.where(i < nct, 0, 1), 0, 0))
        self.per_example = pl.BlockSpec((None, 1, d), lambda b, i: (b, 0, 0))
        self.tok_shape = jax.ShapeDtypeStruct((bn, self.n, d), F32)
        self.part_shape = jax.ShapeDtypeStruct((bn, 2, 1, d), F32)
        self.example_shape = jax.ShapeDtypeStruct((bn, 1, d), F32)
        self.params = pltpu.CompilerParams(dimension_semantics=("parallel", "arbitrary"), vmem_limit_bytes=VMEM_LIMIT)

    def starts_part(self, i):
        return (i == 0) | (i == self.ctx_tiles)


def _modulate(st, x, mod, k):
    def body(x_ref, m_ref, o_ref):
        o_ref[...] = x_ref[...] * (1.0 + m_ref[3 * k + 1:3 * k + 2, :]) + m_ref[3 * k:3 * k + 1, :]

    return pl.pallas_call(body, name=f"modulate_{k}", grid=st.grid, in_specs=[st.tok, st.mod], out_specs=st.tok,
                          out_shape=st.tok_shape, compiler_params=st.params)(x, mod)


def _norm_stats(z):
    mu = jnp.mean(z, axis=-1, keepdims=True)
    zc = z - mu
    rstd = lax.rsqrt(jnp.mean(zc * zc, axis=-1, keepdims=True) + EPS)
    return zc * rstd, rstd


def _post_norm(st, x, y, mod, k, lg, lb, rw, alpha):
    def body(x_ref, y_ref, m_ref, g_ref, b_ref, o_ref):
        xhat, _ = _norm_stats(alpha * x_ref[...] + (rw * m_ref[3 * k + 2:3 * k + 3, :]) * y_ref[...])
        o_ref[...] = xhat * g_ref[...] + b_ref[...]

    return pl.pallas_call(body, name=f"post_norm_{k}", grid=st.grid,
                          in_specs=[st.tok, st.tok, st.mod, st.vec, st.vec], out_specs=st.tok,
                          out_shape=st.tok_shape, compiler_params=st.params)(x, y, mod, lg[None], lb[None])


def _post_norm_bwd(st, x, y, dout, mod, k, lg, rw, alpha):
    def body(x_ref, y_ref, do_ref, m_ref, g_ref, dxr_ref, dy_ref, dgate_ref, dlg_ref, dlb_ref):
        i = pl.program_id(1)
        gate = rw * m_ref[3 * k + 2:3 * k + 3, :]
        yv, dov = y_ref[...], do_ref[...]
        xhat, rstd = _norm_stats(alpha * x_ref[...] + gate * yv)
        dxhat = dov * g_ref[...]
        dz = rstd * (dxhat - jnp.mean(dxhat, axis=-1, keepdims=True)
                     - xhat * jnp.mean(dxhat * xhat, axis=-1, keepdims=True))
        dxr_ref[...] = alpha * dz
        dy_ref[...] = gate * dz

        @pl.when(i == 0)
        def _():
            dlg_ref[...] = jnp.zeros_like(dlg_ref)
            dlb_ref[...] = jnp.zeros_like(dlb_ref)

        @pl.when(st.starts_part(i))
        def _():
            dgate_ref[...] = jnp.zeros_like(dgate_ref)

        dlg_ref[...] += jnp.sum(dov * xhat, axis=0, keepdims=True)
        dlb_ref[...] += jnp.sum(dov, axis=0, keepdims=True)
        dgate_ref[...] += jnp.sum(rw * yv * dz, axis=0, keepdims=True)

    return pl.pallas_call(
        body, name=f"post_norm_bwd_{k}", grid=st.grid, in_specs=[st.tok, st.tok, st.tok, st.mod, st.vec],
        out_specs=[st.tok, st.tok, st.part, st.per_example, st.per_example],
        out_shape=[st.tok_shape, st.tok_shape, st.part_shape, st.example_shape, st.example_shape],
        compiler_params=st.params)(x, y, dout, mod, lg[None])


def _modulate_bwd(st, dh, x, dxr, mod, k):
    def body(dh_ref, x_ref, dxr_ref, m_ref, dx_ref, dsh_ref, dsc_ref):
        dhv = dh_ref[...]
        dx_ref[...] = dxr_ref[...] + dhv * (1.0 + m_ref[3 * k + 1:3 * k + 2, :])

        @pl.when(st.starts_part(pl.program_id(1)))
        def _():
            dsh_ref[...] = jnp.zeros_like(dsh_ref)
            dsc_ref[...] = jnp.zeros_like(dsc_ref)

        dsh_ref[...] += jnp.sum(dhv, axis=0, keepdims=True)
        dsc_ref[...] += jnp.sum(dhv * x_ref[...], axis=0, keepdims=True)

    return pl.pallas_call(
        body, name=f"modulate_bwd_{k}", grid=st.grid, in_specs=[st.tok, st.tok, st.tok, st.mod],
        out_specs=[st.tok, st.part, st.part], out_shape=[st.tok_shape, st.part_shape, st.part_shape],
        compiler_params=st.params)(dh, x, dxr, mod)


def _swiglu_bwd(ds, a, b):
    rows, cols = a.shape
    tr = _row_tile(rows, cols, 5)

    def body(ds_ref, a_ref, b_ref, da_ref, db_ref):
        av, dsv = a_ref[...], ds_ref[...]
        sig = jax.nn.sigmoid(av)
        da_ref[...] = dsv * b_ref[...] * (sig * (1.0 + av * (1.0 - sig)))
        db_ref[...] = dsv * (av * sig)

    spec = pl.BlockSpec((tr, cols), lambda i: (i, 0))
    return pl.pallas_call(body, name="swiglu_bwd", grid=(rows // tr,), in_specs=[spec] * 3, out_specs=[spec] * 2,
                          out_shape=[jax.ShapeDtypeStruct(a.shape, F32)] * 2,
                          compiler_params=pltpu.CompilerParams(vmem_limit_bytes=VMEM_LIMIT))(ds, a, b)


def _rms_norm(x, w):
    return x * lax.rsqrt(jnp.mean(x * x, axis=-1, keepdims=True) + EPS) * w


def _l2_normalize(x):
    return x * lax.rsqrt(jnp.sum(x * x, axis=-1, keepdims=True) + EPS)


def _ffn_fwd(st, x, mod, k, w1, w3, w2, lg, lb, alpha):
    rows = st.bn * st.n
    h = _modulate(st, x, mod, k).reshape(rows, st.d)
    a, b = _mm(h, w1), _mm(h, w3)
    y = _mm(a, w2, a_gate=b).reshape(st.bn, st.n, st.d)
    return _post_norm(st, x, y, mod, k, lg, lb, 0.5, alpha), (h, a, b, y)


def _ffn_bwd(st, x, mod, k, w1, w3, w2, lg, alpha, kept, dout):
    h, a, b, y = kept
    rows = st.bn * st.n
    dxr, dy, dgate, dlg, dlb = _post_norm_bwd(st, x, y, dout, mod, k, lg, 0.5, alpha)
    dy = dy.reshape(rows, st.d)
    ds = _mm(dy, w2, 'nt')
    dw2 = _mm(a, dy, 'tn', a_gate=b)
    da, db = _swiglu_bwd(ds, a, b)
    dw1 = _mm(h, da, 'tn')
    dw3 = _mm(h, db, 'tn')
    dh = _mm(db, w3, 'nt', init=_mm(da, w1, 'nt')).reshape(st.bn, st.n, st.d)
    dx, dsh, dsc = _modulate_bwd(st, dh, x, dxr, mod, k)
    return dx, (dsh, dsc, dgate), dw1, dw3, dw2, jnp.sum(dlg, axis=(0, 1)), jnp.sum(dlb, axis=(0, 1))


def _dwconv_centred(x, w):
    pad = CONV_K // 2
    return lax.conv_general_dilated(
        x, w[:, None, :].astype(x.dtype), window_strides=(1,), padding=[(pad, pad)],
        dimension_numbers=('NWC', 'WIO', 'NWC'), feature_group_count=x.shape[-1])


def _largest_divisor(n, cap):
    return max(t for t in range(1, cap + 1) if n % t == 0)


def _bmm(a, b):
    return lax.dot_general(a, b, (((2,), (1,)), ((0,), (0,))), precision=BF16X3, preferred_element_type=F32)


_BATCHED_DIMS = {'nn': (((2,), (1,)), ((0,), (0,))), 'nt': (((2,), (2,)), ((0,), (0,))),
                 'tn': (((1,), (1,)), ((0,), (0,)))}


def _bdot(a, b, mode):
    return lax.dot_general(a.astype(BF16), b.astype(BF16), _BATCHED_DIMS[mode], preferred_element_type=F32)


@functools.partial(jax.custom_vjp, nondiff_argnums=(2,))
def _bmm_lo(a, b, mode):
    return _bdot(a, b, mode)


def _bmm_lo_fwd(a, b, mode):
    return _bdot(a, b, mode), (a, b)


def _bmm_lo_bwd(mode, res, ct):
    a, b = res
    if mode == 'nn':
        return _bdot(ct, b, 'nt'), _bdot(a, ct, 'tn')
    if mode == 'nt':
        return _bdot(ct, b, 'nn'), _bdot(ct, a, 'tn')
    return _bdot(b, ct, 'nt'), _bdot(a, ct, 'nn')


_bmm_lo.defvjp(_bmm_lo_fwd, _bmm_lo_bwd)


def _gdn_prep(q, k, v, gc, gr, beta):
    n = q.shape[0]
    row = lax.broadcasted_iota(jnp.int32, (n, CHUNK, CHUNK), 1)
    col = lax.broadcasted_iota(jnp.int32, (n, CHUNK, CHUNK), 2)
    incl, strict = row >= col, row > col
    decay = jnp.where(incl, jnp.exp(jnp.where(incl, gc - gr, 0.0)), 0.0)
    kb = k * beta
    lower = jnp.where(strict, _bmm_lo(kb, k, 'nt') * decay, 0.0)
    eye = jnp.where(row == col, 1.0, 0.0)
    inv, power = eye - lower, lower
    for _ in range(5):
        power = _bmm(power, power)
        inv = _bmm(inv, eye + power)
    u = _bmm(inv, v * beta)
    w = _bmm(inv, kb * jnp.exp(gc))
    intra = jnp.where(incl, _bmm_lo(q, k, 'nt') * decay, 0.0)
    is_last = lax.broadcasted_iota(jnp.int32, (n, CHUNK, 1), 1) == CHUNK - 1
    g_last = jnp.sum(jnp.where(is_last, gc, 0.0), axis=1, keepdims=True)
    return u, w, intra, q * jnp.exp(gc), k * jnp.exp(g_last - gc), jnp.exp(g_last) * jnp.ones((n, 1, GDN_DV), F32)


def _gdn_step(state, u, w, intra, qg, kd, gl):
    v_new = u - _bmm_lo(w, state, 'nn')
    o = _bmm_lo(qg, state, 'nn') + _bmm_lo(intra, v_new, 'nn')
    return o, state * gl + _bmm_lo(kd, v_new, 'tn')


def _gdn_prep_specs(cb):
    idx = lambda c, i: (c, i, 0, 0)
    mat = pl.BlockSpec((None, cb, CHUNK, GDN_DK), idx)
    colv = pl.BlockSpec((None, cb, CHUNK, 1), idx)
    rowv = pl.BlockSpec((None, cb, 1, CHUNK), idx)
    return mat, colv, rowv


def _gdn_prep_fwd_call(q, k, v, gc, gr, beta):
    chains, nchunks = q.shape[:2]
    cb = _largest_divisor(nchunks, GDN_PREP_CHUNKS)
    mat, colv, rowv = _gdn_prep_specs(cb)

    def body(q_ref, k_ref, v_ref, gc_ref, gr_ref, b_ref, *out_refs):
        outs = _gdn_prep(q_ref[...], k_ref[...], v_ref[...], gc_ref[...], gr_ref[...], b_ref[...])
        for ref, val in zip(out_refs, outs):
            ref[...] = val

    mshape = jax.ShapeDtypeStruct(q.shape, F32)
    return pl.pallas_call(
        body, name="gdn_prep_fwd", grid=(chains, nchunks // cb),
        in_specs=[mat, mat, mat, colv, rowv, colv], out_specs=[mat] * 5 + [rowv],
        out_shape=[mshape] * 5 + [jax.ShapeDtypeStruct(gr.shape, F32)],
        compiler_params=pltpu.CompilerParams(dimension_semantics=("parallel", "parallel"), vmem_limit_bytes=VMEM_LIMIT),
    )(q, k, v, gc, gr, beta)


def _gdn_prep_bwd_call(q, k, v, gc, gr, beta, cts):
    chains, nchunks = q.shape[:2]
    cb = _largest_divisor(nchunks, GDN_PREP_CHUNKS)
    mat, colv, rowv = _gdn_prep_specs(cb)

    def body(q_ref, k_ref, v_ref, gc_ref, gr_ref, b_ref, du, dw, di, dqg, dkd, dgl, *out_refs):
        _, vjp = jax.vjp(_gdn_prep, q_ref[...], k_ref[...], v_ref[...], gc_ref[...], gr_ref[...], b_ref[...])
        grads = vjp((du[...], dw[...], di[...], dqg[...], dkd[...], dgl[...]))
        for ref, val in zip(out_refs, grads):
            ref[...] = val

    return pl.pallas_call(
        body, name="gdn_prep_bwd", grid=(chains, nchunks // cb),
        in_specs=[mat, mat, mat, colv, rowv, colv] + [mat] * 5 + [rowv],
        out_specs=[mat, mat, mat, colv, rowv, colv],
        out_shape=[jax.ShapeDtypeStruct(t.shape, F32) for t in (q, k, v, gc, gr, beta)],
        compiler_params=pltpu.CompilerParams(dimension_semantics=("parallel", "parallel"), vmem_limit_bytes=VMEM_LIMIT),
    )(q, k, v, gc, gr, beta, *cts)


def _gdn_scan_fwd_call(u, w, intra, qg, kd, gl):
    chains, nchunks = u.shape[:2]
    cc = _largest_divisor(chains, GDN_SCAN_CHAINS)
    idx = lambda c, i: (c, i, 0, 0)
    mat = pl.BlockSpec((cc, None, CHUNK, GDN_DK), idx)
    rowv = pl.BlockSpec((cc, None, 1, CHUNK), idx)

    def body(u_ref, w_ref, i_ref, qg_ref, kd_ref, gl_ref, o_ref, hist_ref, state_ref):
        @pl.when(pl.program_id(1) == 0)
        def _():
            state_ref[...] = jnp.zeros_like(state_ref)

        state = state_ref[...]
        hist_ref[...] = state
        o, new = _gdn_step(state, u_ref[...], w_ref[...], i_ref[...], qg_ref[...], kd_ref[...], gl_ref[...])
        o_ref[...] = o
        state_ref[...] = new

    mshape = jax.ShapeDtypeStruct(u.shape, F32)
    return pl.pallas_call(
        body, name="gdn_scan_fwd", grid=(chains // cc, nchunks),
        in_specs=[mat] * 5 + [rowv], out_specs=[mat, mat], out_shape=[mshape, mshape],
        scratch_shapes=[pltpu.VMEM((cc, GDN_DK, GDN_DV), F32)],
        compiler_params=pltpu.CompilerParams(dimension_semantics=("parallel", "arbitrary"), vmem_limit_bytes=VMEM_LIMIT),
    )(u, w, intra, qg, kd, gl)


def _gdn_scan_bwd_call(u, w, intra, qg, kd, gl, hist, do):
    chains, nchunks = u.shape[:2]
    cc = _largest_divisor(chains, GDN_SCAN_CHAINS)
    idx = lambda c, i: (c, nchunks - 1 - i, 0, 0)
    mat = pl.BlockSpec((cc, None, CHUNK, GDN_DK), idx)
    rowv = pl.BlockSpec((cc, None, 1, CHUNK), idx)

    def body(u_ref, w_ref, i_ref, qg_ref, kd_ref, gl_ref, h_ref, do_ref, du, dw, di, dqg, dkd, dgl, dstate_ref):
        @pl.when(pl.program_id(1) == 0)
        def _():
            dstate_ref[...] = jnp.zeros_like(dstate_ref)

        _, vjp = jax.vjp(_gdn_step, h_ref[...], u_ref[...], w_ref[...], i_ref[...], qg_ref[...], kd_ref[...], gl_ref[...])
        grads = vjp((do_ref[...], dstate_ref[...]))
        dstate_ref[...] = grads[0]
        for ref, val in zip((du, dw, di, dqg, dkd, dgl), grads[1:]):
            ref[...] = val

    mshape = jax.ShapeDtypeStruct(u.shape, F32)
    return pl.pallas_call(
        body, name="gdn_scan_bwd", grid=(chains // cc, nchunks),
        in_specs=[mat] * 5 + [rowv, mat, mat], out_specs=[mat] * 5 + [rowv],
        out_shape=[mshape] * 5 + [jax.ShapeDtypeStruct(gl.shape, F32)],
        scratch_shapes=[pltpu.VMEM((cc, GDN_DK, GDN_DV), F32)],
        compiler_params=pltpu.CompilerParams(dimension_semantics=("parallel", "arbitrary"), vmem_limit_bytes=VMEM_LIMIT),
    )(u, w, intra, qg, kd, gl, hist, do)


@jax.custom_vjp
def _gdn_core(q, k, v, gc, gr, beta):
    return _gdn_scan_fwd_call(*_gdn_prep_fwd_call(q, k, v, gc, gr, beta))[0]


def _gdn_core_fwd(q, k, v, gc, gr, beta):
    prep = _gdn_prep_fwd_call(q, k, v, gc, gr, beta)
    o, hist = _gdn_scan_fwd_call(*prep)
    return o, (q, k, v, gc, gr, beta, prep, hist)


def _gdn_core_bwd(res, do):
    q, k, v, gc, gr, beta, prep, hist = res
    cts = _gdn_scan_bwd_call(*prep, hist, do)
    return tuple(_gdn_prep_bwd_call(q, k, v, gc, gr, beta, cts))


_gdn_core.defvjp(_gdn_core_fwd, _gdn_core_bwd)


def _gdn_inputs(qkv, b, a, conv_w, a_log, dt_bias):
    Bn, T, _ = qkv.shape
    qkv = jax.nn.silu(_dwconv_centred(qkv, conv_w))
    q, k, v = jnp.split(qkv, [GDN_HEADS * GDN_DK, 2 * GDN_HEADS * GDN_DK], axis=-1)

    def to_heads(t, d):
        return t.reshape(Bn, T, GDN_HEADS, d).transpose(0, 2, 1, 3)

    def dir_heads(t):
        return t.reshape(Bn, T, 2, GDN_HEADS).transpose(2, 0, 3, 1)

    q = _l2_normalize(to_heads(q, GDN_DK))
    k = _l2_normalize(to_heads(k, GDN_DK))
    v = to_heads(v, GDN_DV)
    beta = jax.nn.sigmoid(dir_heads(b))
    g = -jnp.exp(a_log)[:, None, :, None] * jax.nn.softplus(dir_heads(a) + dt_bias[:, None, :, None])
    return q, k, v, g, beta


def _gdn_gated_out(o, z, norm_w):
    Bn, H, T, dv = o.shape
    o = _rms_norm(o.transpose(0, 2, 1, 3), norm_w)
    o = o * jax.nn.silu(z.reshape(Bn, T, H, dv))
    return o.reshape(Bn, T, H * dv)


def _gdn_group(qkv, z, b, a, qkv_c, z_c, b_c, a_c, conv_w, a_log, dt_bias, norm_w, with_ctx_out):
    q, k, v, g, beta = _gdn_inputs(qkv, b, a, conv_w, a_log, dt_bias)
    qc, kc, vc, gc, betac = _gdn_inputs(qkv_c, b_c, a_c, conv_w, a_log, dt_bias)
    bn, heads, t_lat, _ = q.shape
    t_ctx = qc.shape[2]
    n = t_ctx + t_lat
    nchunks = n // CHUNK
    chains = 2 * bn * heads

    def both(tc, tl):
        return jnp.stack([jnp.concatenate([tc, tl], axis=2), jnp.concatenate([jnp.flip(tc, 2), jnp.flip(tl, 2)], axis=2)])

    def per_dir(tc, tl):
        return jnp.stack([jnp.concatenate([tc[0], tl[0]], axis=2),
                          jnp.concatenate([jnp.flip(tc[1], 2), jnp.flip(tl[1], 2)], axis=2)])

    def mats(t):
        return t.reshape(chains, nchunks, CHUNK, t.shape[-1])

    gcum = jnp.cumsum(per_dir(gc, g).reshape(chains, nchunks, CHUNK), axis=-1)
    o = _gdn_core(mats(both(qc, q) * GDN_DK ** -0.5), mats(both(kc, k)), mats(both(vc, v)),
                  gcum[..., None], gcum[:, :, None, :], per_dir(betac, beta).reshape(chains, nchunks, CHUNK, 1))
    o = o.reshape(2, bn, heads, n, GDN_DV)
    out = _gdn_gated_out(o[0, :, :, t_ctx:] + jnp.flip(o[1, :, :, t_ctx:], 2), z, norm_w)
    if not with_ctx_out:
        return out, None
    return out, _gdn_gated_out(o[0, :, :, :t_ctx] + jnp.flip(o[1, :, :, :t_ctx], 2), z_c, norm_w)


def _axial_rope_tables(rows):
    row = jnp.repeat(jnp.arange(rows), GRID_W)
    col = jnp.tile(jnp.arange(GRID_W), rows)
    inv_freq = ROPE_THETA ** (-jnp.arange(ROPE_PAIRS, dtype=F32) / ROPE_PAIRS)
    ang = jnp.stack([row, col], axis=-1).astype(F32)[..., None] * inv_freq
    return jnp.cos(ang), jnp.sin(ang)


def _rope_2d(x, cos, sin):
    shp = x.shape
    xr = x.reshape(*shp[:-1], 2, 2, ROPE_PAIRS)
    x1, x2 = xr[..., 0, :], xr[..., 1, :]
    bshape = (shp[1],) + (1,) * (x.ndim - 3) + (2, ROPE_PAIRS)
    c, s = cos.reshape(bshape), sin.reshape(bshape)
    out = jnp.stack([x1 * c - x2 * s, x2 * c + x1 * s], axis=-2)
    return out.reshape(shp)


def _attn_specs(tq, tk):
    q_spec = pl.BlockSpec((None, None, ATT_GROUP, tq, ATT_DH), lambda b, h, i: (b, h, 0, i, 0))
    kv_spec = pl.BlockSpec((None, None, tk, ATT_DH), lambda b, h, i: (b, h, 0, 0))
    return q_spec, kv_spec


def _softmax_rows(q, k):
    s = lax.dot_general(q.astype(BF16), k.astype(BF16), (((1,), (1,)), ((), ())), preferred_element_type=F32)
    s = s * (ATT_DH ** -0.5)
    e = jnp.exp(s - jnp.max(s, axis=-1, keepdims=True))
    return e / jnp.sum(e, axis=-1, keepdims=True)


def _attn_fwd_call(q, k, v):
    bn, _, _, t_q, _ = q.shape
    t_k = k.shape[2]
    tq = _pick(t_q, (Q_BLOCK, 64, 32, 16, 8))
    q_spec, kv_spec = _attn_specs(tq, t_k)

    def body(q_ref, k_ref, v_ref, o_ref):
        p = _softmax_rows(q_ref[...].reshape(ATT_GROUP * tq, ATT_DH), k_ref[...])
        o = lax.dot_general(p.astype(BF16), v_ref[...].astype(BF16), (((1,), (0,)), ((), ())), preferred_element_type=F32)
        o_ref[...] = o.reshape(ATT_GROUP, tq, ATT_DH)

    return pl.pallas_call(
        body, name=f"attention_fwd_{t_q}x{t_k}", grid=(bn, ATT_KV_HEADS, t_q // tq),
        in_specs=[q_spec, kv_spec, kv_spec], out_specs=q_spec,
        out_shape=jax.ShapeDtypeStruct(q.shape, F32),
        compiler_params=pltpu.CompilerParams(dimension_semantics=("parallel", "parallel", "parallel"),
                                             vmem_limit_bytes=VMEM_LIMIT),
    )(q, k, v)


def _attn_bwd_call(q, k, v, do):
    bn, _, _, t_q, _ = q.shape
    t_k = k.shape[2]
    tq = _pick(t_q, (Q_BLOCK, 64, 32, 16, 8))
    q_spec, kv_spec = _attn_specs(tq, t_k)

    def body(q_ref, k_ref, v_ref, do_ref, dq_ref, dk_ref, dv_ref):
        @pl.when(pl.program_id(2) == 0)
        def _():
            dk_ref[...] = jnp.zeros_like(dk_ref)
            dv_ref[...] = jnp.zeros_like(dv_ref)

        qv = q_ref[...].reshape(ATT_GROUP * tq, ATT_DH)
        dov = do_ref[...].reshape(ATT_GROUP * tq, ATT_DH).astype(BF16)
        kb, vb = k_ref[...].astype(BF16), v_ref[...].astype(BF16)
        p = _softmax_rows(qv, k_ref[...])
        dp = lax.dot_general(dov, vb, (((1,), (1,)), ((), ())), preferred_element_type=F32)
        ds = p * (dp - jnp.sum(p * dp, axis=-1, keepdims=True)) * (ATT_DH ** -0.5)
        dsb = ds.astype(BF16)
        dq = lax.dot_general(dsb, kb, (((1,), (0,)), ((), ())), preferred_element_type=F32)
        dq_ref[...] = dq.reshape(ATT_GROUP, tq, ATT_DH)
        dk_ref[...] += lax.dot_general(dsb, qv.astype(BF16), (((0,), (0,)), ((), ())), preferred_element_type=F32)
        dv_ref[...] += lax.dot_general(p.astype(BF16), dov, (((0,), (0,)), ((), ())), preferred_element_type=F32)

    return pl.pallas_call(
        body, name=f"attention_bwd_{t_q}x{t_k}", grid=(bn, ATT_KV_HEADS, t_q // tq),
        in_specs=[q_spec, kv_spec, kv_spec, q_spec], out_specs=[q_spec, kv_spec, kv_spec],
        out_shape=[jax.ShapeDtypeStruct(t.shape, F32) for t in (q, k, v)],
        compiler_params=pltpu.CompilerParams(dimension_semantics=("parallel", "parallel", "arbitrary"),
                                             vmem_limit_bytes=VMEM_LIMIT),
    )(q, k, v, do)


@jax.custom_vjp
def _attn_core(q, k, v):
    return _attn_fwd_call(q, k, v)


def _attn_core_fwd(q, k, v):
    return _attn_fwd_call(q, k, v), (q, k, v)


def _attn_core_bwd(res, do):
    return tuple(_attn_bwd_call(*res, do))


_attn_core.defvjp(_attn_core_fwd, _attn_core_bwd)


def _attention_group(q, k, v, q_c, k_c, v_c, q_norm_w, k_norm_w, cos, sin, with_ctx_out):
    Bn, T, _ = q.shape
    Tc = q_c.shape[1]
    q = _rope_2d(_rms_norm(q.reshape(Bn, T, ATT_KV_HEADS, ATT_GROUP, ATT_DH), q_norm_w), cos, sin)
    k = _rope_2d(_rms_norm(k.reshape(Bn, T, ATT_KV_HEADS, ATT_DH), k_norm_w), cos, sin)
    v = v.reshape(Bn, T, ATT_KV_HEADS, ATT_DH)
    qc = _rms_norm(q_c.reshape(Bn, Tc, ATT_KV_HEADS, ATT_GROUP, ATT_DH), q_norm_w)
    kc = _rms_norm(k_c.reshape(Bn, Tc, ATT_KV_HEADS, ATT_DH), k_norm_w)
    vc = v_c.reshape(Bn, Tc, ATT_KV_HEADS, ATT_DH)
    keys = jnp.concatenate([kc, k], axis=1).transpose(0, 2, 1, 3)
    vals = jnp.concatenate([vc, v], axis=1).transpose(0, 2, 1, 3)
    o = _attn_core(q.transpose(0, 2, 3, 1, 4), keys, vals)
    o = o.transpose(0, 3, 1, 2, 4).reshape(Bn, T, ATT_W)
    if not with_ctx_out:
        return o, None
    o_c = _attn_core(qc.transpose(0, 2, 3, 1, 4), kc.transpose(0, 2, 1, 3), vc.transpose(0, 2, 1, 3))
    return o, o_c.transpose(0, 3, 1, 2, 4).reshape(Bn, Tc, ATT_W)


def _s5_operators(lam_re, lam_im, log_dt, b_re, b_im, c_re, c_im):
    lc = S5_LC
    dt = jnp.exp(log_dt)[..., None]
    ar, ai = lam_re * dt, lam_im * dt
    mag = jnp.exp(ar)
    lbr, lbi = mag * jnp.cos(ai), mag * jnp.sin(ai)
    den = lam_re * lam_re + lam_im * lam_im
    fr = ((lbr - 1.0) * lam_re + lbi * lam_im) / den
    fi = (lbi * lam_re - (lbr - 1.0) * lam_im) / den
    bbr = fr[..., None] * b_re - fi[..., None] * b_im
    bbi = fr[..., None] * b_im + fi[..., None] * b_re
    m = jnp.arange(lc + 1, dtype=F32)[:, None, None, None]
    pmag = jnp.exp(m * ar)
    pr, pi = pmag * jnp.cos(m * ai), pmag * jnp.sin(m * ai)
    cpr = c_re[None] * pr[:, :, :, None, :] - c_im[None] * pi[:, :, :, None, :]
    cpi = c_re[None] * pi[:, :, :, None, :] + c_im[None] * pr[:, :, :, None, :]
    kern = (jnp.einsum('mdghp,dgpk->mdghk', cpr[:lc], bbr, precision=HI)
            - jnp.einsum('mdghp,dgpk->mdghk', cpi[:lc], bbi, precision=HI))
    tail = kern.shape[1:]
    lags = jnp.concatenate([jnp.zeros((lc - 1,) + tail, F32), kern, jnp.zeros((1,) + tail, F32)], axis=0)
    toep = jnp.tile(lags, (lc,) + (1,) * len(tail))[:lc * (2 * lc - 1)].reshape((lc, 2 * lc - 1) + tail)[:, lc - 1:]
    tm = toep.transpose(2, 3, 0, 5, 1, 4).reshape(2, S5_GROUPS, S5_CW, S5_CW)
    prr, pir = pr[lc - 1::-1], pi[lc - 1::-1]
    mre = prr[..., None] * bbr[None] - pir[..., None] * bbi[None]
    mim = prr[..., None] * bbi[None] + pir[..., None] * bbr[None]
    mm = jnp.concatenate([mre, mim], axis=3).transpose(1, 2, 0, 4, 3).reshape(2, S5_GROUPS, S5_CW, S5_SW)
    nm = jnp.concatenate([cpr[1:], -cpi[1:]], axis=-1)
    nm = nm.transpose(1, 2, 4, 0, 3).reshape(2, S5_GROUPS, S5_SW, S5_CW)
    a1 = jnp.concatenate([pr[lc], pr[lc]], axis=-1)
    a2 = jnp.concatenate([-pi[lc], pi[lc]], axis=-1)
    lam_rows = jnp.concatenate([a1[:, :, None], a2[:, :, None],
                                jnp.zeros((2, S5_GROUPS, SUBLANES - 2, S5_SW), F32)], axis=2)
    return tm, mm, nm, lam_rows


def _dot_hi(a, b, dims):
    return lax.dot_general(a, b, (dims, ((), ())), precision=BF16X3, preferred_element_type=F32)


def _s5_blocks(rows):
    def blk(r, c):
        return pl.BlockSpec((None, None, r, c), lambda d, g: (d, g, 0, 0))
    return (blk(rows, S5_CW), blk(S5_CW, S5_CW), blk(S5_CW, S5_SW), blk(S5_SW, S5_CW), blk(SUBLANES, S5_SW),
            blk(rows, S5_SW))


def _s5_core_fwd_call(s, tm, mm, nm, lam_rows):
    rows = s.shape[2]
    chunks = rows // S5_ROWS
    seq, top, mop, nop, lop, sta = _s5_blocks(rows)

    def body(s_ref, t_ref, m_ref, n_ref, l_ref, y_ref, h_ref, e_ref):
        sv = s_ref[...]
        e_ref[...] = _dot_hi(sv, m_ref[...], ((1,), (0,)))
        a1, a2 = l_ref[0:1, :], l_ref[1:2, :]
        h_ref[0:S5_ROWS, :] = jnp.zeros((S5_ROWS, S5_SW), F32)

        def step(k, carry):
            at = pl.multiple_of((k - 1) * S5_ROWS, S5_ROWS)
            prev = h_ref[pl.ds(at, S5_ROWS), :]
            new = a1 * prev + a2 * pltpu.roll(prev, S5_P, 1) + e_ref[pl.ds(at, S5_ROWS), :]
            h_ref[pl.ds(pl.multiple_of(k * S5_ROWS, S5_ROWS), S5_ROWS), :] = new
            return carry

        lax.fori_loop(1, chunks, step, 0)
        y_ref[...] = _dot_hi(sv, t_ref[...], ((1,), (0,))) + _dot_hi(h_ref[...], n_ref[...], ((1,), (0,)))

    return pl.pallas_call(
        body, name="s5_chunks_fwd", grid=(2, S5_GROUPS),
        in_specs=[seq, top, mop, nop, lop], out_specs=[seq, sta],
        out_shape=[jax.ShapeDtypeStruct(s.shape, F32), jax.ShapeDtypeStruct(s.shape[:3] + (S5_SW,), F32)],
        scratch_shapes=[pltpu.VMEM((rows, S5_SW), F32)],
        compiler_params=pltpu.CompilerParams(dimension_semantics=("parallel", "parallel"), vmem_limit_bytes=VMEM_LIMIT),
    )(s, tm, mm, nm, lam_rows)


def _s5_core_bwd_call(s, dy, tm, mm, nm, lam_rows, hin):
    rows = s.shape[2]
    chunks = rows // S5_ROWS
    seq, top, mop, nop, lop, sta = _s5_blocks(rows)

    def body(s_ref, dy_ref, t_ref, m_ref, n_ref, l_ref, h_ref, ds_ref, dt_ref, dm_ref, dn_ref, dl_ref, dh_ref, de_ref):
        sv, dyv, hv = s_ref[...], dy_ref[...], h_ref[...]
        dh_ref[...] = _dot_hi(dyv, n_ref[...], ((1,), (1,)))
        a1, a2 = l_ref[0:1, :], l_ref[1:2, :]
        last = (chunks - 1) * S5_ROWS
        de_ref[last:last + S5_ROWS, :] = jnp.zeros((S5_ROWS, S5_SW), F32)

        def step(i, g):
            k = chunks - 2 - i
            at = pl.multiple_of(k * S5_ROWS, S5_ROWS)
            de_ref[pl.ds(at, S5_ROWS), :] = g
            return dh_ref[pl.ds(at, S5_ROWS), :] + a1 * g + pltpu.roll(a2 * g, S5_P, 1)

        lax.fori_loop(0, chunks - 1, step, dh_ref[last:last + S5_ROWS, :])
        dev = de_ref[...]
        ds_ref[...] = _dot_hi(dyv, t_ref[...], ((1,), (1,))) + _dot_hi(dev, m_ref[...], ((1,), (1,)))
        dt_ref[...] = _dot_hi(sv, dyv, ((0,), (0,)))
        dm_ref[...] = _dot_hi(sv, dev, ((0,), (0,)))
        dn_ref[...] = _dot_hi(hv, dyv, ((0,), (0,)))
        da1 = jnp.sum(hv * dev, axis=0, keepdims=True)
        da2 = jnp.sum(pltpu.roll(hv, S5_P, 1) * dev, axis=0, keepdims=True)
        dl_ref[...] = jnp.concatenate([da1, da2, jnp.zeros((SUBLANES - 2, S5_SW), F32)], axis=0)

    return pl.pallas_call(
        body, name="s5_chunks_bwd", grid=(2, S5_GROUPS),
        in_specs=[seq, seq, top, mop, nop, lop, sta], out_specs=[seq, top, mop, nop, lop],
        out_shape=[jax.ShapeDtypeStruct(t.shape, F32) for t in (s, tm, mm, nm, lam_rows)],
        scratch_shapes=[pltpu.VMEM((rows, S5_SW), F32), pltpu.VMEM((rows, S5_SW), F32)],
        compiler_params=pltpu.CompilerParams(dimension_semantics=("parallel", "parallel"), vmem_limit_bytes=VMEM_LIMIT),
    )(s, dy, tm, mm, nm, lam_rows, hin)


@jax.custom_vjp
def _s5_core(s, tm, mm, nm, lam_rows):
    return _s5_core_fwd_call(s, tm, mm, nm, lam_rows)[0]


def _s5_core_fwd(s, tm, mm, nm, lam_rows):
    y, hin = _s5_core_fwd_call(s, tm, mm, nm, lam_rows)
    return y, (s, tm, mm, nm, lam_rows, hin)


def _s5_core_bwd(res, dy):
    s, tm, mm, nm, lam_rows, hin = res
    return tuple(_s5_core_bwd_call(s, dy, tm, mm, nm, lam_rows, hin))


_s5_core.defvjp(_s5_core_fwd, _s5_core_bwd)


def _s5_group(u, u_c, lam_re, lam_im, log_dt, b_re, b_im, c_re, c_im, d_skip, glu_w, glu_b, with_ctx_out):
    bn, t_lat, _ = u.shape
    t_ctx = u_c.shape[1]
    n = t_ctx + t_lat
    chunks = n // S5_LC
    assert bn <= S5_ROWS and t_ctx % S5_LC == 0 and t_lat % S5_LC == 0
    seqs = jnp.stack([jnp.concatenate([u_c, u], axis=1),
                      jnp.concatenate([jnp.flip(u_c, 1), jnp.flip(u, 1)], axis=1)])
    s = seqs.reshape(2, bn, chunks, S5_LC, S5_GROUPS, S5_GH).transpose(0, 4, 2, 1, 3, 5)
    s = jnp.pad(s, ((0, 0), (0, 0), (0, 0), (0, S5_ROWS - bn), (0, 0), (0, 0)))
    s = s.reshape(2, S5_GROUPS, chunks * S5_ROWS, S5_CW)
    y = _s5_core(s, *_s5_operators(lam_re, lam_im, log_dt, b_re, b_im, c_re, c_im))
    y = y.reshape(2, S5_GROUPS, chunks, S5_ROWS, S5_LC, S5_GH)[:, :, :, :bn]
    y = y.transpose(0, 3, 2, 4, 1, 5).reshape(2, bn, n, S5_W)
    yl = d_skip * u + y[0, :, t_ctx:] + jnp.flip(y[1, :, t_ctx:], 1)

    def glu(yy):
        zz = jax.nn.gelu(yy)
        return zz * jax.nn.sigmoid(zz @ glu_w + glu_b)

    if not with_ctx_out:
        return glu(yl), None
    yc = d_skip * u_c + y[0, :, :t_ctx] + jnp.flip(y[1, :, :t_ctx], 1)
    return glu(yl), glu(yc)


def _s5_wide_operators(lam_re, lam_im, log_dt, b_re, b_im, c_re, c_im):
    lc, g4 = S5W_LC, S5W_TILE // S5_SW
    dt = jnp.exp(log_dt)[..., None]
    ar, ai = lam_re * dt, lam_im * dt
    mag = jnp.exp(ar)
    lbr, lbi = mag * jnp.cos(ai), mag * jnp.sin(ai)
    den = lam_re * lam_re + lam_im * lam_im
    fr = ((lbr - 1.0) * lam_re + lbi * lam_im) / den
    fi = (lbi * lam_re - (lbr - 1.0) * lam_im) / den
    bbr = fr[..., None] * b_re - fi[..., None] * b_im
    bbi = fr[..., None] * b_im + fi[..., None] * b_re
    m = jnp.arange(lc + 1, dtype=F32)[:, None, None, None]
    pmag = jnp.exp(m * ar)
    pr, pi = pmag * jnp.cos(m * ai), pmag * jnp.sin(m * ai)
    cpr = c_re[None] * pr[:, :, :, None, :] - c_im[None] * pi[:, :, :, None, :]
    cpi = c_re[None] * pi[:, :, :, None, :] + c_im[None] * pr[:, :, :, None, :]
    kern = (jnp.einsum('mdghp,dgpk->mdghk', cpr[:lc], bbr, precision=HI)
            - jnp.einsum('mdghp,dgpk->mdghk', cpi[:lc], bbi, precision=HI))
    lags = jnp.concatenate([jnp.zeros((lc - 1,) + kern.shape[1:], F32), kern], axis=0)
    toep = jnp.stack([lags[lc - 1 - j:2 * lc - 1 - j] for j in range(lc)])
    eye = jnp.eye(S5_GROUPS, dtype=F32)
    tm = jnp.einsum('jtdghk,gn->djgktnh', toep, eye).reshape(2, S5W_RW, S5W_RW)
    back = (lc - 1.0 - jnp.arange(lc, dtype=F32))[:, None, None, None]
    bmag = jnp.exp(back * ar)
    prr, pir = bmag * jnp.cos(back * ai), bmag * jnp.sin(back * ai)
    left = jnp.stack([prr[..., None] * bbr[None] - pir[..., None] * bbi[None],
                      prr[..., None] * bbi[None] + pir[..., None] * bbr[None]])
    mm = jnp.einsum('ajdgpk,gn->djgknap', left, eye)
    mm = mm.reshape(2, lc, S5_GROUPS, S5_GH, S5_GROUPS // g4, g4, 2, S5_P).transpose(0, 1, 2, 3, 4, 6, 5, 7)
    mm = mm.reshape(2, S5W_RW, S5W_SW)
    right = jnp.stack([cpr[1:], -cpi[1:]])
    nm = jnp.einsum('atdghp,gn->dnaptgh', right, eye)
    nm = nm.reshape(2, S5_GROUPS // g4, g4, 2, S5_P, lc, S5_GROUPS, S5_GH).transpose(0, 1, 3, 2, 4, 5, 6, 7)
    nm = nm.reshape(2, S5W_SW, S5W_RW)

    def state_cols(re_part, im_part):
        t = jnp.stack([re_part, im_part], axis=1).reshape(2, 2, S5_GROUPS // g4, g4, S5_P)
        return t.transpose(0, 2, 1, 3, 4).reshape(2, S5W_SW)

    a1, a2 = state_cols(pr[lc], pr[lc]), state_cols(-pi[lc], pi[lc])
    lam_rows = jnp.concatenate([a1[:, None], a2[:, None], jnp.zeros((2, SUBLANES - 2, S5W_SW), F32)], axis=1)
    return tm, mm, nm, lam_rows


def _s5_carry_specs(rows):
    wide = pl.BlockSpec((rows, S5W_TILE), lambda j: (0, j))
    lam = pl.BlockSpec((SUBLANES, S5W_TILE), lambda j: (0, j))
    return wide, lam


def _s5_carry_fwd(e, lam_rows):
    rows = e.shape[0]
    steps = rows // S5_ROWS
    wide, lam = _s5_carry_specs(rows)

    def body(e_ref, l_ref, h_ref):
        a1, a2 = l_ref[0:1, :], l_ref[1:2, :]
        h_ref[0:S5_ROWS, :] = jnp.zeros((S5_ROWS, S5W_TILE), F32)

        def step(k, carry):
            at = pl.multiple_of((k - 1) * S5_ROWS, S5_ROWS)
            prev = h_ref[pl.ds(at, S5_ROWS), :]
            new = a1 * prev + a2 * pltpu.roll(prev, S5W_TILE // 2, 1) + e_ref[pl.ds(at, S5_ROWS), :]
            h_ref[pl.ds(pl.multiple_of(k * S5_ROWS, S5_ROWS), S5_ROWS), :] = new
            return carry

        lax.fori_loop(1, steps, step, 0)

    return pl.pallas_call(
        body, name="s5_carry_fwd", grid=(S5W_SW // S5W_TILE,), in_specs=[wide, lam], out_specs=wide,
        out_shape=jax.ShapeDtypeStruct(e.shape, F32),
        compiler_params=pltpu.CompilerParams(dimension_semantics=("parallel",), vmem_limit_bytes=VMEM_LIMIT),
    )(e, lam_rows)


def _s5_carry_bwd(dh, hin, lam_rows):
    rows = dh.shape[0]
    steps = rows // S5_ROWS
    wide, lam = _s5_carry_specs(rows)

    def body(dh_ref, h_ref, l_ref, de_ref, dl_ref):
        a1, a2 = l_ref[0:1, :], l_ref[1:2, :]
        last = (steps - 1) * S5_ROWS
        de_ref[last:last + S5_ROWS, :] = jnp.zeros((S5_ROWS, S5W_TILE), F32)

        def step(i, g):
            at = pl.multiple_of((steps - 2 - i) * S5_ROWS, S5_ROWS)
            de_ref[pl.ds(at, S5_ROWS), :] = g
            return dh_ref[pl.ds(at, S5_ROWS), :] + a1 * g + pltpu.roll(a2 * g, S5W_TILE // 2, 1)

        lax.fori_loop(0, steps - 1, step, dh_ref[last:last + S5_ROWS, :])
        hv, dev = h_ref[...], de_ref[...]
        da1 = jnp.sum(hv * dev, axis=0, keepdims=True)
        da2 = jnp.sum(pltpu.roll(hv, S5W_TILE // 2, 1) * dev, axis=0, keepdims=True)
        dl_ref[...] = jnp.concatenate([da1, da2, jnp.zeros((SUBLANES - 2, S5W_TILE), F32)], axis=0)

    return pl.pallas_call(
        body, name="s5_carry_bwd", grid=(S5W_SW // S5W_TILE,), in_specs=[wide, wide, lam], out_specs=[wide, lam],
        out_shape=[jax.ShapeDtypeStruct(dh.shape, F32), jax.ShapeDtypeStruct(lam_rows.shape, F32)],
        compiler_params=pltpu.CompilerParams(dimension_semantics=("parallel",), vmem_limit_bytes=VMEM_LIMIT),
    )(dh, hin, lam_rows)


@jax.custom_vjp
def _s5_scan(s, tm, mm, nm, lam_rows):
    hin = _s5_carry_fwd(_mm(s, mm, split=True), lam_rows)
    return _mm(hin, nm, init=_mm(s, tm, split=True), split=True)


def _s5_scan_fwd(s, tm, mm, nm, lam_rows):
    hin = _s5_carry_fwd(_mm(s, mm, split=True), lam_rows)
    return _mm(hin, nm, init=_mm(s, tm, split=True), split=True), (s, tm, mm, nm, lam_rows, hin)


def _s5_scan_bwd(res, dy):
    s, tm, mm, nm, lam_rows, hin = res
    de, dlam = _s5_carry_bwd(_mm(dy, nm, 'nt', split=True), hin, lam_rows)
    ds = _mm(de, mm, 'nt', init=_mm(dy, tm, 'nt', split=True), split=True)
    return ds, _mm(s, dy, 'tn', split=True), _mm(s, de, 'tn', split=True), _mm(hin, dy, 'tn', split=True), dlam


_s5_scan.defvjp(_s5_scan_fwd, _s5_scan_bwd)


def _s5_wide_group(u, u_c, lam_re, lam_im, log_dt, b_re, b_im, c_re, c_im, d_skip, glu_w, glu_b):
    bn, t_lat, _ = u.shape
    t_ctx = u_c.shape[1]
    n = t_ctx + t_lat
    steps = n // S5W_LC
    assert bn <= S5_ROWS and t_ctx % S5W_LC == 0 and t_lat % S5W_LC == 0
    ops = _s5_wide_operators(lam_re, lam_im, log_dt, b_re, b_im, c_re, c_im)
    seqs = [jnp.concatenate([u_c, u], axis=1), jnp.concatenate([jnp.flip(u_c, 1), jnp.flip(u, 1)], axis=1)]
    ys = []
    for d, seq in enumerate(seqs):
        s = jnp.pad(seq.reshape(bn, steps, S5W_RW).transpose(1, 0, 2), ((0, 0), (0, S5_ROWS - bn), (0, 0)))
        y = _s5_scan(s.reshape(steps * S5_ROWS, S5W_RW), *[op[d] for op in ops])
        ys.append(y.reshape(steps, S5_ROWS, S5W_RW)[:, :bn].transpose(1, 0, 2).reshape(bn, n, S5_W))

    def glu(yy):
        zz = jax.nn.gelu(yy)
        return zz * jax.nn.sigmoid(zz @ glu_w + glu_b)

    yl = d_skip * u + ys[0][:, t_ctx:] + jnp.flip(ys[1][:, t_ctx:], 1)
    yc = d_skip * u_c + ys[0][:, :t_ctx] + jnp.flip(ys[1][:, :t_ctx], 1)
    return glu(yl), glu(yc)


def _heads(proj, proj_c, hp, cos, sin, with_ctx_out):
    g_qkv, g_z, g_b, g_a, a_q, a_k, a_v, s_u = jnp.split(proj[..., :IN_COLS], list(IN_CUTS), axis=-1)
    c_qkv, c_z, c_b, c_a, c_q, c_k, c_v, c_u = jnp.split(proj_c[..., :IN_COLS], list(IN_CUTS), axis=-1)
    o_gdn, oc_gdn = _gdn_group(g_qkv, g_z, g_b, g_a, c_qkv, c_z, c_b, c_a, hp['gdn_conv_w'], hp['gdn_a_log'],
                               hp['gdn_dt_bias'], hp['gdn_norm_w'], with_ctx_out)
    o_att, oc_att = _attention_group(a_q, a_k, a_v, c_q, c_k, c_v, hp['q_norm_w'], hp['k_norm_w'], cos, sin,
                                     with_ctx_out)
    o_s5, oc_s5 = _s5_wide_group(s_u, c_u, hp['s5_lam_re'], hp['s5_lam_im'], hp['s5_log_dt'], hp['s5_b_re'],
                                 hp['s5_b_im'], hp['s5_c_re'], hp['s5_c_im'], hp['s5_d'], hp['glu_w'], hp['glu_b'])
    o = jnp.concatenate([o_gdn, o_att, o_s5], axis=-1)
    if not with_ctx_out:
        return (o,)
    return o, jnp.concatenate([oc_gdn, oc_att, oc_s5], axis=-1)


def _mixer_fwd(st, x, mod, t_ctx, w_in, w_out, lg, lb, hp, cos, sin, alpha):
    rows = st.bn * st.n
    h = _modulate(st, x, mod, 1).reshape(rows, st.d)
    proj = _mm(h, w_in).reshape(st.bn, st.n, IN_COLS_PAD)
    (o_lat, o_ctx), heads_vjp = jax.vjp(lambda p, pc, hp_: _heads(p, pc, hp_, cos, sin, True),
                                         proj[:, t_ctx:], proj[:, :t_ctx], hp)
    o = jnp.concatenate([o_ctx, o_lat], axis=1).reshape(rows, st.d)
    y = _mm(o, w_out).reshape(st.bn, st.n, st.d)
    return _post_norm(st, x, y, mod, 1, lg, lb, 1.0, alpha), (h, heads_vjp, o, y)


def _mixer_bwd(st, x, mod, t_ctx, w_in, w_out, lg, alpha, kept, dout):
    h, heads_vjp, o, y = kept
    rows = st.bn * st.n
    dxr, dy, dgate, dlg, dlb = _post_norm_bwd(st, x, y, dout, mod, 1, lg, 1.0, alpha)
    dy = dy.reshape(rows, st.d)
    dw_out = _mm(o, dy, 'tn')
    do = _mm(dy, w_out, 'nt').reshape(st.bn, st.n, st.d)
    dproj, dproj_c, dhp = heads_vjp((do[:, t_ctx:], do[:, :t_ctx]))
    dp = jnp.concatenate([dproj_c, dproj], axis=1).reshape(rows, IN_COLS_PAD)
    dw_in = _mm(h, dp, 'tn')
    dh = _mm(dp, w_in, 'nt').reshape(st.bn, st.n, st.d)
    dx, dsh, dsc = _modulate_bwd(st, dh, x, dxr, mod, 1)
    return dx, (dsh, dsc, dgate), dw_in, dw_out, jnp.sum(dlg, axis=(0, 1)), jnp.sum(dlb, axis=(0, 1)), dhp


def _natural(name, gathered):
    ax = SHARD_AXIS[name]
    t = jnp.moveaxis(gathered, 0, ax)
    return t.reshape(t.shape[:ax] + (t.shape[ax] * t.shape[ax + 1],) + t.shape[ax + 2:])


def _to_shards(name, full):
    ax = SHARD_AXIS[name]
    t = full.reshape(full.shape[:ax] + (N_CHIPS, full.shape[ax] // N_CHIPS) + full.shape[ax + 1:])
    return jnp.moveaxis(t, ax, 0)


def _pack(arrays):
    flat = jnp.concatenate([a.reshape(-1) for a in arrays])
    pad = (-flat.size) % (SUBLANES * LANES)
    return jnp.pad(flat, (0, pad)).reshape(-1, LANES)


def _unpack(packed, like):
    flat = packed.reshape(-1)
    out, at = [], 0
    for a in like:
        out.append(flat[at:at + a.size].reshape(a.shape))
        at += a.size
    return out


def kernel(x, c, ctx, c_ctx, w_ada, b_ada, ln_g, ln_b, ffn_w1, ffn_w3, ffn_w2, w_in, w_out, gdn_conv_w, gdn_a_log, gdn_dt_bias, gdn_norm_w, q_norm_w, k_norm_w, s5_lam_re, s5_lam_im, s5_log_dt, s5_b_re, s5_b_im, s5_c_re, s5_c_im, s5_d, glu_w, glu_b, loss_target, m_c_ctx, m_w_ada, m_b_ada, m_ln_g, m_ln_b, m_ffn_w1, m_ffn_w3, m_ffn_w2, m_w_in, m_w_out, m_gdn_conv_w, m_gdn_a_log, m_gdn_dt_bias, m_gdn_norm_w, m_q_norm_w, m_k_norm_w, m_s5_lam_re, m_s5_lam_im, m_s5_log_dt, m_s5_b_re, m_s5_b_im, m_s5_c_re, m_s5_c_im, m_s5_d, m_glu_w, m_glu_b, v_c_ctx, v_w_ada, v_b_ada, v_ln_g, v_ln_b, v_ffn_w1, v_ffn_w3, v_ffn_w2, v_w_in, v_w_out, v_gdn_conv_w, v_gdn_a_log, v_gdn_dt_bias, v_gdn_norm_w, v_q_norm_w, v_k_norm_w, v_s5_lam_re, v_s5_lam_im, v_s5_log_dt, v_s5_b_re, v_s5_b_im, v_s5_c_re, v_s5_c_im, v_s5_d, v_glu_w, v_glu_b):
    given = dict(locals())
    w = {n: given[n] for n in WEIGHTS}
    mom = {n: given['m_' + n] for n in WEIGHTS}
    var = {n: given['v_' + n] for n in WEIGHTS}
    depth = w_ada.shape[0]
    bn, t_lat, d = x.shape
    alpha = (2.0 * depth) ** 0.25
    chip = 2 * lax.axis_index("x") + lax.axis_index("y")

    gathered = _gather_weight_shards([w[n].astype(BF16) for n in BIG])
    full = {n: _natural(n, g) for n, g in zip(BIG, gathered)}
    full['w_in'] = jnp.pad(full['w_in'], ((0, 0), (0, 0), (0, IN_COLS_PAD - IN_COLS)))
    small_sh = [w[n] for n in SMALL_SHARDED]
    small_all = _all_gather8(_pack(small_sh))
    for n, parts in zip(SMALL_SHARDED, zip(*[_unpack(small_all[2 * j], small_sh) for j in range(N_CHIPS)])):
        full[n] = _natural(n, jnp.stack(parts))
    for n in REPLICATED:
        full[n] = w[n]

    cos, sin = _axial_rope_tables(t_lat // GRID_W)
    act = jnp.zeros((ADA_ROWS, d), F32).at[:bn].set(jax.nn.silu(c)).at[bn].set(jax.nn.silu(c_ctx))
    t_ctx = ctx.shape[1]
    st = _Stream(bn, t_ctx, t_lat, d)
    xs = jnp.concatenate([ctx, x], axis=1)
    saved = []
    for l in range(depth):
        mod = (_mm(act, full['w_ada'][l]) + full['b_ada'][l]).reshape(ADA_ROWS, N_MOD, d)
        hp = {n: full[n][l] for n in HEAD_PARAMS}
        lg, lb = full['ln_g'][l], full['ln_b'][l]
        f1 = (full['ffn_w1'][l], full['ffn_w3'][l], full['ffn_w2'][l])
        x0 = xs
        x1, kept1 = _ffn_fwd(st, x0, mod, 0, f1[0][0], f1[1][0], f1[2][0], lg[0], lb[0], alpha)
        x2, kept2 = _mixer_fwd(st, x1, mod, t_ctx, full['w_in'][l], full['w_out'][l], lg[1], lb[1], hp, cos, sin, alpha)
        xs, kept3 = _ffn_fwd(st, x2, mod, 2, f1[0][1], f1[1][1], f1[2][1], lg[2], lb[2], alpha)
        saved.append((x0, x1, x2, mod, kept1, kept2, kept3))

    err = xs[:, t_ctx:] - loss_target
    loss = lax.psum(0.5 * jnp.sum(jnp.mean(err * err, axis=-1)), ("x", "y", "c"))
    dxs = jnp.concatenate([jnp.zeros((bn, t_ctx, d), F32), err / d], axis=1)

    grads = {n: [None] * depth for n in WEIGHTS if n != 'c_ctx'}
    dact = jnp.zeros((ADA_ROWS, d), F32)
    for l in reversed(range(depth)):
        x0, x1, x2, mod, kept1, kept2, kept3 = saved[l]
        lg = full['ln_g'][l]
        f1 = (full['ffn_w1'][l], full['ffn_w3'][l], full['ffn_w2'][l])
        dx2, dm3, dw1b, dw3b, dw2b, dlg2, dlb2 = _ffn_bwd(st, x2, mod, 2, f1[0][1], f1[1][1], f1[2][1], lg[2], alpha,
                                                           kept3, dxs)
        dx1, dm2, dw_in, dw_out, dlg1, dlb1, dhp = _mixer_bwd(st, x1, mod, t_ctx, full['w_in'][l], full['w_out'][l],
                                                              lg[1], alpha, kept2, dx2)
        dxs, dm1, dw1a, dw3a, dw2a, dlg0, dlb0 = _ffn_bwd(st, x0, mod, 0, f1[0][0], f1[1][0], f1[2][0], lg[0], alpha,
                                                          kept1, dx1)
        parts = jnp.stack([t[:, :, 0] for grp in (dm1, dm2, dm3) for t in grp], axis=2)
        dmod = jnp.concatenate([parts[:, 1], jnp.sum(parts[:, 0], axis=0, keepdims=True),
                                jnp.zeros((ADA_ROWS - bn - 1, N_MOD, d), F32)], axis=0).reshape(ADA_ROWS, N_MOD * d)
        grads['w_ada'][l] = _mm(act, dmod, 'tn')
        grads['b_ada'][l] = jnp.sum(dmod, axis=0)
        dact = dact + _mm(dmod, full['w_ada'][l], 'nt')
        grads['ffn_w1'][l] = jnp.stack([dw1a, dw1b])
        grads['ffn_w3'][l] = jnp.stack([dw3a, dw3b])
        grads['ffn_w2'][l] = jnp.stack([dw2a, dw2b])
        grads['w_in'][l] = dw_in[:, :IN_COLS]
        grads['w_out'][l] = dw_out
        grads['ln_g'][l] = jnp.stack([dlg0, dlg1, dlg2])
        grads['ln_b'][l] = jnp.stack([dlb0, dlb1, dlb2])
        for n in HEAD_PARAMS:
            grads[n][l] = dhp[n]
    grad = {n: jnp.stack(g) for n, g in grads.items()}
    sig = jax.nn.sigmoid(c_ctx)
    grad['c_ctx'] = dact[bn] * (sig * (1.0 + c_ctx * (1.0 - sig)))

    half = depth // 2
    laid = []
    for n in BIG:
        s = _to_shards(n, grad[n])
        s = s.reshape((N_CHIPS, 2, half) + s.shape[2:])
        laid.append(jnp.moveaxis(s, 1, 0))
    theirs = _swap_halves(laid)
    pair = [_add_sibling(g, r) for g, r in zip(laid, theirs)]
    landed = _scatter_partials([lo for _, lo in pair])
    mine = [_add_chips(p, r) for (p, _), r in zip(pair, landed)]
    reduced = dict(zip(BIG, _share_halves(mine)))

    small_names = REPLICATED + SMALL_SHARDED
    small_grads = [grad[n] for n in small_names]
    summed = _unpack(_sum_devices(_all_gather8(_pack(small_grads))), small_grads)
    for n, g in zip(small_names, summed):
        if n in SMALL_SHARDED:
            ax = SHARD_AXIS[n]
            width = g.shape[ax] // N_CHIPS
            g = lax.dynamic_slice_in_dim(g, chip * width, width, axis=ax)
        reduced[n] = g

    delta, new_m, new_v = {}, {}, {}
    for n in BIG:
        delta[n], new_m[n], new_v[n] = _adamw(w[n], reduced[n], mom[n], var[n])
    packs = [_pack([t[n] for n in small_names]) for t in (w, reduced, mom, var)]
    like = [w[n] for n in small_names]
    for res, packed in zip((delta, new_m, new_v), _adamw(*packs)):
        res.update(zip(small_names, _unpack(packed, like)))

    return (loss, dxs[:, t_ctx:], *[reduced[n] for n in WEIGHTS], *[delta[n] for n in WEIGHTS],
            *[new_m[n] for n in WEIGHTS], *[new_v[n] for n in WEIGHTS])
```

```python
import functools
import math

import jax
import jax.numpy as jnp
from jax import lax
from jax.experimental import pallas as pl
from jax.experimental.pallas import tpu as pltpu

F32 = jnp.float32
BF16 = jnp.bfloat16
MESH_IDS = pl.DeviceIdType.MESH
ANY = pl.BlockSpec(memory_space=pl.ANY)
N_CHIPS = 4
N_DEV = 8
LANES = 128
SUBLANES = 8
ADA_ROWS = 128
VMEM_LIMIT = 48 * 1024 * 1024

D_MODEL = 1024
GRID_W = 64
GDN_HEADS = 6
GDN_DK = 64
GDN_DV = 64
GDN_W = GDN_HEADS * GDN_DV
GDN_QKV = GDN_HEADS * (2 * GDN_DK + GDN_DV)
CONV_K = 5
CHUNK = 64
ATT_HEADS = 6
ATT_KV_HEADS = 2
ATT_DH = 64
ATT_W = ATT_HEADS * ATT_DH
ATT_GROUP = ATT_HEADS // ATT_KV_HEADS
Q_BLOCK = 128
ROPE_THETA = 10000.0
ROPE_PAIRS = ATT_DH // 4
S5_GROUPS = 16
S5_GH = 16
S5_P = 64
S5_W = S5_GROUPS * S5_GH
S5_LC = 32
S5_ROWS = SUBLANES
S5_CW = S5_LC * S5_GH
S5_SW = 2 * S5_P
S5W_LC = 8
S5W_RW = S5W_LC * S5_W
S5W_SW = S5_GROUPS * S5_SW
S5W_TILE = 4 * S5_SW
GDN_PREP_CHUNKS = 6
GDN_SCAN_CHAINS = 16
HI = lax.Precision.HIGHEST
BF16X3 = lax.Precision.HIGH
N_MOD = 9
EPS = 1e-6
OFF_GDN_Z = GDN_QKV
OFF_GDN_B = OFF_GDN_Z + GDN_W
OFF_GDN_A = OFF_GDN_B + 2 * GDN_HEADS
OFF_ATT_Q = OFF_GDN_A + 2 * GDN_HEADS
OFF_ATT_K = OFF_ATT_Q + ATT_W
OFF_ATT_V = OFF_ATT_K + ATT_KV_HEADS * ATT_DH
OFF_S5 = OFF_ATT_V + ATT_KV_HEADS * ATT_DH
IN_COLS = OFF_S5 + S5_W
IN_COLS_PAD = 2560
IN_CUTS = (OFF_GDN_Z, OFF_GDN_B, OFF_GDN_A, OFF_ATT_Q, OFF_ATT_K, OFF_ATT_V, OFF_S5)

ADAM_LR = 0.001
ADAM_B1 = 0.9
ADAM_B2 = 0.999
ADAM_EPS = 1e-08
ADAM_WD = 0.01
ADAM_STEP = 10

WEIGHTS = ['c_ctx', 'w_ada', 'b_ada', 'ln_g', 'ln_b', 'ffn_w1', 'ffn_w3', 'ffn_w2', 'w_in', 'w_out', 'gdn_conv_w',
           'gdn_a_log', 'gdn_dt_bias', 'gdn_norm_w', 'q_norm_w', 'k_norm_w', 's5_lam_re', 's5_lam_im', 's5_log_dt',
           's5_b_re', 's5_b_im', 's5_c_re', 's5_c_im', 's5_d', 'glu_w', 'glu_b']
BIG = ['w_ada', 'ffn_w1', 'ffn_w3', 'ffn_w2', 'w_in', 'w_out']
SMALL_SHARDED = ['ln_g', 'ln_b', 'gdn_conv_w', 'glu_w']
SHARD_AXIS = {'w_ada': 2, 'ffn_w1': 3, 'ffn_w3': 3, 'ffn_w2': 2, 'w_in': 2, 'w_out': 1,
              'ln_g': 2, 'ln_b': 2, 'gdn_conv_w': 2, 'glu_w': 1}
REPLICATED = [n for n in WEIGHTS if n not in BIG and n not in SMALL_SHARDED]
HEAD_PARAMS = ['gdn_conv_w', 'gdn_a_log', 'gdn_dt_bias', 'gdn_norm_w', 'q_norm_w', 'k_norm_w', 's5_lam_re',
               's5_lam_im', 's5_log_dt', 's5_b_re', 's5_b_im', 's5_c_re', 's5_c_im', 's5_d', 'glu_w', 'glu_b']


def _place():
    return lax.axis_index("x"), lax.axis_index("y"), lax.axis_index("c")


def _pick(n, cands):
    for t in cands:
        if n % t == 0:
            return t
    return n


def _remote(src, dst, send_sem, recv_sem, to):
    return pltpu.make_async_remote_copy(src_ref=src, dst_ref=dst, send_sem=send_sem, recv_sem=recv_sem,
                                        device_id=to, device_id_type=MESH_IDS)


def _gather_weight_shards(shards):
    n = len(shards)
    halves = [s.shape[0] // 2 for s in shards]

    def body(*refs):
        ins, outs = refs[:n], refs[n:2 * n]
        send_sems, recv_sems = refs[2 * n:]
        x, y, c = _place()
        chip = 2 * x + y
        sibling = (x, y, 1 - c)
        others = [(1 - x, y), (x, 1 - y), (1 - x, 1 - y)]
        sends = []
        for i in range(n):
            h = halves[i]
            for j, (px, py) in enumerate(others):
                k = 6 * i + j
                cp = _remote(ins[i].at[pl.ds(c * h, h)], outs[i].at[chip, pl.ds(c * h, h)],
                             send_sems.at[k], recv_sems.at[k], (px, py, c))
                cp.start()
                sends.append(cp)
        for i in range(n):
            h = halves[i]
            for j, (px, py) in enumerate(others):
                slab = outs[i].at[2 * px + py, pl.ds(c * h, h)]
                _remote(slab, slab, send_sems.at[6 * i + j], recv_sems.at[6 * i + j], (px, py, c)).wait_recv()
                fw = _remote(slab, slab, send_sems.at[6 * i + 3 + j], recv_sems.at[6 * i + 3 + j], sibling)
                fw.start()
                sends.append(fw)
        for i in range(n):
            h = halves[i]
            for j, (px, py) in enumerate(others):
                slab = outs[i].at[2 * px + py, pl.ds((1 - c) * h, h)]
                _remote(slab, slab, send_sems.at[6 * i + 3 + j], recv_sems.at[6 * i + 3 + j], sibling).wait_recv()
        for cp in sends:
            cp.wait_send()

    gathered = pl.pallas_call(
        body, name="gather_weight_shards",
        out_shape=[jax.ShapeDtypeStruct((N_CHIPS,) + s.shape, s.dtype) for s in shards],
        in_specs=[ANY] * n, out_specs=[ANY] * n,
        scratch_shapes=[pltpu.SemaphoreType.DMA((6 * n,)), pltpu.SemaphoreType.DMA((6 * n,))],
    )(*shards)
    chip = 2 * lax.axis_index("x") + lax.axis_index("y")
    return [lax.dynamic_update_slice_in_dim(g, s[None], chip, axis=0) for g, s in zip(gathered, shards)]


def _all_gather8(v):
    def body(v_ref, out_ref, send_sems, recv_sems, local_sem):
        x, y, c = _place()
        me, sibling = (x, y, c), (x, y, 1 - c)
        chips = [(1 - x, y), (x, 1 - y), (1 - x, 1 - y)]

        def slot(px, py, pc):
            return out_ref.at[4 * px + 2 * py + pc]

        def copy(k, block, to, src=None):
            return _remote(slot(*block) if src is None else src, slot(*block), send_sems.at[k], recv_sems.at[k], to)

        mine = pltpu.make_async_copy(v_ref, slot(*me), local_sem)
        mine.start()
        first = [copy(0, me, sibling, src=v_ref)]
        first += [copy(1 + j, me, (*chip, c), src=v_ref) for j, chip in enumerate(chips)]
        for cp in first:
            cp.start()
        passed = [copy(4 + j, (*chip, c), sibling) for j, chip in enumerate(chips)]
        for j, chip in enumerate(chips):
            copy(1 + j, (*chip, c), me).wait_recv()
            passed[j].start()
        copy(0, sibling, me).wait_recv()
        for j, chip in enumerate(chips):
            copy(4 + j, (*chip, 1 - c), me).wait_recv()
        for cp in first + passed:
            cp.wait_send()
        mine.wait()

    return pl.pallas_call(
        body, name="all_gather8",
        out_shape=jax.ShapeDtypeStruct((N_DEV,) + v.shape, v.dtype),
        in_specs=[ANY], out_specs=ANY,
        scratch_shapes=[pltpu.SemaphoreType.DMA((7,)), pltpu.SemaphoreType.DMA((7,)), pltpu.SemaphoreType.DMA],
    )(v)


def _swap_halves(grads):
    n = len(grads)

    def body(*refs):
        ins, outs = refs[:n], refs[n:2 * n]
        send_sems, recv_sems = refs[2 * n:]
        x, y, c = _place()
        cps = [_remote(ins[i].at[1 - c], outs[i], send_sems.at[i], recv_sems.at[i], (x, y, 1 - c)) for i in range(n)]
        for cp in cps:
            cp.start()
        for cp in cps:
            cp.wait()

    return pl.pallas_call(
        body, name="swap_halves",
        out_shape=[jax.ShapeDtypeStruct(g.shape[1:], g.dtype) for g in grads],
        in_specs=[ANY] * n, out_specs=[ANY] * n,
        scratch_shapes=[pltpu.SemaphoreType.DMA((n,)), pltpu.SemaphoreType.DMA((n,))],
    )(*grads)


def _scatter_partials(parts):
    n = len(parts)

    def body(*refs):
        ins, outs = refs[:n], refs[n:2 * n]
        send_sems, recv_sems = refs[2 * n:]
        x, y, c = _place()
        others = [(1 - x, y), (x, 1 - y), (1 - x, 1 - y)]
        cps = []
        for i in range(n):
            for j, (px, py) in enumerate(others):
                k = 3 * i + j
                cps.append(_remote(ins[i].at[2 * px + py], outs[i].at[j], send_sems.at[k], recv_sems.at[k], (px, py, c)))
        for cp in cps:
            cp.start()
        for cp in cps:
            cp.wait()

    return pl.pallas_call(
        body, name="scatter_partials",
        out_shape=[jax.ShapeDtypeStruct((3,) + p.shape[1:], p.dtype) for p in parts],
        in_specs=[ANY] * n, out_specs=[ANY] * n,
        scratch_shapes=[pltpu.SemaphoreType.DMA((3 * n,)), pltpu.SemaphoreType.DMA((3 * n,))],
    )(*parts)


def _share_halves(halves):
    n = len(halves)

    def body(*refs):
        ins, outs = refs[:n], refs[n:2 * n]
        send_sems, recv_sems = refs[2 * n:]
        x, y, c = _place()
        cps = [_remote(ins[i], outs[i], send_sems.at[i], recv_sems.at[i], (x, y, 1 - c)) for i in range(n)]
        for cp in cps:
            cp.start()
        for cp in cps:
            cp.wait()

    theirs = pl.pallas_call(
        body, name="share_halves",
        out_shape=[jax.ShapeDtypeStruct(p.shape, p.dtype) for p in halves],
        in_specs=[ANY] * n, out_specs=[ANY] * n,
        scratch_shapes=[pltpu.SemaphoreType.DMA((n,)), pltpu.SemaphoreType.DMA((n,))],
    )(*halves)
    south = lax.axis_index("c") == 0
    return [jnp.concatenate([jnp.where(south, a, b), jnp.where(south, b, a)], axis=0) for a, b in zip(halves, theirs)]


def _row_tile(rows, cols, n_arrays):
    budget = (VMEM_LIMIT // 3) // (2 * n_arrays * 4 * max(cols, LANES))
    for t in (1024, 512, 256, 128, 64, 32, 16, 8):
        if t <= budget and rows % t == 0:
            return t
    return rows


def _add_sibling(grad, recv):
    cols = grad.shape[-1]
    rows = recv.size // cols
    g3 = grad.reshape(2, rows, cols)
    r2 = recv.reshape(rows, cols)
    tr = _row_tile(rows, cols, 4)

    def body(c_ref, g_ref, r_ref, o_ref, lo_ref):
        total = g_ref[...] + r_ref[...]
        o_ref[...] = total
        lo_ref[...] = total.astype(BF16)

    spec = pl.BlockSpec((tr, cols), lambda i, c_ref: (i, 0))
    out, lo = pl.pallas_call(
        body, name="add_sibling",
        grid_spec=pltpu.PrefetchScalarGridSpec(
            num_scalar_prefetch=1, grid=(rows // tr,),
            in_specs=[pl.BlockSpec((None, tr, cols), lambda i, c_ref: (c_ref[0], i, 0)), spec],
            out_specs=[spec, spec]),
        out_shape=[jax.ShapeDtypeStruct((rows, cols), F32), jax.ShapeDtypeStruct((rows, cols), BF16)],
        compiler_params=pltpu.CompilerParams(vmem_limit_bytes=VMEM_LIMIT),
    )(lax.axis_index("c").astype(jnp.int32).reshape(1), g3, r2)
    return out.reshape(recv.shape), lo.reshape(recv.shape)


def _add_chips(part, recv):
    cols = part.shape[-1]
    rows = part[0].size // cols
    p3 = part.reshape(N_CHIPS, rows, cols)
    r3 = recv.reshape(3, rows, cols)
    tr = _row_tile(rows, cols, 5)

    def body(chip_ref, p_ref, r0_ref, r1_ref, r2_ref, o_ref):
        o_ref[...] = ((p_ref[...] + r0_ref[...].astype(F32)) + r1_ref[...].astype(F32)) + r2_ref[...].astype(F32)

    def recv_spec(j):
        return pl.BlockSpec((None, tr, cols), lambda i, chip_ref: (j, i, 0))

    chip = (2 * lax.axis_index("x") + lax.axis_index("y")).astype(jnp.int32).reshape(1)
    out = pl.pallas_call(
        body, name="add_chips",
        grid_spec=pltpu.PrefetchScalarGridSpec(
            num_scalar_prefetch=1, grid=(rows // tr,),
            in_specs=[pl.BlockSpec((None, tr, cols), lambda i, chip_ref: (chip_ref[0], i, 0)),
                      recv_spec(0), recv_spec(1), recv_spec(2)],
            out_specs=pl.BlockSpec((tr, cols), lambda i, chip_ref: (i, 0))),
        out_shape=jax.ShapeDtypeStruct((rows, cols), F32),
        compiler_params=pltpu.CompilerParams(vmem_limit_bytes=VMEM_LIMIT),
    )(chip, p3, r3, r3, r3)
    return out.reshape(part.shape[1:])


def _sum_devices(gathered):
    _, rows, cols = gathered.shape
    tr = _row_tile(rows, cols, 9)

    def body(g_ref, o_ref):
        acc = g_ref[0]
        for k in range(1, N_DEV):
            acc = acc + g_ref[k]
        o_ref[...] = acc

    return pl.pallas_call(
        body, name="sum_devices", grid=(rows // tr,),
        in_specs=[pl.BlockSpec((N_DEV, tr, cols), lambda i: (0, i, 0))],
        out_specs=pl.BlockSpec((tr, cols), lambda i: (i, 0)),
        out_shape=jax.ShapeDtypeStruct((rows, cols), F32),
        compiler_params=pltpu.CompilerParams(vmem_limit_bytes=VMEM_LIMIT),
    )(gathered)


def _adamw(w, g, m, v):
    shape = w.shape
    cols = shape[-1]
    rows = w.size // cols
    tr = _row_tile(rows, cols, 7)

    def body(w_ref, g_ref, m_ref, v_ref, d_ref, nm_ref, nv_ref):
        gv = g_ref[...]
        nm = ADAM_B1 * m_ref[...] + (1.0 - ADAM_B1) * gv
        nv = ADAM_B2 * v_ref[...] + (1.0 - ADAM_B2) * (gv * gv)
        m_hat = nm / (1.0 - ADAM_B1 ** ADAM_STEP)
        v_hat = nv / (1.0 - ADAM_B2 ** ADAM_STEP)
        d_ref[...] = -ADAM_LR * (m_hat / (jnp.sqrt(v_hat) + ADAM_EPS) + ADAM_WD * w_ref[...])
        nm_ref[...] = nm
        nv_ref[...] = nv

    spec = pl.BlockSpec((tr, cols), lambda i: (i, 0))
    outs = pl.pallas_call(
        body, name="adamw", grid=(rows // tr,),
        in_specs=[spec] * 4, out_specs=[spec] * 3,
        out_shape=[jax.ShapeDtypeStruct((rows, cols), F32)] * 3,
        compiler_params=pltpu.CompilerParams(vmem_limit_bytes=VMEM_LIMIT),
    )(*[t.reshape(rows, cols) for t in (w, g, m, v)])
    return tuple(o.reshape(shape) for o in outs)


_DOT_DIMS = {'nn': (((1,), (0,)), ((), ())), 'nt': (((1,), (1,)), ((), ())), 'tn': (((0,), (0,)), ((), ()))}


def _mm(a, b, mode='nn', a_gate=None, init=None):
    if mode == 'nn':
        (m, k), (_, n) = a.shape, b.shape
    elif mode == 'nt':
        (m, k), (n, _) = a.shape, b.shape
    else:
        (k, m), (_, n) = a.shape, b.shape
    tn = _pick(n, (1408, 1280, 1024, 512, 256, 128))
    if mode == 'tn':
        tm = _pick(m, (1408, 1024, 512, 256, 128))
        tk = _pick(k, (512, 256, 128, 64, 32, 16, 8))
    else:
        tm = _pick(m, (1024, 768, 512, 256, 128, 64, 32, 16, 8))
        tk = _pick(k, (1408, 1280, 1024, 512, 256, 128))
    nk = k // tk

    gated, seeded = a_gate is not None, init is not None

    def body(*refs):
        a_ref, b_ref = refs[0], refs[1 + gated]
        o_ref, acc_ref = refs[-2], refs[-1]
        step = pl.program_id(2)

        @pl.when(step == 0)
        def _():
            acc_ref[...] = refs[2 + gated][...] if seeded else jnp.zeros_like(acc_ref)

        left = a_ref[...]
        if gated:
            left = left * jax.nn.sigmoid(left) * refs[1][...]
        acc_ref[...] += lax.dot_general(left.astype(BF16), b_ref[...].astype(BF16), _DOT_DIMS[mode],
                                        preferred_element_type=F32)

        @pl.when(step == nk - 1)
        def _():
            o_ref[...] = acc_ref[...]

    if mode == 'tn':
        a_spec = pl.BlockSpec((tk, tm), lambda i, j, s: (s, i))
    else:
        a_spec = pl.BlockSpec((tm, tk), lambda i, j, s: (i, s))
    if mode == 'nt':
        b_spec = pl.BlockSpec((tn, tk), lambda i, j, s: (j, s))
    else:
        b_spec = pl.BlockSpec((tk, tn), lambda i, j, s: (s, j))
    o_spec = pl.BlockSpec((tm, tn), lambda i, j, s: (i, j))
    operands = [a] + ([a_gate] if gated else []) + [b] + ([init] if seeded else [])
    return pl.pallas_call(
        body, name=f"mm_{mode}{'_gated' if gated else ''}{'_seeded' if seeded else ''}_{m}x{k}x{n}",
        grid=(m // tm, n // tn, nk),
        in_specs=[a_spec] * (1 + gated) + [b_spec] + [o_spec] * seeded, out_specs=o_spec,
        out_shape=jax.ShapeDtypeStruct((m, n), F32),
        scratch_shapes=[pltpu.VMEM((tm, tn), F32)],
        compiler_params=pltpu.CompilerParams(dimension_semantics=("parallel", "parallel", "arbitrary"),
                                             vmem_limit_bytes=VMEM_LIMIT),
    )(*operands)


class _Stream:
    def __init__(self, bn, t_ctx, t_lat, d):
        self.bn, self.n, self.d = bn, t_ctx + t_lat, d
        self.tr = _pick(math.gcd(t_ctx, t_lat), (256, 128, 64, 32, 16, 8))
        self.ctx_tiles = t_ctx // self.tr
        self.grid = (bn, self.n // self.tr)
        nct, rows = self.ctx_tiles, bn
        self.tok = pl.BlockSpec((None, self.tr, d), lambda b, i: (b, i, 0))
        self.mod = pl.BlockSpec((None, N_MOD, d), lambda b, i: (jnp.where(i < nct, rows, b), 0, 0))
        self.vec = pl.BlockSpec((1, d), lambda b, i: (0, 0))
        self.part = pl.BlockSpec((None, None, 1, d), lambda b, i: (b, jnp.where(i < nct, 0, 1), 0, 0))
        self.per_example = pl.BlockSpec((None, 1, d), lambda b, i: (b, 0, 0))
        self.tok_shape = jax.ShapeDtypeStruct((bn, self.n, d), F32)
        self.part_shape = jax.ShapeDtypeStruct((bn, 2, 1, d), F32)
        self.example_shape = jax.ShapeDtypeStruct((bn, 1, d), F32)
        self.params = pltpu.CompilerParams(dimension_semantics=("parallel", "arbitrary"), vmem_limit_bytes=VMEM_LIMIT)

    def starts_part(self, i):
        return (i == 0) | (i == self.ctx_tiles)


def _modulate(st, x, mod, k):
    def body(x_ref, m_ref, o_ref):
        o_ref[...] = x_ref[...] * (1.0 + m_ref[3 * k + 1:3 * k + 2, :]) + m_ref[3 * k:3 * k + 1, :]

    return pl.pallas_call(body, name=f"modulate_{k}", grid=st.grid, in_specs=[st.tok, st.mod], out_specs=st.tok,
                          out_shape=st.tok_shape, compiler_params=st.params)(x, mod)


def _norm_stats(z):
    mu = jnp.mean(z, axis=-1, keepdims=True)
    zc = z - mu
    rstd = lax.rsqrt(jnp.mean(zc * zc, axis=-1, keepdims=True) + EPS)
    return zc * rstd, rstd


def _post_norm(st, x, y, mod, k, lg, lb, rw, alpha):
    def body(x_ref, y_ref, m_ref, g_ref, b_ref, o_ref):
        xhat, _ = _norm_stats(alpha * x_ref[...] + (rw * m_ref[3 * k + 2:3 * k + 3, :]) * y_ref[...])
        o_ref[...] = xhat * g_ref[...] + b_ref[...]

    return pl.pallas_call(body, name=f"post_norm_{k}", grid=st.grid,
                          in_specs=[st.tok, st.tok, st.mod, st.vec, st.vec], out_specs=st.tok,
                          out_shape=st.tok_shape, compiler_params=st.params)(x, y, mod, lg[None], lb[None])


def _post_norm_bwd(st, x, y, dout, mod, k, lg, rw, alpha):
    def body(x_ref, y_ref, do_ref, m_ref, g_ref, dxr_ref, dy_ref, dgate_ref, dlg_ref, dlb_ref):
        i = pl.program_id(1)
        gate = rw * m_ref[3 * k + 2:3 * k + 3, :]
        yv, dov = y_ref[...], do_ref[...]
        xhat, rstd = _norm_stats(alpha * x_ref[...] + gate * yv)
        dxhat = dov * g_ref[...]
        dz = rstd * (dxhat - jnp.mean(dxhat, axis=-1, keepdims=True)
                     - xhat * jnp.mean(dxhat * xhat, axis=-1, keepdims=True))
        dxr_ref[...] = alpha * dz
        dy_ref[...] = gate * dz

        @pl.when(i == 0)
        def _():
            dlg_ref[...] = jnp.zeros_like(dlg_ref)
            dlb_ref[...] = jnp.zeros_like(dlb_ref)

        @pl.when(st.starts_part(i))
        def _():
            dgate_ref[...] = jnp.zeros_like(dgate_ref)

        dlg_ref[...] += jnp.sum(dov * xhat, axis=0, keepdims=True)
        dlb_ref[...] += jnp.sum(dov, axis=0, keepdims=True)
        dgate_ref[...] += jnp.sum(rw * yv * dz, axis=0, keepdims=True)

    return pl.pallas_call(
        body, name=f"post_norm_bwd_{k}", grid=st.grid, in_specs=[st.tok, st.tok, st.tok, st.mod, st.vec],
        out_specs=[st.tok, st.tok, st.part, st.per_example, st.per_example],
        out_shape=[st.tok_shape, st.tok_shape, st.part_shape, st.example_shape, st.example_shape],
        compiler_params=st.params)(x, y, dout, mod, lg[None])


def _modulate_bwd(st, dh, x, dxr, mod, k):
    def body(dh_ref, x_ref, dxr_ref, m_ref, dx_ref, dsh_ref, dsc_ref):
        dhv = dh_ref[...]
        dx_ref[...] = dxr_ref[...] + dhv * (1.0 + m_ref[3 * k + 1:3 * k + 2, :])

        @pl.when(st.starts_part(pl.program_id(1)))
        def _():
            dsh_ref[...] = jnp.zeros_like(dsh_ref)
            dsc_ref[...] = jnp.zeros_like(dsc_ref)

        dsh_ref[...] += jnp.sum(dhv, axis=0, keepdims=True)
        dsc_ref[...] += jnp.sum(dhv * x_ref[...], axis=0, keepdims=True)

    return pl.pallas_call(
        body, name=f"modulate_bwd_{k}", grid=st.grid, in_specs=[st.tok, st.tok, st.tok, st.mod],
        out_specs=[st.tok, st.part, st.part], out_shape=[st.tok_shape, st.part_shape, st.part_shape],
        compiler_params=st.params)(dh, x, dxr, mod)


def _swiglu_bwd(ds, a, b):
    rows, cols = a.shape
    tr = _row_tile(rows, cols, 5)

    def body(ds_ref, a_ref, b_ref, da_ref, db_ref):
        av, dsv = a_ref[...], ds_ref[...]
        sig = jax.nn.sigmoid(av)
        da_ref[...] = dsv * b_ref[...] * (sig * (1.0 + av * (1.0 - sig)))
        db_ref[...] = dsv * (av * sig)

    spec = pl.BlockSpec((tr, cols), lambda i: (i, 0))
    return pl.pallas_call(body, name="swiglu_bwd", grid=(rows // tr,), in_specs=[spec] * 3, out_specs=[spec] * 2,
                          out_shape=[jax.ShapeDtypeStruct(a.shape, F32)] * 2,
                          compiler_params=pltpu.CompilerParams(vmem_limit_bytes=VMEM_LIMIT))(ds, a, b)


def _rms_norm(x, w):
    return x * lax.rsqrt(jnp.mean(x * x, axis=-1, keepdims=True) + EPS) * w


def _l2_normalize(x):
    return x * lax.rsqrt(jnp.sum(x * x, axis=-1, keepdims=True) + EPS)


def _ffn_fwd(st, x, mod, k, w1, w3, w2, lg, lb, alpha):
    rows = st.bn * st.n
    h = _modulate(st, x, mod, k).reshape(rows, st.d)
    a, b = _mm(h, w1), _mm(h, w3)
    y = _mm(a, w2, a_gate=b).reshape(st.bn, st.n, st.d)
    return _post_norm(st, x, y, mod, k, lg, lb, 0.5, alpha), (h, a, b, y)


def _ffn_bwd(st, x, mod, k, w1, w3, w2, lg, alpha, kept, dout):
    h, a, b, y = kept
    rows = st.bn * st.n
    dxr, dy, dgate, dlg, dlb = _post_norm_bwd(st, x, y, dout, mod, k, lg, 0.5, alpha)
    dy = dy.reshape(rows, st.d)
    ds = _mm(dy, w2, 'nt')
    dw2 = _mm(a, dy, 'tn', a_gate=b)
    da, db = _swiglu_bwd(ds, a, b)
    dw1 = _mm(h, da, 'tn')
    dw3 = _mm(h, db, 'tn')
    dh = _mm(db, w3, 'nt', init=_mm(da, w1, 'nt')).reshape(st.bn, st.n, st.d)
    dx, dsh, dsc = _modulate_bwd(st, dh, x, dxr, mod, k)
    return dx, (dsh, dsc, dgate), dw1, dw3, dw2, jnp.sum(dlg, axis=(0, 1)), jnp.sum(dlb, axis=(0, 1))


def _dwconv_centred(x, w):
    pad = CONV_K // 2
    return lax.conv_general_dilated(
        x, w[:, None, :].astype(x.dtype), window_strides=(1,), padding=[(pad, pad)],
        dimension_numbers=('NWC', 'WIO', 'NWC'), feature_group_count=x.shape[-1])


def _largest_divisor(n, cap):
    return max(t for t in range(1, cap + 1) if n % t == 0)


def _bmm(a, b):
    return lax.dot_general(a, b, (((2,), (1,)), ((0,), (0,))), precision=BF16X3, preferred_element_type=F32)


_BATCHED_DIMS = {'nn': (((2,), (1,)), ((0,), (0,))), 'nt': (((2,), (2,)), ((0,), (0,))),
                 'tn': (((1,), (1,)), ((0,), (0,)))}


def _bdot(a, b, mode):
    return lax.dot_general(a.astype(BF16), b.astype(BF16), _BATCHED_DIMS[mode], preferred_element_type=F32)


@functools.partial(jax.custom_vjp, nondiff_argnums=(2,))
def _bmm_lo(a, b, mode):
    return _bdot(a, b, mode)


def _bmm_lo_fwd(a, b, mode):
    return _bdot(a, b, mode), (a, b)


def _bmm_lo_bwd(mode, res, ct):
    a, b = res
    if mode == 'nn':
        return _bdot(ct, b, 'nt'), _bdot(a, ct, 'tn')
    if mode == 'nt':
        return _bdot(ct, b, 'nn'), _bdot(ct, a, 'tn')
    return _bdot(b, ct, 'nt'), _bdot(a, ct, 'nn')


_bmm_lo.defvjp(_bmm_lo_fwd, _bmm_lo_bwd)


def _lower_inverse_product(lower):
    n = lower.shape[0]
    eye = jnp.where(lax.broadcasted_iota(jnp.int32, (n, CHUNK, CHUNK), 1)
                    == lax.broadcasted_iota(jnp.int32, (n, CHUNK, CHUNK), 2), 1.0, 0.0)
    inv, power = eye - lower, lower
    for _ in range(5):
        power = _bmm(power, power)
        inv = _bmm(inv, eye + power)
    return inv


@jax.custom_vjp
def _unit_lower_inverse(lower):
    return _lower_inverse_product(lower)


def _unit_lower_inverse_fwd(lower):
    inv = _lower_inverse_product(lower)
    return inv, inv


def _unit_lower_inverse_bwd(inv, g):
    right = lax.dot_general(g, inv, _BATCHED_DIMS['nt'], precision=BF16X3, preferred_element_type=F32)
    return (-lax.dot_general(inv, right, _BATCHED_DIMS['tn'], precision=BF16X3, preferred_element_type=F32),)


_unit_lower_inverse.defvjp(_unit_lower_inverse_fwd, _unit_lower_inverse_bwd)


def _gdn_prep(q, k, v, gc, gr, beta):
    n = q.shape[0]
    row = lax.broadcasted_iota(jnp.int32, (n, CHUNK, CHUNK), 1)
    col = lax.broadcasted_iota(jnp.int32, (n, CHUNK, CHUNK), 2)
    incl, strict = row >= col, row > col
    decay = jnp.where(incl, jnp.exp(jnp.where(incl, gc - gr, 0.0)), 0.0)
    kb = k * beta
    lower = jnp.where(strict, _bmm_lo(kb, k, 'nt') * decay, 0.0)
    inv = _unit_lower_inverse(lower)
    u = _bmm(inv, v * beta)
    w = _bmm(inv, kb * jnp.exp(gc))
    intra = jnp.where(incl, _bmm_lo(q, k, 'nt') * decay, 0.0)
    is_last = lax.broadcasted_iota(jnp.int32, (n, CHUNK, 1), 1) == CHUNK - 1
    g_last = jnp.sum(jnp.where(is_last, gc, 0.0), axis=1, keepdims=True)
    return u, w, intra, q * jnp.exp(gc), k * jnp.exp(g_last - gc), jnp.exp(g_last) * jnp.ones((n, 1, GDN_DV), F32)


def _gdn_step(state, u, w, intra, qg, kd, gl):
    v_new = u - _bmm_lo(w, state, 'nn')
    o = _bmm_lo(qg, state, 'nn') + _bmm_lo(intra, v_new, 'nn')
    return o, state * gl + _bmm_lo(kd, v_new, 'tn')


def _gdn_prep_specs(cb):
    idx = lambda c, i: (c, i, 0, 0)
    mat = pl.BlockSpec((None, cb, CHUNK, GDN_DK), idx)
    colv = pl.BlockSpec((None, cb, CHUNK, 1), idx)
    rowv = pl.BlockSpec((None, cb, 1, CHUNK), idx)
    return mat, colv, rowv


def _gdn_prep_fwd_call(q, k, v, gc, gr, beta):
    chains, nchunks = q.shape[:2]
    cb = _largest_divisor(nchunks, GDN_PREP_CHUNKS)
    mat, colv, rowv = _gdn_prep_specs(cb)

    def body(q_ref, k_ref, v_ref, gc_ref, gr_ref, b_ref, *out_refs):
        outs = _gdn_prep(q_ref[...], k_ref[...], v_ref[...], gc_ref[...], gr_ref[...], b_ref[...])
        for ref, val in zip(out_refs, outs):
            ref[...] = val

    mshape = jax.ShapeDtypeStruct(q.shape, F32)
    return pl.pallas_call(
        body, name="gdn_prep_fwd", grid=(chains, nchunks // cb),
        in_specs=[mat, mat, mat, colv, rowv, colv], out_specs=[mat] * 5 + [rowv],
        out_shape=[mshape] * 5 + [jax.ShapeDtypeStruct(gr.shape, F32)],
        compiler_params=pltpu.CompilerParams(dimension_semantics=("parallel", "parallel"), vmem_limit_bytes=VMEM_LIMIT),
    )(q, k, v, gc, gr, beta)


def _gdn_prep_bwd_call(q, k, v, gc, gr, beta, cts):
    chains, nchunks = q.shape[:2]
    cb = _largest_divisor(nchunks, GDN_PREP_CHUNKS)
    mat, colv, rowv = _gdn_prep_specs(cb)

    def body(q_ref, k_ref, v_ref, gc_ref, gr_ref, b_ref, du, dw, di, dqg, dkd, dgl, *out_refs):
        _, vjp = jax.vjp(_gdn_prep, q_ref[...], k_ref[...], v_ref[...], gc_ref[...], gr_ref[...], b_ref[...])
        grads = vjp((du[...], dw[...], di[...], dqg[...], dkd[...], dgl[...]))
        for ref, val in zip(out_refs, grads):
            ref[...] = val

    return pl.pallas_call(
        body, name="gdn_prep_bwd", grid=(chains, nchunks // cb),
        in_specs=[mat, mat, mat, colv, rowv, colv] + [mat] * 5 + [rowv],
        out_specs=[mat, mat, mat, colv, rowv, colv],
        out_shape=[jax.ShapeDtypeStruct(t.shape, F32) for t in (q, k, v, gc, gr, beta)],
        compiler_params=pltpu.CompilerParams(dimension_semantics=("parallel", "parallel"), vmem_limit_bytes=VMEM_LIMIT),
    )(q, k, v, gc, gr, beta, *cts)


def _gdn_scan_fwd_call(u, w, intra, qg, kd, gl):
    chains, nchunks = u.shape[:2]
    cc = _largest_divisor(chains, GDN_SCAN_CHAINS)
    idx = lambda c, i: (c, i, 0, 0)
    mat = pl.BlockSpec((cc, None, CHUNK, GDN_DK), idx)
    rowv = pl.BlockSpec((cc, None, 1, CHUNK), idx)

    def body(u_ref, w_ref, i_ref, qg_ref, kd_ref, gl_ref, o_ref, hist_ref, state_ref):
        @pl.when(pl.program_id(1) == 0)
        def _():
            state_ref[...] = jnp.zeros_like(state_ref)

        state = state_ref[...]
        hist_ref[...] = state
        o, new = _gdn_step(state, u_ref[...], w_ref[...], i_ref[...], qg_ref[...], kd_ref[...], gl_ref[...])
        o_ref[...] = o
        state_ref[...] = new

    mshape = jax.ShapeDtypeStruct(u.shape, F32)
    return pl.pallas_call(
        body, name="gdn_scan_fwd", grid=(chains // cc, nchunks),
        in_specs=[mat] * 5 + [rowv], out_specs=[mat, mat], out_shape=[mshape, mshape],
        scratch_shapes=[pltpu.VMEM((cc, GDN_DK, GDN_DV), F32)],
        compiler_params=pltpu.CompilerParams(dimension_semantics=("parallel", "arbitrary"), vmem_limit_bytes=VMEM_LIMIT),
    )(u, w, intra, qg, kd, gl)


def _gdn_scan_bwd_call(u, w, intra, qg, kd, gl, hist, do):
    chains, nchunks = u.shape[:2]
    cc = _largest_divisor(chains, GDN_SCAN_CHAINS)
    idx = lambda c, i: (c, nchunks - 1 - i, 0, 0)
    mat = pl.BlockSpec((cc, None, CHUNK, GDN_DK), idx)
    rowv = pl.BlockSpec((cc, None, 1, CHUNK), idx)

    def body(u_ref, w_ref, i_ref, qg_ref, kd_ref, gl_ref, h_ref, do_ref, du, dw, di, dqg, dkd, dgl, dstate_ref):
        @pl.when(pl.program_id(1) == 0)
        def _():
            dstate_ref[...] = jnp.zeros_like(dstate_ref)

        _, vjp = jax.vjp(_gdn_step, h_ref[...], u_ref[...], w_ref[...], i_ref[...], qg_ref[...], kd_ref[...], gl_ref[...])
        grads = vjp((do_ref[...], dstate_ref[...]))
        dstate_ref[...] = grads[0]
        for ref, val in zip((du, dw, di, dqg, dkd, dgl), grads[1:]):
            ref[...] = val

    mshape = jax.ShapeDtypeStruct(u.shape, F32)
    return pl.pallas_call(
        body, name="gdn_scan_bwd", grid=(chains // cc, nchunks),
        in_specs=[mat] * 5 + [rowv, mat, mat], out_specs=[mat] * 5 + [rowv],
        out_shape=[mshape] * 5 + [jax.ShapeDtypeStruct(gl.shape, F32)],
        scratch_shapes=[pltpu.VMEM((cc, GDN_DK, GDN_DV), F32)],
        compiler_params=pltpu.CompilerParams(dimension_semantics=("parallel", "arbitrary"), vmem_limit_bytes=VMEM_LIMIT),
    )(u, w, intra, qg, kd, gl, hist, do)


@jax.custom_vjp
def _gdn_core(q, k, v, gc, gr, beta):
    return _gdn_scan_fwd_call(*_gdn_prep_fwd_call(q, k, v, gc, gr, beta))[0]


def _gdn_core_fwd(q, k, v, gc, gr, beta):
    prep = _gdn_prep_fwd_call(q, k, v, gc, gr, beta)
    o, hist = _gdn_scan_fwd_call(*prep)
    return o, (q, k, v, gc, gr, beta, prep, hist)


def _gdn_core_bwd(res, do):
    q, k, v, gc, gr, beta, prep, hist = res
    cts = _gdn_scan_bwd_call(*prep, hist, do)
    return tuple(_gdn_prep_bwd_call(q, k, v, gc, gr, beta, cts))


_gdn_core.defvjp(_gdn_core_fwd, _gdn_core_bwd)


def _gdn_inputs(qkv, b, a, conv_w, a_log, dt_bias):
    Bn, T, _ = qkv.shape
    qkv = jax.nn.silu(_dwconv_centred(qkv, conv_w))
    q, k, v = jnp.split(qkv, [GDN_HEADS * GDN_DK, 2 * GDN_HEADS * GDN_DK], axis=-1)

    def to_heads(t, d):
        return t.reshape(Bn, T, GDN_HEADS, d).transpose(0, 2, 1, 3)

    def dir_heads(t):
        return t.reshape(Bn, T, 2, GDN_HEADS).transpose(2, 0, 3, 1)

    q = _l2_normalize(to_heads(q, GDN_DK))
    k = _l2_normalize(to_heads(k, GDN_DK))
    v = to_heads(v, GDN_DV)
    beta = jax.nn.sigmoid(dir_heads(b))
    g = -jnp.exp(a_log)[:, None, :, None] * jax.nn.softplus(dir_heads(a) + dt_bias[:, None, :, None])
    return q, k, v, g, beta


def _gdn_gated_out(o, z, norm_w):
    Bn, H, T, dv = o.shape
    o = _rms_norm(o.transpose(0, 2, 1, 3), norm_w)
    o = o * jax.nn.silu(z.reshape(Bn, T, H, dv))
    return o.reshape(Bn, T, H * dv)


def _gdn_group(qkv, z, b, a, qkv_c, z_c, b_c, a_c, conv_w, a_log, dt_bias, norm_w, with_ctx_out):
    q, k, v, g, beta = _gdn_inputs(qkv, b, a, conv_w, a_log, dt_bias)
    qc, kc, vc, gc, betac = _gdn_inputs(qkv_c, b_c, a_c, conv_w, a_log, dt_bias)
    bn, heads, t_lat, _ = q.shape
    t_ctx = qc.shape[2]
    n = t_ctx + t_lat
    nchunks = n // CHUNK
    chains = 2 * bn * heads

    def both(tc, tl):
        return jnp.stack([jnp.concatenate([tc, tl], axis=2), jnp.concatenate([jnp.flip(tc, 2), jnp.flip(tl, 2)], axis=2)])

    def per_dir(tc, tl):
        return jnp.stack([jnp.concatenate([tc[0], tl[0]], axis=2),
                          jnp.concatenate([jnp.flip(tc[1], 2), jnp.flip(tl[1], 2)], axis=2)])

    def mats(t):
        return t.reshape(chains, nchunks, CHUNK, t.shape[-1])

    gcum = jnp.cumsum(per_dir(gc, g).reshape(chains, nchunks, CHUNK), axis=-1)
    o = _gdn_core(mats(both(qc, q) * GDN_DK ** -0.5), mats(both(kc, k)), mats(both(vc, v)),
                  gcum[..., None], gcum[:, :, None, :], per_dir(betac, beta).reshape(chains, nchunks, CHUNK, 1))
    o = o.reshape(2, bn, heads, n, GDN_DV)
    out = _gdn_gated_out(o[0, :, :, t_ctx:] + jnp.flip(o[1, :, :, t_ctx:], 2), z, norm_w)
    if not with_ctx_out:
        return out, None
    return out, _gdn_gated_out(o[0, :, :, :t_ctx] + jnp.flip(o[1, :, :, :t_ctx], 2), z_c, norm_w)


def _axial_rope_tables(rows):
    row = jnp.repeat(jnp.arange(rows), GRID_W)
    col = jnp.tile(jnp.arange(GRID_W), rows)
    inv_freq = ROPE_THETA ** (-jnp.arange(ROPE_PAIRS, dtype=F32) / ROPE_PAIRS)
    ang = jnp.stack([row, col], axis=-1).astype(F32)[..., None] * inv_freq
    return jnp.cos(ang), jnp.sin(ang)


def _rope_2d(x, cos, sin):
    shp = x.shape
    xr = x.reshape(*shp[:-1], 2, 2, ROPE_PAIRS)
    x1, x2 = xr[..., 0, :], xr[..., 1, :]
    bshape = (shp[1],) + (1,) * (x.ndim - 3) + (2, ROPE_PAIRS)
    c, s = cos.reshape(bshape), sin.reshape(bshape)
    out = jnp.stack([x1 * c - x2 * s, x2 * c + x1 * s], axis=-2)
    return out.reshape(shp)


def _attn_specs(tq, tk):
    q_spec = pl.BlockSpec((None, None, ATT_GROUP, tq, ATT_DH), lambda b, h, i: (b, h, 0, i, 0))
    kv_spec = pl.BlockSpec((None, None, tk, ATT_DH), lambda b, h, i: (b, h, 0, 0))
    return q_spec, kv_spec


def _softmax_rows(q, k):
    s = lax.dot_general(q.astype(BF16), k.astype(BF16), (((1,), (1,)), ((), ())), preferred_element_type=F32)
    s = s * (ATT_DH ** -0.5)
    e = jnp.exp(s - jnp.max(s, axis=-1, keepdims=True))
    return e / jnp.sum(e, axis=-1, keepdims=True)


def _attn_fwd_call(q, k, v):
    bn, _, _, t_q, _ = q.shape
    t_k = k.shape[2]
    tq = _pick(t_q, (Q_BLOCK, 64, 32, 16, 8))
    q_spec, kv_spec = _attn_specs(tq, t_k)

    def body(q_ref, k_ref, v_ref, o_ref):
        p = _softmax_rows(q_ref[...].reshape(ATT_GROUP * tq, ATT_DH), k_ref[...])
        o = lax.dot_general(p.astype(BF16), v_ref[...].astype(BF16), (((1,), (0,)), ((), ())), preferred_element_type=F32)
        o_ref[...] = o.reshape(ATT_GROUP, tq, ATT_DH)

    return pl.pallas_call(
        body, name=f"attention_fwd_{t_q}x{t_k}", grid=(bn, ATT_KV_HEADS, t_q // tq),
        in_specs=[q_spec, kv_spec, kv_spec], out_specs=q_spec,
        out_shape=jax.ShapeDtypeStruct(q.shape, F32),
        compiler_params=pltpu.CompilerParams(dimension_semantics=("parallel", "parallel", "parallel"),
                                             vmem_limit_bytes=VMEM_LIMIT),
    )(q, k, v)


def _attn_bwd_call(q, k, v, do):
    bn, _, _, t_q, _ = q.shape
    t_k = k.shape[2]
    tq = _pick(t_q, (Q_BLOCK, 64, 32, 16, 8))
    q_spec, kv_spec = _attn_specs(tq, t_k)

    def body(q_ref, k_ref, v_ref, do_ref, dq_ref, dk_ref, dv_ref):
        @pl.when(pl.program_id(2) == 0)
        def _():
            dk_ref[...] = jnp.zeros_like(dk_ref)
            dv_ref[...] = jnp.zeros_like(dv_ref)

        qv = q_ref[...].reshape(ATT_GROUP * tq, ATT_DH)
        dov = do_ref[...].reshape(ATT_GROUP * tq, ATT_DH).astype(BF16)
        kb, vb = k_ref[...].astype(BF16), v_ref[...].astype(BF16)
        p = _softmax_rows(qv, k_ref[...])
        dp = lax.dot_general(dov, vb, (((1,), (1,)), ((), ())), preferred_element_type=F32)
        ds = p * (dp - jnp.sum(p * dp, axis=-1, keepdims=True)) * (ATT_DH ** -0.5)
        dsb = ds.astype(BF16)
        dq = lax.dot_general(dsb, kb, (((1,), (0,)), ((), ())), preferred_element_type=F32)
        dq_ref[...] = dq.reshape(ATT_GROUP, tq, ATT_DH)
        dk_ref[...] += lax.dot_general(dsb, qv.astype(BF16), (((0,), (0,)), ((), ())), preferred_element_type=F32)
        dv_ref[...] += lax.dot_general(p.astype(BF16), dov, (((0,), (0,)), ((), ())), preferred_element_type=F32)

    return pl.pallas_call(
        body, name=f"attention_bwd_{t_q}x{t_k}", grid=(bn, ATT_KV_HEADS, t_q // tq),
        in_specs=[q_spec, kv_spec, kv_spec, q_spec], out_specs=[q_spec, kv_spec, kv_spec],
        out_shape=[jax.ShapeDtypeStruct(t.shape, F32) for t in (q, k, v)],
        compiler_params=pltpu.CompilerParams(dimension_semantics=("parallel", "parallel", "arbitrary"),
                                             vmem_limit_bytes=VMEM_LIMIT),
    )(q, k, v, do)


@jax.custom_vjp
def _attn_core(q, k, v):
    return _attn_fwd_call(q, k, v)


def _attn_core_fwd(q, k, v):
    return _attn_fwd_call(q, k, v), (q, k, v)


def _attn_core_bwd(res, do):
    return tuple(_attn_bwd_call(*res, do))


_attn_core.defvjp(_attn_core_fwd, _attn_core_bwd)


def _attention_group(q, k, v, q_c, k_c, v_c, q_norm_w, k_norm_w, cos, sin, with_ctx_out):
    Bn, T, _ = q.shape
    Tc = q_c.shape[1]
    q = _rope_2d(_rms_norm(q.reshape(Bn, T, ATT_KV_HEADS, ATT_GROUP, ATT_DH), q_norm_w), cos, sin)
    k = _rope_2d(_rms_norm(k.reshape(Bn, T, ATT_KV_HEADS, ATT_DH), k_norm_w), cos, sin)
    v = v.reshape(Bn, T, ATT_KV_HEADS, ATT_DH)
    qc = _rms_norm(q_c.reshape(Bn, Tc, ATT_KV_HEADS, ATT_GROUP, ATT_DH), q_norm_w)
    kc = _rms_norm(k_c.reshape(Bn, Tc, ATT_KV_HEADS, ATT_DH), k_norm_w)
    vc = v_c.reshape(Bn, Tc, ATT_KV_HEADS, ATT_DH)
    keys = jnp.concatenate([kc, k], axis=1).transpose(0, 2, 1, 3)
    vals = jnp.concatenate([vc, v], axis=1).transpose(0, 2, 1, 3)
    o = _attn_core(q.transpose(0, 2, 3, 1, 4), keys, vals)
    o = o.transpose(0, 3, 1, 2, 4).reshape(Bn, T, ATT_W)
    if not with_ctx_out:
        return o, None
    o_c = _attn_core(qc.transpose(0, 2, 3, 1, 4), kc.transpose(0, 2, 1, 3), vc.transpose(0, 2, 1, 3))
    return o, o_c.transpose(0, 3, 1, 2, 4).reshape(Bn, Tc, ATT_W)


def _s5_operators(lam_re, lam_im, log_dt, b_re, b_im, c_re, c_im):
    lc = S5_LC
    dt = jnp.exp(log_dt)[..., None]
    ar, ai = lam_re * dt, lam_im * dt
    mag = jnp.exp(ar)
    lbr, lbi = mag * jnp.cos(ai), mag * jnp.sin(ai)
    den = lam_re * lam_re + lam_im * lam_im
    fr = ((lbr - 1.0) * lam_re + lbi * lam_im) / den
    fi = (lbi * lam_re - (lbr - 1.0) * lam_im) / den
    bbr = fr[..., None] * b_re - fi[..., None] * b_im
    bbi = fr[..., None] * b_im + fi[..., None] * b_re
    m = jnp.arange(lc + 1, dtype=F32)[:, None, None, None]
    pmag = jnp.exp(m * ar)
    pr, pi = pmag * jnp.cos(m * ai), pmag * jnp.sin(m * ai)
    cpr = c_re[None] * pr[:, :, :, None, :] - c_im[None] * pi[:, :, :, None, :]
    cpi = c_re[None] * pi[:, :, :, None, :] + c_im[None] * pr[:, :, :, None, :]
    kern = (jnp.einsum('mdghp,dgpk->mdghk', cpr[:lc], bbr, precision=HI)
            - jnp.einsum('mdghp,dgpk->mdghk', cpi[:lc], bbi, precision=HI))
    tail = kern.shape[1:]
    lags = jnp.concatenate([jnp.zeros((lc - 1,) + tail, F32), kern, jnp.zeros((1,) + tail, F32)], axis=0)
    toep = jnp.tile(lags, (lc,) + (1,) * len(tail))[:lc * (2 * lc - 1)].reshape((lc, 2 * lc - 1) + tail)[:, lc - 1:]
    tm = toep.transpose(2, 3, 0, 5, 1, 4).reshape(2, S5_GROUPS, S5_CW, S5_CW)
    prr, pir = pr[lc - 1::-1], pi[lc - 1::-1]
    mre = prr[..., None] * bbr[None] - pir[..., None] * bbi[None]
    mim = prr[..., None] * bbi[None] + pir[..., None] * bbr[None]
    mm = jnp.concatenate([mre, mim], axis=3).transpose(1, 2, 0, 4, 3).reshape(2, S5_GROUPS, S5_CW, S5_SW)
    nm = jnp.concatenate([cpr[1:], -cpi[1:]], axis=-1)
    nm = nm.transpose(1, 2, 4, 0, 3).reshape(2, S5_GROUPS, S5_SW, S5_CW)
    a1 = jnp.concatenate([pr[lc], pr[lc]], axis=-1)
    a2 = jnp.concatenate([-pi[lc], pi[lc]], axis=-1)
    lam_rows = jnp.concatenate([a1[:, :, None], a2[:, :, None],
                                jnp.zeros((2, S5_GROUPS, SUBLANES - 2, S5_SW), F32)], axis=2)
    return tm, mm, nm, lam_rows


def _dot_hi(a, b, dims):
    return lax.dot_general(a, b, (dims, ((), ())), precision=BF16X3, preferred_element_type=F32)


def _s5_blocks(rows):
    def blk(r, c):
        return pl.BlockSpec((None, None, r, c), lambda d, g: (d, g, 0, 0))
    return (blk(rows, S5_CW), blk(S5_CW, S5_CW), blk(S5_CW, S5_SW), blk(S5_SW, S5_CW), blk(SUBLANES, S5_SW),
            blk(rows, S5_SW))


def _s5_core_fwd_call(s, tm, mm, nm, lam_rows):
    rows = s.shape[2]
    chunks = rows // S5_ROWS
    seq, top, mop, nop, lop, sta = _s5_blocks(rows)

    def body(s_ref, t_ref, m_ref, n_ref, l_ref, y_ref, h_ref, e_ref):
        sv = s_ref[...]
        e_ref[...] = _dot_hi(sv, m_ref[...], ((1,), (0,)))
        a1, a2 = l_ref[0:1, :], l_ref[1:2, :]
        h_ref[0:S5_ROWS, :] = jnp.zeros((S5_ROWS, S5_SW), F32)

        def step(k, carry):
            at = pl.multiple_of((k - 1) * S5_ROWS, S5_ROWS)
            prev = h_ref[pl.ds(at, S5_ROWS), :]
            new = a1 * prev + a2 * pltpu.roll(prev, S5_P, 1) + e_ref[pl.ds(at, S5_ROWS), :]
            h_ref[pl.ds(pl.multiple_of(k * S5_ROWS, S5_ROWS), S5_ROWS), :] = new
            return carry

        lax.fori_loop(1, chunks, step, 0)
        y_ref[...] = _dot_hi(sv, t_ref[...], ((1,), (0,))) + _dot_hi(h_ref[...], n_ref[...], ((1,), (0,)))

    return pl.pallas_call(
        body, name="s5_chunks_fwd", grid=(2, S5_GROUPS),
        in_specs=[seq, top, mop, nop, lop], out_specs=[seq, sta],
        out_shape=[jax.ShapeDtypeStruct(s.shape, F32), jax.ShapeDtypeStruct(s.shape[:3] + (S5_SW,), F32)],
        scratch_shapes=[pltpu.VMEM((rows, S5_SW), F32)],
        compiler_params=pltpu.CompilerParams(dimension_semantics=("parallel", "parallel"), vmem_limit_bytes=VMEM_LIMIT),
    )(s, tm, mm, nm, lam_rows)


def _s5_core_bwd_call(s, dy, tm, mm, nm, lam_rows, hin):
    rows = s.shape[2]
    chunks = rows // S5_ROWS
    seq, top, mop, nop, lop, sta = _s5_blocks(rows)

    def body(s_ref, dy_ref, t_ref, m_ref, n_ref, l_ref, h_ref, ds_ref, dt_ref, dm_ref, dn_ref, dl_ref, dh_ref, de_ref):
        sv, dyv, hv = s_ref[...], dy_ref[...], h_ref[...]
        dh_ref[...] = _dot_hi(dyv, n_ref[...], ((1,), (1,)))
        a1, a2 = l_ref[0:1, :], l_ref[1:2, :]
        last = (chunks - 1) * S5_ROWS
        de_ref[last:last + S5_ROWS, :] = jnp.zeros((S5_ROWS, S5_SW), F32)

        def step(i, g):
            k = chunks - 2 - i
            at = pl.multiple_of(k * S5_ROWS, S5_ROWS)
            de_ref[pl.ds(at, S5_ROWS), :] = g
            return dh_ref[pl.ds(at, S5_ROWS), :] + a1 * g + pltpu.roll(a2 * g, S5_P, 1)

        lax.fori_loop(0, chunks - 1, step, dh_ref[last:last + S5_ROWS, :])
        dev = de_ref[...]
        ds_ref[...] = _dot_hi(dyv, t_ref[...], ((1,), (1,))) + _dot_hi(dev, m_ref[...], ((1,), (1,)))
        dt_ref[...] = _dot_hi(sv, dyv, ((0,), (0,)))
        dm_ref[...] = _dot_hi(sv, dev, ((0,), (0,)))
        dn_ref[...] = _dot_hi(hv, dyv, ((0,), (0,)))
        da1 = jnp.sum(hv * dev, axis=0, keepdims=True)
        da2 = jnp.sum(pltpu.roll(hv, S5_P, 1) * dev, axis=0, keepdims=True)
        dl_ref[...] = jnp.concatenate([da1, da2, jnp.zeros((SUBLANES - 2, S5_SW), F32)], axis=0)

    return pl.pallas_call(
        body, name="s5_chunks_bwd", grid=(2, S5_GROUPS),
        in_specs=[seq, seq, top, mop, nop, lop, sta], out_specs=[seq, top, mop, nop, lop],
        out_shape=[jax.ShapeDtypeStruct(t.shape, F32) for t in (s, tm, mm, nm, lam_rows)],
        scratch_shapes=[pltpu.VMEM((rows, S5_SW), F32), pltpu.VMEM((rows, S5_SW), F32)],
        compiler_params=pltpu.CompilerParams(dimension_semantics=("parallel", "parallel"), vmem_limit_bytes=VMEM_LIMIT),
    )(s, dy, tm, mm, nm, lam_rows, hin)


@jax.custom_vjp
def _s5_core(s, tm, mm, nm, lam_rows):
    return _s5_core_fwd_call(s, tm, mm, nm, lam_rows)[0]


def _s5_core_fwd(s, tm, mm, nm, lam_rows):
    y, hin = _s5_core_fwd_call(s, tm, mm, nm, lam_rows)
    return y, (s, tm, mm, nm, lam_rows, hin)


def _s5_core_bwd(res, dy):
    s, tm, mm, nm, lam_rows, hin = res
    return tuple(_s5_core_bwd_call(s, dy, tm, mm, nm, lam_rows, hin))


_s5_core.defvjp(_s5_core_fwd, _s5_core_bwd)


def _s5_group(u, u_c, lam_re, lam_im, log_dt, b_re, b_im, c_re, c_im, d_skip, glu_w, glu_b, with_ctx_out):
    bn, t_lat, _ = u.shape
    t_ctx = u_c.shape[1]
    n = t_ctx + t_lat
    chunks = n // S5_LC
    assert bn <= S5_ROWS and t_ctx % S5_LC == 0 and t_lat % S5_LC == 0
    seqs = jnp.stack([jnp.concatenate([u_c, u], axis=1),
                      jnp.concatenate([jnp.flip(u_c, 1), jnp.flip(u, 1)], axis=1)])
    s = seqs.reshape(2, bn, chunks, S5_LC, S5_GROUPS, S5_GH).transpose(0, 4, 2, 1, 3, 5)
    s = jnp.pad(s, ((0, 0), (0, 0), (0, 0), (0, S5_ROWS - bn), (0, 0), (0, 0)))
    s = s.reshape(2, S5_GROUPS, chunks * S5_ROWS, S5_CW)
    y = _s5_core(s, *_s5_operators(lam_re, lam_im, log_dt, b_re, b_im, c_re, c_im))
    y = y.reshape(2, S5_GROUPS, chunks, S5_ROWS, S5_LC, S5_GH)[:, :, :, :bn]
    y = y.transpose(0, 3, 2, 4, 1, 5).reshape(2, bn, n, S5_W)
    yl = d_skip * u + y[0, :, t_ctx:] + jnp.flip(y[1, :, t_ctx:], 1)

    def glu(yy):
        zz = jax.nn.gelu(yy)
        return zz * jax.nn.sigmoid(zz @ glu_w + glu_b)

    if not with_ctx_out:
        return glu(yl), None
    yc = d_skip * u_c + y[0, :, :t_ctx] + jnp.flip(y[1, :, :t_ctx], 1)
    return glu(yl), glu(yc)


def _s5_wide_operators(lam_re, lam_im, log_dt, b_re, b_im, c_re, c_im):
    lc, g4, nd = S5W_LC, S5W_TILE // S5_SW, lam_re.shape[0]
    dt = jnp.exp(log_dt)[..., None]
    ar, ai = lam_re * dt, lam_im * dt
    mag = jnp.exp(ar)
    lbr, lbi = mag * jnp.cos(ai), mag * jnp.sin(ai)
    den = lam_re * lam_re + lam_im * lam_im
    fr = ((lbr - 1.0) * lam_re + lbi * lam_im) / den
    fi = (lbi * lam_re - (lbr - 1.0) * lam_im) / den
    bbr = fr[..., None] * b_re - fi[..., None] * b_im
    bbi = fr[..., None] * b_im + fi[..., None] * b_re
    m = jnp.arange(lc + 1, dtype=F32)[:, None, None, None]
    pmag = jnp.exp(m * ar)
    pr, pi = pmag * jnp.cos(m * ai), pmag * jnp.sin(m * ai)
    cpr = c_re[None] * pr[:, :, :, None, :] - c_im[None] * pi[:, :, :, None, :]
    cpi = c_re[None] * pi[:, :, :, None, :] + c_im[None] * pr[:, :, :, None, :]
    kern = (jnp.einsum('mdghp,dgpk->mdghk', cpr[:lc], bbr, precision=HI)
            - jnp.einsum('mdghp,dgpk->mdghk', cpi[:lc], bbi, precision=HI))
    lags = jnp.concatenate([jnp.zeros((lc - 1,) + kern.shape[1:], F32), kern], axis=0)
    toep = jnp.stack([lags[lc - 1 - j:2 * lc - 1 - j] for j in range(lc)])
    eye = jnp.eye(S5_GROUPS, dtype=F32)
    tm = jnp.einsum('jtdghk,gn->djgktnh', toep, eye).reshape(nd, S5W_RW, S5W_RW)
    back = (lc - 1.0 - jnp.arange(lc, dtype=F32))[:, None, None, None]
    bmag = jnp.exp(back * ar)
    prr, pir = bmag * jnp.cos(back * ai), bmag * jnp.sin(back * ai)
    left = jnp.stack([prr[..., None] * bbr[None] - pir[..., None] * bbi[None],
                      prr[..., None] * bbi[None] + pir[..., None] * bbr[None]])
    mm = jnp.einsum('ajdgpk,gn->djgknap', left, eye)
    mm = mm.reshape(nd, lc, S5_GROUPS, S5_GH, S5_GROUPS // g4, g4, 2, S5_P).transpose(0, 1, 2, 3, 4, 6, 5, 7)
    mm = mm.reshape(nd, S5W_RW, S5W_SW)
    right = jnp.stack([cpr[1:], -cpi[1:]])
    nm = jnp.einsum('atdghp,gn->dnaptgh', right, eye)
    nm = nm.reshape(nd, S5_GROUPS // g4, g4, 2, S5_P, lc, S5_GROUPS, S5_GH).transpose(0, 1, 3, 2, 4, 5, 6, 7)
    nm = nm.reshape(nd, S5W_SW, S5W_RW)

    def state_cols(re_part, im_part):
        t = jnp.stack([re_part, im_part], axis=1).reshape(nd, 2, S5_GROUPS // g4, g4, S5_P)
        return t.transpose(0, 2, 1, 3, 4).reshape(nd, S5W_SW)

    a1, a2 = state_cols(pr[lc], pr[lc]), state_cols(-pi[lc], pi[lc])
    lam_rows = jnp.concatenate([a1[:, None], a2[:, None], jnp.zeros((nd, SUBLANES - 2, S5W_SW), F32)], axis=1)
    return tm, mm, nm, lam_rows


def _s5_carry_specs(rows):
    wide = pl.BlockSpec((rows, S5W_TILE), lambda j: (0, j))
    lam = pl.BlockSpec((SUBLANES, S5W_TILE), lambda j: (0, j))
    return wide, lam


def _s5_carry_fwd(e, lam_rows):
    rows = e.shape[0]
    steps = rows // S5_ROWS
    wide, lam = _s5_carry_specs(rows)

    def body(e_ref, l_ref, h_ref):
        a1, a2 = l_ref[0:1, :], l_ref[1:2, :]
        h_ref[0:S5_ROWS, :] = jnp.zeros((S5_ROWS, S5W_TILE), F32)

        def step(k, carry):
            at = pl.multiple_of((k - 1) * S5_ROWS, S5_ROWS)
            prev = h_ref[pl.ds(at, S5_ROWS), :]
            new = a1 * prev + a2 * pltpu.roll(prev, S5W_TILE // 2, 1) + e_ref[pl.ds(at, S5_ROWS), :]
            h_ref[pl.ds(pl.multiple_of(k * S5_ROWS, S5_ROWS), S5_ROWS), :] = new
            return carry

        lax.fori_loop(1, steps, step, 0)

    return pl.pallas_call(
        body, name="s5_carry_fwd", grid=(S5W_SW // S5W_TILE,), in_specs=[wide, lam], out_specs=wide,
        out_shape=jax.ShapeDtypeStruct(e.shape, F32),
        compiler_params=pltpu.CompilerParams(dimension_semantics=("parallel",), vmem_limit_bytes=VMEM_LIMIT),
    )(e, lam_rows)


def _s5_carry_bwd(dh, hin, lam_rows):
    rows = dh.shape[0]
    steps = rows // S5_ROWS
    wide, lam = _s5_carry_specs(rows)

    def body(dh_ref, h_ref, l_ref, de_ref, dl_ref):
        a1, a2 = l_ref[0:1, :], l_ref[1:2, :]
        last = (steps - 1) * S5_ROWS
        de_ref[last:last + S5_ROWS, :] = jnp.zeros((S5_ROWS, S5W_TILE), F32)

        def step(i, g):
            at = pl.multiple_of((steps - 2 - i) * S5_ROWS, S5_ROWS)
            de_ref[pl.ds(at, S5_ROWS), :] = g
            return dh_ref[pl.ds(at, S5_ROWS), :] + a1 * g + pltpu.roll(a2 * g, S5W_TILE // 2, 1)

        lax.fori_loop(0, steps - 1, step, dh_ref[last:last + S5_ROWS, :])
        hv, dev = h_ref[...], de_ref[...]
        da1 = jnp.sum(hv * dev, axis=0, keepdims=True)
        da2 = jnp.sum(pltpu.roll(hv, S5W_TILE // 2, 1) * dev, axis=0, keepdims=True)
        dl_ref[...] = jnp.concatenate([da1, da2, jnp.zeros((SUBLANES - 2, S5W_TILE), F32)], axis=0)

    return pl.pallas_call(
        body, name="s5_carry_bwd", grid=(S5W_SW // S5W_TILE,), in_specs=[wide, wide, lam], out_specs=[wide, lam],
        out_shape=[jax.ShapeDtypeStruct(dh.shape, F32), jax.ShapeDtypeStruct(lam_rows.shape, F32)],
        compiler_params=pltpu.CompilerParams(dimension_semantics=("parallel",), vmem_limit_bytes=VMEM_LIMIT),
    )(dh, hin, lam_rows)


@jax.custom_vjp
def _s5_scan(s, tm, mm, nm, lam_rows):
    hin = _s5_carry_fwd(_mm(s, mm), lam_rows)
    return _mm(hin, nm, init=_mm(s, tm))


def _s5_scan_fwd(s, tm, mm, nm, lam_rows):
    hin = _s5_carry_fwd(_mm(s, mm), lam_rows)
    return _mm(hin, nm, init=_mm(s, tm)), (s, tm, mm, nm, lam_rows, hin)


def _s5_scan_bwd(res, dy):
    s, tm, mm, nm, lam_rows, hin = res
    de, dlam = _s5_carry_bwd(_mm(dy, nm, 'nt'), hin, lam_rows)
    ds = _mm(de, mm, 'nt', init=_mm(dy, tm, 'nt'))
    return ds, _mm(s, dy, 'tn'), _mm(s, de, 'tn'), _mm(hin, dy, 'tn'), dlam


_s5_scan.defvjp(_s5_scan_fwd, _s5_scan_bwd)


def _s5_wide_group(u, u_c, lam_re, lam_im, log_dt, b_re, b_im, c_re, c_im, d_skip, glu_w, glu_b):
    bn, t_lat, _ = u.shape
    t_ctx = u_c.shape[1]
    n = t_ctx + t_lat
    steps = n // S5W_LC
    assert bn <= S5_ROWS and t_ctx % S5W_LC == 0 and t_lat % S5W_LC == 0
    seqs = [jnp.concatenate([u_c, u], axis=1), jnp.concatenate([jnp.flip(u_c, 1), jnp.flip(u, 1)], axis=1)]
    ys = []
    for d, seq in enumerate(seqs):
        ops = _s5_wide_operators(*[t[d:d + 1] for t in (lam_re, lam_im, log_dt, b_re, b_im, c_re, c_im)])
        s = jnp.pad(seq.reshape(bn, steps, S5W_RW).transpose(1, 0, 2), ((0, 0), (0, S5_ROWS - bn), (0, 0)))
        y = _s5_scan(s.reshape(steps * S5_ROWS, S5W_RW), *[op[0] for op in ops])
        ys.append(y.reshape(steps, S5_ROWS, S5W_RW)[:, :bn].transpose(1, 0, 2).reshape(bn, n, S5_W))

    def glu(yy):
        zz = jax.nn.gelu(yy)
        return zz * jax.nn.sigmoid(zz @ glu_w + glu_b)

    yl = d_skip * u + ys[0][:, t_ctx:] + jnp.flip(ys[1][:, t_ctx:], 1)
    yc = d_skip * u_c + ys[0][:, :t_ctx] + jnp.flip(ys[1][:, :t_ctx], 1)
    return glu(yl), glu(yc)


def _heads(proj, proj_c, hp, cos, sin, with_ctx_out):
    g_qkv, g_z, g_b, g_a, a_q, a_k, a_v, s_u = jnp.split(proj[..., :IN_COLS], list(IN_CUTS), axis=-1)
    c_qkv, c_z, c_b, c_a, c_q, c_k, c_v, c_u = jnp.split(proj_c[..., :IN_COLS], list(IN_CUTS), axis=-1)
    o_gdn, oc_gdn = _gdn_group(g_qkv, g_z, g_b, g_a, c_qkv, c_z, c_b, c_a, hp['gdn_conv_w'], hp['gdn_a_log'],
                               hp['gdn_dt_bias'], hp['gdn_norm_w'], with_ctx_out)
    o_att, oc_att = _attention_group(a_q, a_k, a_v, c_q, c_k, c_v, hp['q_norm_w'], hp['k_norm_w'], cos, sin,
                                     with_ctx_out)
    o_s5, oc_s5 = _s5_wide_group(s_u, c_u, hp['s5_lam_re'], hp['s5_lam_im'], hp['s5_log_dt'], hp['s5_b_re'],
                                 hp['s5_b_im'], hp['s5_c_re'], hp['s5_c_im'], hp['s5_d'], hp['glu_w'], hp['glu_b'])
    o = jnp.concatenate([o_gdn, o_att, o_s5], axis=-1)
    if not with_ctx_out:
        return (o,)
    return o, jnp.concatenate([oc_gdn, oc_att, oc_s5], axis=-1)


def _mixer_fwd(st, x, mod, t_ctx, w_in, w_out, lg, lb, hp, cos, sin, alpha):
    rows = st.bn * st.n
    h = _modulate(st, x, mod, 1).reshape(rows, st.d)
    proj = _mm(h, w_in).reshape(st.bn, st.n, IN_COLS_PAD)
    (o_lat, o_ctx), heads_vjp = jax.vjp(lambda p, pc, hp_: _heads(p, pc, hp_, cos, sin, True),
                                         proj[:, t_ctx:], proj[:, :t_ctx], hp)
    o = jnp.concatenate([o_ctx, o_lat], axis=1).reshape(rows, st.d)
    y = _mm(o, w_out).reshape(st.bn, st.n, st.d)
    return _post_norm(st, x, y, mod, 1, lg, lb, 1.0, alpha), (h, heads_vjp, o, y)


def _mixer_bwd(st, x, mod, t_ctx, w_in, w_out, lg, alpha, kept, dout):
    h, heads_vjp, o, y = kept
    rows = st.bn * st.n
    dxr, dy, dgate, dlg, dlb = _post_norm_bwd(st, x, y, dout, mod, 1, lg, 1.0, alpha)
    dy = dy.reshape(rows, st.d)
    dw_out = _mm(o, dy, 'tn')
    do = _mm(dy, w_out, 'nt').reshape(st.bn, st.n, st.d)
    dproj, dproj_c, dhp = heads_vjp((do[:, t_ctx:], do[:, :t_ctx]))
    dp = jnp.concatenate([dproj_c, dproj], axis=1).reshape(rows, IN_COLS_PAD)
    dw_in = _mm(h, dp, 'tn')
    dh = _mm(dp, w_in, 'nt').reshape(st.bn, st.n, st.d)
    dx, dsh, dsc = _modulate_bwd(st, dh, x, dxr, mod, 1)
    return dx, (dsh, dsc, dgate), dw_in, dw_out, jnp.sum(dlg, axis=(0, 1)), jnp.sum(dlb, axis=(0, 1)), dhp


def _natural(name, gathered):
    ax = SHARD_AXIS[name]
    t = jnp.moveaxis(gathered, 0, ax)
    return t.reshape(t.shape[:ax] + (t.shape[ax] * t.shape[ax + 1],) + t.shape[ax + 2:])


def _to_shards(name, full):
    ax = SHARD_AXIS[name]
    t = full.reshape(full.shape[:ax] + (N_CHIPS, full.shape[ax] // N_CHIPS) + full.shape[ax + 1:])
    return jnp.moveaxis(t, ax, 0)


def _pack(arrays):
    flat = jnp.concatenate([a.reshape(-1) for a in arrays])
    pad = (-flat.size) % (SUBLANES * LANES)
    return jnp.pad(flat, (0, pad)).reshape(-1, LANES)


def _unpack(packed, like):
    flat = packed.reshape(-1)
    out, at = [], 0
    for a in like:
        out.append(flat[at:at + a.size].reshape(a.shape))
        at += a.size
    return out


def kernel(x, c, ctx, c_ctx, w_ada, b_ada, ln_g, ln_b, ffn_w1, ffn_w3, ffn_w2, w_in, w_out, gdn_conv_w, gdn_a_log, gdn_dt_bias, gdn_norm_w, q_norm_w, k_norm_w, s5_lam_re, s5_lam_im, s5_log_dt, s5_b_re, s5_b_im, s5_c_re, s5_c_im, s5_d, glu_w, glu_b, loss_target, m_c_ctx, m_w_ada, m_b_ada, m_ln_g, m_ln_b, m_ffn_w1, m_ffn_w3, m_ffn_w2, m_w_in, m_w_out, m_gdn_conv_w, m_gdn_a_log, m_gdn_dt_bias, m_gdn_norm_w, m_q_norm_w, m_k_norm_w, m_s5_lam_re, m_s5_lam_im, m_s5_log_dt, m_s5_b_re, m_s5_b_im, m_s5_c_re, m_s5_c_im, m_s5_d, m_glu_w, m_glu_b, v_c_ctx, v_w_ada, v_b_ada, v_ln_g, v_ln_b, v_ffn_w1, v_ffn_w3, v_ffn_w2, v_w_in, v_w_out, v_gdn_conv_w, v_gdn_a_log, v_gdn_dt_bias, v_gdn_norm_w, v_q_norm_w, v_k_norm_w, v_s5_lam_re, v_s5_lam_im, v_s5_log_dt, v_s5_b_re, v_s5_b_im, v_s5_c_re, v_s5_c_im, v_s5_d, v_glu_w, v_glu_b):
    given = dict(locals())
    w = {n: given[n] for n in WEIGHTS}
    mom = {n: given['m_' + n] for n in WEIGHTS}
    var = {n: given['v_' + n] for n in WEIGHTS}
    depth = w_ada.shape[0]
    bn, t_lat, d = x.shape
    alpha = (2.0 * depth) ** 0.25
    chip = 2 * lax.axis_index("x") + lax.axis_index("y")

    gathered = _gather_weight_shards([w[n].astype(BF16) for n in BIG])
    full = {n: _natural(n, g) for n, g in zip(BIG, gathered)}
    full['w_in'] = jnp.pad(full['w_in'], ((0, 0), (0, 0), (0, IN_COLS_PAD - IN_COLS)))
    small_sh = [w[n] for n in SMALL_SHARDED]
    small_all = _all_gather8(_pack(small_sh))
    for n, parts in zip(SMALL_SHARDED, zip(*[_unpack(small_all[2 * j], small_sh) for j in range(N_CHIPS)])):
        full[n] = _natural(n, jnp.stack(parts))
    for n in REPLICATED:
        full[n] = w[n]

    cos, sin = _axial_rope_tables(t_lat // GRID_W)
    act = jnp.zeros((ADA_ROWS, d), F32).at[:bn].set(jax.nn.silu(c)).at[bn].set(jax.nn.silu(c_ctx))
    t_ctx = ctx.shape[1]
    st = _Stream(bn, t_ctx, t_lat, d)
    xs = jnp.concatenate([ctx, x], axis=1)
    saved = []
    for l in range(depth):
        mod = (_mm(act, full['w_ada'][l]) + full['b_ada'][l]).reshape(ADA_ROWS, N_MOD, d)
        hp = {n: full[n][l] for n in HEAD_PARAMS}
        lg, lb = full['ln_g'][l], full['ln_b'][l]
        f1 = (full['ffn_w1'][l], full['ffn_w3'][l], full['ffn_w2'][l])
        x0 = xs
        x1, kept1 = _ffn_fwd(st, x0, mod, 0, f1[0][0], f1[1][0], f1[2][0], lg[0], lb[0], alpha)
        x2, kept2 = _mixer_fwd(st, x1, mod, t_ctx, full['w_in'][l], full['w_out'][l], lg[1], lb[1], hp, cos, sin, alpha)
        xs, kept3 = _ffn_fwd(st, x2, mod, 2, f1[0][1], f1[1][1], f1[2][1], lg[2], lb[2], alpha)
        saved.append((x0, x1, x2, mod, kept1, kept2, kept3))

    err = xs[:, t_ctx:] - loss_target
    loss = lax.psum(0.5 * jnp.sum(jnp.mean(err * err, axis=-1)), ("x", "y", "c"))
    dxs = jnp.concatenate([jnp.zeros((bn, t_ctx, d), F32), err / d], axis=1)

    grads = {n: [None] * depth for n in WEIGHTS if n != 'c_ctx'}
    dact = jnp.zeros((ADA_ROWS, d), F32)
    for l in reversed(range(depth)):
        x0, x1, x2, mod, kept1, kept2, kept3 = saved[l]
        lg = full['ln_g'][l]
        f1 = (full['ffn_w1'][l], full['ffn_w3'][l], full['ffn_w2'][l])
        dx2, dm3, dw1b, dw3b, dw2b, dlg2, dlb2 = _ffn_bwd(st, x2, mod, 2, f1[0][1], f1[1][1], f1[2][1], lg[2], alpha,
                                                           kept3, dxs)
        dx1, dm2, dw_in, dw_out, dlg1, dlb1, dhp = _mixer_bwd(st, x1, mod, t_ctx, full['w_in'][l], full['w_out'][l],
                                                              lg[1], alpha, kept2, dx2)
        dxs, dm1, dw1a, dw3a, dw2a, dlg0, dlb0 = _ffn_bwd(st, x0, mod, 0, f1[0][0], f1[1][0], f1[2][0], lg[0], alpha,
                                                          kept1, dx1)
        parts = jnp.stack([t[:, :, 0] for grp in (dm1, dm2, dm3) for t in grp], axis=2)
        dmod = jnp.concatenate([parts[:, 1], jnp.sum(parts[:, 0], axis=0, keepdims=True),
                                jnp.zeros((ADA_ROWS - bn - 1, N_MOD, d), F32)], axis=0).reshape(ADA_ROWS, N_MOD * d)
        grads['w_ada'][l] = _mm(act, dmod, 'tn')
        grads['b_ada'][l] = jnp.sum(dmod, axis=0)
        dact = dact + _mm(dmod, full['w_ada'][l], 'nt')
        grads['ffn_w1'][l] = jnp.stack([dw1a, dw1b])
        grads['ffn_w3'][l] = jnp.stack([dw3a, dw3b])
        grads['ffn_w2'][l] = jnp.stack([dw2a, dw2b])
        grads['w_in'][l] = dw_in[:, :IN_COLS]
        grads['w_out'][l] = dw_out
        grads['ln_g'][l] = jnp.stack([dlg0, dlg1, dlg2])
        grads['ln_b'][l] = jnp.stack([dlb0, dlb1, dlb2])
        for n in HEAD_PARAMS:
            grads[n][l] = dhp[n]
    grad = {n: jnp.stack(g) for n, g in grads.items()}
    sig = jax.nn.sigmoid(c_ctx)
    grad['c_ctx'] = dact[bn] * (sig * (1.0 + c_ctx * (1.0 - sig)))

    half = depth // 2
    laid = []
    for n in BIG:
        s = _to_shards(n, grad[n])
        s = s.reshape((N_CHIPS, 2, half) + s.shape[2:])
        laid.append(jnp.moveaxis(s, 1, 0))
    theirs = _swap_halves(laid)
    pair = [_add_sibling(g, r) for g, r in zip(laid, theirs)]
    landed = _scatter_partials([lo for _, lo in pair])
    mine = [_add_chips(p, r) for (p, _), r in zip(pair, landed)]
    reduced = dict(zip(BIG, _share_halves(mine)))

    small_names = REPLICATED + SMALL_SHARDED
    small_grads = [grad[n] for n in small_names]
    summed = _unpack(_sum_devices(_all_gather8(_pack(small_grads))), small_grads)
    for n, g in zip(small_names, summed):
        if n in SMALL_SHARDED:
            ax = SHARD_AXIS[n]
            width = g.shape[ax] // N_CHIPS
            g = lax.dynamic_slice_in_dim(g, chip * width, width, axis=ax)
        reduced[n] = g

    delta, new_m, new_v = {}, {}, {}
    for n in BIG:
        delta[n], new_m[n], new_v[n] = _adamw(w[n], reduced[n], mom[n], var[n])
    packs = [_pack([t[n] for n in small_names]) for t in (w, reduced, mom, var)]
    like = [w[n] for n in small_names]
    for res, packed in zip((delta, new_m, new_v), _adamw(*packs)):
        res.update(zip(small_names, _unpack(packed, like)))

    return (loss, dxs[:, t_ctx:], *[reduced[n] for n in WEIGHTS], *[delta[n] for n in WEIGHTS],
            *[new_m[n] for n in WEIGHTS], *[new_v[n] for n in WEIGHTS])
```

```python
import functools
import math

import jax
import jax.numpy as jnp
from jax import lax
from jax.experimental import pallas as pl
from jax.experimental.pallas import tpu as pltpu

F32 = jnp.float32
BF16 = jnp.bfloat16
MESH_IDS = pl.DeviceIdType.MESH
ANY = pl.BlockSpec(memory_space=pl.ANY)
N_CHIPS = 4
N_DEV = 8
LANES = 128
SUBLANES = 8
ADA_ROWS = 128
VMEM_LIMIT = 48 * 1024 * 1024

D_MODEL = 1024
GRID_W = 64
GDN_HEADS = 6
GDN_DK = 64
GDN_DV = 64
GDN_W = GDN_HEADS * GDN_DV
GDN_QKV = GDN_HEADS * (2 * GDN_DK + GDN_DV)
CONV_K = 5
CHUNK = 64
ATT_HEADS = 6
ATT_KV_HEADS = 2
ATT_DH = 64
ATT_W = ATT_HEADS * ATT_DH
ATT_GROUP = ATT_HEADS // ATT_KV_HEADS
Q_BLOCK = 128
ROPE_THETA = 10000.0
ROPE_PAIRS = ATT_DH // 4
S5_GROUPS = 16
S5_GH = 16
S5_P = 64
S5_W = S5_GROUPS * S5_GH
S5_LC = 32
S5_ROWS = SUBLANES
S5_CW = S5_LC * S5_GH
S5_SW = 2 * S5_P
S5W_LC = 8
S5W_RW = S5W_LC * S5_W
S5W_SW = S5_GROUPS * S5_SW
S5W_TILE = 4 * S5_SW
GDN_PREP_CHUNKS = 6
GDN_SCAN_CHAINS = 16
HI = lax.Precision.HIGHEST
BF16X3 = lax.Precision.HIGH
N_MOD = 9
EPS = 1e-6
OFF_GDN_Z = GDN_QKV
OFF_GDN_B = OFF_GDN_Z + GDN_W
OFF_GDN_A = OFF_GDN_B + 2 * GDN_HEADS
OFF_ATT_Q = OFF_GDN_A + 2 * GDN_HEADS
OFF_ATT_K = OFF_ATT_Q + ATT_W
OFF_ATT_V = OFF_ATT_K + ATT_KV_HEADS * ATT_DH
OFF_S5 = OFF_ATT_V + ATT_KV_HEADS * ATT_DH
IN_COLS = OFF_S5 + S5_W
IN_COLS_PAD = 2560
IN_CUTS = (OFF_GDN_Z, OFF_GDN_B, OFF_GDN_A, OFF_ATT_Q, OFF_ATT_K, OFF_ATT_V, OFF_S5)

ADAM_LR = 0.001
ADAM_B1 = 0.9
ADAM_B2 = 0.999
ADAM_EPS = 1e-08
ADAM_WD = 0.01
ADAM_STEP = 10

WEIGHTS = ['c_ctx', 'w_ada', 'b_ada', 'ln_g', 'ln_b', 'ffn_w1', 'ffn_w3', 'ffn_w2', 'w_in', 'w_out', 'gdn_conv_w',
           'gdn_a_log', 'gdn_dt_bias', 'gdn_norm_w', 'q_norm_w', 'k_norm_w', 's5_lam_re', 's5_lam_im', 's5_log_dt',
           's5_b_re', 's5_b_im', 's5_c_re', 's5_c_im', 's5_d', 'glu_w', 'glu_b']
BIG = ['w_ada', 'ffn_w1', 'ffn_w3', 'ffn_w2', 'w_in', 'w_out']
SMALL_SHARDED = ['ln_g', 'ln_b', 'gdn_conv_w', 'glu_w']
SHARD_AXIS = {'w_ada': 2, 'ffn_w1': 3, 'ffn_w3': 3, 'ffn_w2': 2, 'w_in': 2, 'w_out': 1,
              'ln_g': 2, 'ln_b': 2, 'gdn_conv_w': 2, 'glu_w': 1}
REPLICATED = [n for n in WEIGHTS if n not in BIG and n not in SMALL_SHARDED]
HEAD_PARAMS = ['gdn_conv_w', 'gdn_a_log', 'gdn_dt_bias', 'gdn_norm_w', 'q_norm_w', 'k_norm_w', 's5_lam_re',
               's5_lam_im', 's5_log_dt', 's5_b_re', 's5_b_im', 's5_c_re', 's5_c_im', 's5_d', 'glu_w', 'glu_b']


def _place():
    return lax.axis_index("x"), lax.axis_index("y"), lax.axis_index("c")


def _pick(n, cands):
    for t in cands:
        if n % t == 0:
            return t
    return n


def _remote(src, dst, send_sem, recv_sem, to):
    return pltpu.make_async_remote_copy(src_ref=src, dst_ref=dst, send_sem=send_sem, recv_sem=recv_sem,
                                        device_id=to, device_id_type=MESH_IDS)


def _gather_weight_shards(shards):
    n = len(shards)
    halves = [s.shape[0] // 2 for s in shards]

    def body(*refs):
        ins, outs = refs[:n], refs[n:2 * n]
        send_sems, recv_sems = refs[2 * n:]
        x, y, c = _place()
        chip = 2 * x + y
        sibling = (x, y, 1 - c)
        others = [(1 - x, y), (x, 1 - y), (1 - x, 1 - y)]
        sends = []
        for i in range(n):
            h = halves[i]
            for j, (px, py) in enumerate(others):
                k = 6 * i + j
                cp = _remote(ins[i].at[pl.ds(c * h, h)], outs[i].at[chip, pl.ds(c * h, h)],
                             send_sems.at[k], recv_sems.at[k], (px, py, c))
                cp.start()
                sends.append(cp)
        for i in range(n):
            h = halves[i]
            for j, (px, py) in enumerate(others):
                slab = outs[i].at[2 * px + py, pl.ds(c * h, h)]
                _remote(slab, slab, send_sems.at[6 * i + j], recv_sems.at[6 * i + j], (px, py, c)).wait_recv()
                fw = _remote(slab, slab, send_sems.at[6 * i + 3 + j], recv_sems.at[6 * i + 3 + j], sibling)
                fw.start()
                sends.append(fw)
        for i in range(n):
            h = halves[i]
            for j, (px, py) in enumerate(others):
                slab = outs[i].at[2 * px + py, pl.ds((1 - c) * h, h)]
                _remote(slab, slab, send_sems.at[6 * i + 3 + j], recv_sems.at[6 * i + 3 + j], sibling).wait_recv()
        for cp in sends:
            cp.wait_send()

    gathered = pl.pallas_call(
        body, name="gather_weight_shards",
        out_shape=[jax.ShapeDtypeStruct((N_CHIPS,) + s.shape, s.dtype) for s in shards],
        in_specs=[ANY] * n, out_specs=[ANY] * n,
        scratch_shapes=[pltpu.SemaphoreType.DMA((6 * n,)), pltpu.SemaphoreType.DMA((6 * n,))],
    )(*shards)
    chip = 2 * lax.axis_index("x") + lax.axis_index("y")
    return [lax.dynamic_update_slice_in_dim(g, s[None], chip, axis=0) for g, s in zip(gathered, shards)]


def _all_gather8(v):
    def body(v_ref, out_ref, send_sems, recv_sems, local_sem):
        x, y, c = _place()
        me, sibling = (x, y, c), (x, y, 1 - c)
        chips = [(1 - x, y), (x, 1 - y), (1 - x, 1 - y)]

        def slot(px, py, pc):
            return out_ref.at[4 * px + 2 * py + pc]

        def copy(k, block, to, src=None):
            return _remote(slot(*block) if src is None else src, slot(*block), send_sems.at[k], recv_sems.at[k], to)

        mine = pltpu.make_async_copy(v_ref, slot(*me), local_sem)
        mine.start()
        first = [copy(0, me, sibling, src=v_ref)]
        first += [copy(1 + j, me, (*chip, c), src=v_ref) for j, chip in enumerate(chips)]
        for cp in first:
            cp.start()
        passed = [copy(4 + j, (*chip, c), sibling) for j, chip in enumerate(chips)]
        for j, chip in enumerate(chips):
            copy(1 + j, (*chip, c), me).wait_recv()
            passed[j].start()
        copy(0, sibling, me).wait_recv()
        for j, chip in enumerate(chips):
            copy(4 + j, (*chip, 1 - c), me).wait_recv()
        for cp in first + passed:
            cp.wait_send()
        mine.wait()

    return pl.pallas_call(
        body, name="all_gather8",
        out_shape=jax.ShapeDtypeStruct((N_DEV,) + v.shape, v.dtype),
        in_specs=[ANY], out_specs=ANY,
        scratch_shapes=[pltpu.SemaphoreType.DMA((7,)), pltpu.SemaphoreType.DMA((7,)), pltpu.SemaphoreType.DMA],
    )(v)


def _swap_halves(grads):
    n = len(grads)

    def body(*refs):
        ins, outs = refs[:n], refs[n:2 * n]
        send_sems, recv_sems = refs[2 * n:]
        x, y, c = _place()
        cps = [_remote(ins[i].at[1 - c], outs[i], send_sems.at[i], recv_sems.at[i], (x, y, 1 - c)) for i in range(n)]
        for cp in cps:
            cp.start()
        for cp in cps:
            cp.wait()

    return pl.pallas_call(
        body, name="swap_halves",
        out_shape=[jax.ShapeDtypeStruct(g.shape[1:], g.dtype) for g in grads],
        in_specs=[ANY] * n, out_specs=[ANY] * n,
        scratch_shapes=[pltpu.SemaphoreType.DMA((n,)), pltpu.SemaphoreType.DMA((n,))],
    )(*grads)


def _scatter_partials(parts):
    n = len(parts)

    def body(*refs):
        ins, outs = refs[:n], refs[n:2 * n]
        send_sems, recv_sems = refs[2 * n:]
        x, y, c = _place()
        others = [(1 - x, y), (x, 1 - y), (1 - x, 1 - y)]
        cps = []
        for i in range(n):
            for j, (px, py) in enumerate(others):
                k = 3 * i + j
                cps.append(_remote(ins[i].at[2 * px + py], outs[i].at[j], send_sems.at[k], recv_sems.at[k], (px, py, c)))
        for cp in cps:
            cp.start()
        for cp in cps:
            cp.wait()

    return pl.pallas_call(
        body, name="scatter_partials",
        out_shape=[jax.ShapeDtypeStruct((3,) + p.shape[1:], p.dtype) for p in parts],
        in_specs=[ANY] * n, out_specs=[ANY] * n,
        scratch_shapes=[pltpu.SemaphoreType.DMA((3 * n,)), pltpu.SemaphoreType.DMA((3 * n,))],
    )(*parts)


def _share_halves(halves):
    n = len(halves)

    def body(*refs):
        ins, outs = refs[:n], refs[n:2 * n]
        send_sems, recv_sems = refs[2 * n:]
        x, y, c = _place()
        cps = [_remote(ins[i], outs[i], send_sems.at[i], recv_sems.at[i], (x, y, 1 - c)) for i in range(n)]
        for cp in cps:
            cp.start()
        for cp in cps:
            cp.wait()

    theirs = pl.pallas_call(
        body, name="share_halves",
        out_shape=[jax.ShapeDtypeStruct(p.shape, p.dtype) for p in halves],
        in_specs=[ANY] * n, out_specs=[ANY] * n,
        scratch_shapes=[pltpu.SemaphoreType.DMA((n,)), pltpu.SemaphoreType.DMA((n,))],
    )(*halves)
    south = lax.axis_index("c") == 0
    return [jnp.concatenate([jnp.where(south, a, b), jnp.where(south, b, a)], axis=0) for a, b in zip(halves, theirs)]


def _row_tile(rows, cols, n_arrays):
    budget = (VMEM_LIMIT // 3) // (2 * n_arrays * 4 * max(cols, LANES))
    for t in (1024, 512, 256, 128, 64, 32, 16, 8):
        if t <= budget and rows % t == 0:
            return t
    return rows


def _add_sibling(grad, recv):
    cols = grad.shape[-1]
    rows = recv.size // cols
    g3 = grad.reshape(2, rows, cols)
    r2 = recv.reshape(rows, cols)
    tr = _row_tile(rows, cols, 4)

    def body(c_ref, g_ref, r_ref, o_ref, lo_ref):
        total = g_ref[...] + r_ref[...]
        o_ref[...] = total
        lo_ref[...] = total.astype(BF16)

    spec = pl.BlockSpec((tr, cols), lambda i, c_ref: (i, 0))
    out, lo = pl.pallas_call(
        body, name="add_sibling",
        grid_spec=pltpu.PrefetchScalarGridSpec(
            num_scalar_prefetch=1, grid=(rows // tr,),
            in_specs=[pl.BlockSpec((None, tr, cols), lambda i, c_ref: (c_ref[0], i, 0)), spec],
            out_specs=[spec, spec]),
        out_shape=[jax.ShapeDtypeStruct((rows, cols), F32), jax.ShapeDtypeStruct((rows, cols), BF16)],
        compiler_params=pltpu.CompilerParams(vmem_limit_bytes=VMEM_LIMIT),
    )(lax.axis_index("c").astype(jnp.int32).reshape(1), g3, r2)
    return out.reshape(recv.shape), lo.reshape(recv.shape)


def _add_chips(part, recv):
    cols = part.shape[-1]
    rows = part[0].size // cols
    p3 = part.reshape(N_CHIPS, rows, cols)
    r3 = recv.reshape(3, rows, cols)
    tr = _row_tile(rows, cols, 5)

    def body(chip_ref, p_ref, r0_ref, r1_ref, r2_ref, o_ref):
        o_ref[...] = ((p_ref[...] + r0_ref[...].astype(F32)) + r1_ref[...].astype(F32)) + r2_ref[...].astype(F32)

    def recv_spec(j):
        return pl.BlockSpec((None, tr, cols), lambda i, chip_ref: (j, i, 0))

    chip = (2 * lax.axis_index("x") + lax.axis_index("y")).astype(jnp.int32).reshape(1)
    out = pl.pallas_call(
        body, name="add_chips",
        grid_spec=pltpu.PrefetchScalarGridSpec(
            num_scalar_prefetch=1, grid=(rows // tr,),
            in_specs=[pl.BlockSpec((None, tr, cols), lambda i, chip_ref: (chip_ref[0], i, 0)),
                      recv_spec(0), recv_spec(1), recv_spec(2)],
            out_specs=pl.BlockSpec((tr, cols), lambda i, chip_ref: (i, 0))),
        out_shape=jax.ShapeDtypeStruct((rows, cols), F32),
        compiler_params=pltpu.CompilerParams(vmem_limit_bytes=VMEM_LIMIT),
    )(chip, p3, r3, r3, r3)
    return out.reshape(part.shape[1:])


def _sum_devices(gathered):
    _, rows, cols = gathered.shape
    tr = _row_tile(rows, cols, 9)

    def body(g_ref, o_ref):
        acc = g_ref[0]
        for k in range(1, N_DEV):
            acc = acc + g_ref[k]
        o_ref[...] = acc

    return pl.pallas_call(
        body, name="sum_devices", grid=(rows // tr,),
        in_specs=[pl.BlockSpec((N_DEV, tr, cols), lambda i: (0, i, 0))],
        out_specs=pl.BlockSpec((tr, cols), lambda i: (i, 0)),
        out_shape=jax.ShapeDtypeStruct((rows, cols), F32),
        compiler_params=pltpu.CompilerParams(vmem_limit_bytes=VMEM_LIMIT),
    )(gathered)


def _adamw(w, g, m, v):
    shape = w.shape
    cols = shape[-1]
    rows = w.size // cols
    tr = _row_tile(rows, cols, 7)

    def body(w_ref, g_ref, m_ref, v_ref, d_ref, nm_ref, nv_ref):
        gv = g_ref[...]
        nm = ADAM_B1 * m_ref[...] + (1.0 - ADAM_B1) * gv
        nv = ADAM_B2 * v_ref[...] + (1.0 - ADAM_B2) * (gv * gv)
        m_hat = nm / (1.0 - ADAM_B1 ** ADAM_STEP)
        v_hat = nv / (1.0 - ADAM_B2 ** ADAM_STEP)
        d_ref[...] = -ADAM_LR * (m_hat / (jnp.sqrt(v_hat) + ADAM_EPS) + ADAM_WD * w_ref[...])
        nm_ref[...] = nm
        nv_ref[...] = nv

    spec = pl.BlockSpec((tr, cols), lambda i: (i, 0))
    outs = pl.pallas_call(
        body, name="adamw", grid=(rows // tr,),
        in_specs=[spec] * 4, out_specs=[spec] * 3,
        out_shape=[jax.ShapeDtypeStruct((rows, cols), F32)] * 3,
        compiler_params=pltpu.CompilerParams(vmem_limit_bytes=VMEM_LIMIT),
    )(*[t.reshape(rows, cols) for t in (w, g, m, v)])
    return tuple(o.reshape(shape) for o in outs)


_DOT_DIMS = {'nn': (((1,), (0,)), ((), ())), 'nt': (((1,), (1,)), ((), ())), 'tn': (((0,), (0,)), ((), ()))}


def _form_operand(kind, tiles):
    if kind == 'plain':
        return tiles[0]
    if kind == 'swiglu':
        a, b = tiles
        return a * jax.nn.sigmoid(a) * b
    ds, a, b = tiles
    sig = jax.nn.sigmoid(a)
    return ds * b * (sig * (1.0 + a * (1.0 - sig))) if kind == 'dswiglu_a' else ds * (a * sig)


def _mm(a, b, mode='nn', init=None):
    (lkind, *larr) = a if isinstance(a, tuple) else ('plain', a)
    (rkind, *rarr) = b if isinstance(b, tuple) else ('plain', b)
    if mode == 'nn':
        (m, k), (_, n) = larr[0].shape, rarr[0].shape
    elif mode == 'nt':
        (m, k), (n, _) = larr[0].shape, rarr[0].shape
    else:
        (k, m), (_, n) = larr[0].shape, rarr[0].shape
    tn = _pick(n, (1408, 1280, 1024, 512, 256, 128))
    if mode == 'tn':
        tm = _pick(m, (1408, 1024, 512, 256, 128))
        tk = _pick(k, (512, 256, 128, 64, 32, 16, 8))
    else:
        tm = _pick(m, (512, 256, 128) if len(larr) == 3 else (1024, 768, 512, 256, 128, 64, 32, 16, 8))
        tk = _pick(k, (1408, 1280, 1024, 512, 256, 128))
    nk = k // tk
    nl, nr, seeded = len(larr), len(rarr), init is not None

    def body(*refs):
        o_ref, acc_ref = refs[-2], refs[-1]
        step = pl.program_id(2)

        @pl.when(step == 0)
        def _():
            acc_ref[...] = refs[nl + nr][...] if seeded else jnp.zeros_like(acc_ref)

        left = _form_operand(lkind, [r[...] for r in refs[:nl]])
        right = _form_operand(rkind, [r[...] for r in refs[nl:nl + nr]])
        acc_ref[...] += lax.dot_general(left.astype(BF16), right.astype(BF16), _DOT_DIMS[mode],
                                        preferred_element_type=F32)

        @pl.when(step == nk - 1)
        def _():
            o_ref[...] = acc_ref[...]

    if mode == 'tn':
        a_spec = pl.BlockSpec((tk, tm), lambda i, j, s: (s, i))
    else:
        a_spec = pl.BlockSpec((tm, tk), lambda i, j, s: (i, s))
    if mode == 'nt':
        b_spec = pl.BlockSpec((tn, tk), lambda i, j, s: (j, s))
    else:
        b_spec = pl.BlockSpec((tk, tn), lambda i, j, s: (s, j))
    o_spec = pl.BlockSpec((tm, tn), lambda i, j, s: (i, j))
    return pl.pallas_call(
        body, name=f"mm_{mode}_{lkind}_{rkind}{'_seeded' if seeded else ''}_{m}x{k}x{n}",
        grid=(m // tm, n // tn, nk),
        in_specs=[a_spec] * nl + [b_spec] * nr + [o_spec] * seeded, out_specs=o_spec,
        out_shape=jax.ShapeDtypeStruct((m, n), F32),
        scratch_shapes=[pltpu.VMEM((tm, tn), F32)],
        compiler_params=pltpu.CompilerParams(dimension_semantics=("parallel", "parallel", "arbitrary"),
                                             vmem_limit_bytes=VMEM_LIMIT),
    )(*larr, *rarr, *([init] if seeded else []))


class _Stream:
    def __init__(self, bn, t_ctx, t_lat, d):
        self.bn, self.n, self.d = bn, t_ctx + t_lat, d
        self.tr = _pick(math.gcd(t_ctx, t_lat), (256, 128, 64, 32, 16, 8))
        self.ctx_tiles = t_ctx // self.tr
        self.grid = (bn, self.n // self.tr)
        nct, rows = self.ctx_tiles, bn
        self.tok = pl.BlockSpec((None, self.tr, d), lambda b, i: (b, i, 0))
        self.mod = pl.BlockSpec((None, N_MOD, d), lambda b, i: (jnp.where(i < nct, rows, b), 0, 0))
        self.vec = pl.BlockSpec((1, d), lambda b, i: (0, 0))
        self.part = pl.BlockSpec((None, None, 1, d), lambda b, i: (b, jnp.where(i < nct, 0, 1), 0, 0))
        self.per_example = pl.BlockSpec((None, 1, d), lambda b, i: (b, 0, 0))
        self.tok_shape = jax.ShapeDtypeStruct((bn, self.n, d), F32)
        self.part_shape = jax.ShapeDtypeStruct((bn, 2, 1, d), F32)
        self.example_shape = jax.ShapeDtypeStruct((bn, 1, d), F32)
        self.params = pltpu.CompilerParams(dimension_semantics=("parallel", "arbitrary"), vmem_limit_bytes=VMEM_LIMIT)

    def starts_part(self, i):
        return (i == 0) | (i == self.ctx_tiles)


def _modulate(st, x, mod, k):
    def body(x_ref, m_ref, o_ref):
        o_ref[...] = x_ref[...] * (1.0 + m_ref[3 * k + 1:3 * k + 2, :]) + m_ref[3 * k:3 * k + 1, :]

    return pl.pallas_call(body, name=f"modulate_{k}", grid=st.grid, in_specs=[st.tok, st.mod], out_specs=st.tok,
                          out_shape=st.tok_shape, compiler_params=st.params)(x, mod)


def _norm_stats(z):
    mu = jnp.mean(z, axis=-1, keepdims=True)
    zc = z - mu
    rstd = lax.rsqrt(jnp.mean(zc * zc, axis=-1, keepdims=True) + EPS)
    return zc * rstd, rstd


def _post_norm(st, x, y, mod, k, lg, lb, rw, alpha):
    def body(x_ref, y_ref, m_ref, g_ref, b_ref, o_ref):
        xhat, _ = _norm_stats(alpha * x_ref[...] + (rw * m_ref[3 * k + 2:3 * k + 3, :]) * y_ref[...])
        o_ref[...] = xhat * g_ref[...] + b_ref[...]

    return pl.pallas_call(body, name=f"post_norm_{k}", grid=st.grid,
                          in_specs=[st.tok, st.tok, st.mod, st.vec, st.vec], out_specs=st.tok,
                          out_shape=st.tok_shape, compiler_params=st.params)(x, y, mod, lg[None], lb[None])


def _post_norm_bwd(st, x, y, dout, mod, k, lg, rw, alpha):
    def body(x_ref, y_ref, do_ref, m_ref, g_ref, dxr_ref, dy_ref, dgate_ref, dlg_ref, dlb_ref):
        i = pl.program_id(1)
        gate = rw * m_ref[3 * k + 2:3 * k + 3, :]
        yv, dov = y_ref[...], do_ref[...]
        xhat, rstd = _norm_stats(alpha * x_ref[...] + gate * yv)
        dxhat = dov * g_ref[...]
        dz = rstd * (dxhat - jnp.mean(dxhat, axis=-1, keepdims=True)
                     - xhat * jnp.mean(dxhat * xhat, axis=-1, keepdims=True))
        dxr_ref[...] = alpha * dz
        dy_ref[...] = gate * dz

        @pl.when(i == 0)
        def _():
            dlg_ref[...] = jnp.zeros_like(dlg_ref)
            dlb_ref[...] = jnp.zeros_like(dlb_ref)

        @pl.when(st.starts_part(i))
        def _():
            dgate_ref[...] = jnp.zeros_like(dgate_ref)

        dlg_ref[...] += jnp.sum(dov * xhat, axis=0, keepdims=True)
        dlb_ref[...] += jnp.sum(dov, axis=0, keepdims=True)
        dgate_ref[...] += jnp.sum(rw * yv * dz, axis=0, keepdims=True)

    return pl.pallas_call(
        body, name=f"post_norm_bwd_{k}", grid=st.grid, in_specs=[st.tok, st.tok, st.tok, st.mod, st.vec],
        out_specs=[st.tok, st.tok, st.part, st.per_example, st.per_example],
        out_shape=[st.tok_shape, st.tok_shape, st.part_shape, st.example_shape, st.example_shape],
        compiler_params=st.params)(x, y, dout, mod, lg[None])


def _modulate_bwd(st, dh, x, dxr, mod, k):
    def body(dh_ref, x_ref, dxr_ref, m_ref, dx_ref, dsh_ref, dsc_ref):
        dhv = dh_ref[...]
        dx_ref[...] = dxr_ref[...] + dhv * (1.0 + m_ref[3 * k + 1:3 * k + 2, :])

        @pl.when(st.starts_part(pl.program_id(1)))
        def _():
            dsh_ref[...] = jnp.zeros_like(dsh_ref)
            dsc_ref[...] = jnp.zeros_like(dsc_ref)

        dsh_ref[...] += jnp.sum(dhv, axis=0, keepdims=True)
        dsc_ref[...] += jnp.sum(dhv * x_ref[...], axis=0, keepdims=True)

    return pl.pallas_call(
        body, name=f"modulate_bwd_{k}", grid=st.grid, in_specs=[st.tok, st.tok, st.tok, st.mod],
        out_specs=[st.tok, st.part, st.part], out_shape=[st.tok_shape, st.part_shape, st.part_shape],
        compiler_params=st.params)(dh, x, dxr, mod)


def _rms_norm(x, w):
    return x * lax.rsqrt(jnp.mean(x * x, axis=-1, keepdims=True) + EPS) * w


def _l2_normalize(x):
    return x * lax.rsqrt(jnp.sum(x * x, axis=-1, keepdims=True) + EPS)


def _ffn_fwd(st, x, mod, k, w1, w3, w2, lg, lb, alpha):
    rows = st.bn * st.n
    h = _modulate(st, x, mod, k).reshape(rows, st.d)
    a, b = _mm(h, w1), _mm(h, w3)
    y = _mm(('swiglu', a, b), w2).reshape(st.bn, st.n, st.d)
    return _post_norm(st, x, y, mod, k, lg, lb, 0.5, alpha), (h, a, b, y)


def _ffn_bwd(st, x, mod, k, w1, w3, w2, lg, alpha, kept, dout):
    h, a, b, y = kept
    rows = st.bn * st.n
    dxr, dy, dgate, dlg, dlb = _post_norm_bwd(st, x, y, dout, mod, k, lg, 0.5, alpha)
    dy = dy.reshape(rows, st.d)
    ds = _mm(dy, w2, 'nt')
    dw2 = _mm(('swiglu', a, b), dy, 'tn')
    da, db = ('dswiglu_a', ds, a, b), ('dswiglu_b', ds, a, b)
    dw1 = _mm(h, da, 'tn')
    dw3 = _mm(h, db, 'tn')
    dh = _mm(db, w3, 'nt', init=_mm(da, w1, 'nt')).reshape(st.bn, st.n, st.d)
    dx, dsh, dsc = _modulate_bwd(st, dh, x, dxr, mod, k)
    return dx, (dsh, dsc, dgate), dw1, dw3, dw2, jnp.sum(dlg, axis=(0, 1)), jnp.sum(dlb, axis=(0, 1))


def _dwconv_centred(x, w):
    pad = CONV_K // 2
    return lax.conv_general_dilated(
        x, w[:, None, :].astype(x.dtype), window_strides=(1,), padding=[(pad, pad)],
        dimension_numbers=('NWC', 'WIO', 'NWC'), feature_group_count=x.shape[-1])


def _largest_divisor(n, cap):
    return max(t for t in range(1, cap + 1) if n % t == 0)


def _bmm(a, b):
    return lax.dot_general(a, b, (((2,), (1,)), ((0,), (0,))), precision=BF16X3, preferred_element_type=F32)


_BATCHED_DIMS = {'nn': (((2,), (1,)), ((0,), (0,))), 'nt': (((2,), (2,)), ((0,), (0,))),
                 'tn': (((1,), (1,)), ((0,), (0,)))}


def _bdot(a, b, mode):
    return lax.dot_general(a.astype(BF16), b.astype(BF16), _BATCHED_DIMS[mode], preferred_element_type=F32)


@functools.partial(jax.custom_vjp, nondiff_argnums=(2,))
def _bmm_lo(a, b, mode):
    return _bdot(a, b, mode)


def _bmm_lo_fwd(a, b, mode):
    return _bdot(a, b, mode), (a, b)


def _bmm_lo_bwd(mode, res, ct):
    a, b = res
    if mode == 'nn':
        return _bdot(ct, b, 'nt'), _bdot(a, ct, 'tn')
    if mode == 'nt':
        return _bdot(ct, b, 'nn'), _bdot(ct, a, 'tn')
    return _bdot(b, ct, 'nt'), _bdot(a, ct, 'nn')


_bmm_lo.defvjp(_bmm_lo_fwd, _bmm_lo_bwd)


def _lower_inverse_product(lower):
    n = lower.shape[0]
    eye = jnp.where(lax.broadcasted_iota(jnp.int32, (n, CHUNK, CHUNK), 1)
                    == lax.broadcasted_iota(jnp.int32, (n, CHUNK, CHUNK), 2), 1.0, 0.0)
    inv, power = eye - lower, lower
    for _ in range(5):
        power = _bmm(power, power)
        inv = _bmm(inv, eye + power)
    return inv


@jax.custom_vjp
def _unit_lower_inverse(lower):
    return _lower_inverse_product(lower)


def _unit_lower_inverse_fwd(lower):
    inv = _lower_inverse_product(lower)
    return inv, inv


def _unit_lower_inverse_bwd(inv, g):
    right = lax.dot_general(g, inv, _BATCHED_DIMS['nt'], precision=BF16X3, preferred_element_type=F32)
    return (-lax.dot_general(inv, right, _BATCHED_DIMS['tn'], precision=BF16X3, preferred_element_type=F32),)


_unit_lower_inverse.defvjp(_unit_lower_inverse_fwd, _unit_lower_inverse_bwd)


def _gdn_prep(q, k, v, gc, gr, beta):
    n = q.shape[0]
    row = lax.broadcasted_iota(jnp.int32, (n, CHUNK, CHUNK), 1)
    col = lax.broadcasted_iota(jnp.int32, (n, CHUNK, CHUNK), 2)
    incl, strict = row >= col, row > col
    decay = jnp.where(incl, jnp.exp(jnp.where(incl, gc - gr, 0.0)), 0.0)
    kb = k * beta
    lower = jnp.where(strict, _bmm_lo(kb, k, 'nt') * decay, 0.0)
    inv = _unit_lower_inverse(lower)
    u = _bmm(inv, v * beta)
    w = _bmm(inv, kb * jnp.exp(gc))
    intra = jnp.where(incl, _bmm_lo(q, k, 'nt') * decay, 0.0)
    is_last = lax.broadcasted_iota(jnp.int32, (n, CHUNK, 1), 1) == CHUNK - 1
    g_last = jnp.sum(jnp.where(is_last, gc, 0.0), axis=1, keepdims=True)
    return u, w, intra, q * jnp.exp(gc), k * jnp.exp(g_last - gc), jnp.exp(g_last) * jnp.ones((n, 1, GDN_DV), F32)


def _gdn_step(state, u, w, intra, qg, kd, gl):
    v_new = u - _bmm_lo(w, state, 'nn')
    o = _bmm_lo(qg, state, 'nn') + _bmm_lo(intra, v_new, 'nn')
    return o, state * gl + _bmm_lo(kd, v_new, 'tn')


def _gdn_prep_specs(cb):
    idx = lambda c, i: (c, i, 0, 0)
    mat = pl.BlockSpec((None, cb, CHUNK, GDN_DK), idx)
    colv = pl.BlockSpec((None, cb, CHUNK, 1), idx)
    rowv = pl.BlockSpec((None, cb, 1, CHUNK), idx)
    return mat, colv, rowv


def _gdn_prep_fwd_call(q, k, v, gc, gr, beta):
    chains, nchunks = q.shape[:2]
    cb = _largest_divisor(nchunks, GDN_PREP_CHUNKS)
    mat, colv, rowv = _gdn_prep_specs(cb)

    def body(q_ref, k_ref, v_ref, gc_ref, gr_ref, b_ref, *out_refs):
        outs = _gdn_prep(q_ref[...], k_ref[...], v_ref[...], gc_ref[...], gr_ref[...], b_ref[...])
        for ref, val in zip(out_refs, outs):
            ref[...] = val

    mshape = jax.ShapeDtypeStruct(q.shape, F32)
    return pl.pallas_call(
        body, name="gdn_prep_fwd", grid=(chains, nchunks // cb),
        in_specs=[mat, mat, mat, colv, rowv, colv], out_specs=[mat] * 5 + [rowv],
        out_shape=[mshape] * 5 + [jax.ShapeDtypeStruct(gr.shape, F32)],
        compiler_params=pltpu.CompilerParams(dimension_semantics=("parallel", "parallel"), vmem_limit_bytes=VMEM_LIMIT),
    )(q, k, v, gc, gr, beta)


def _gdn_prep_bwd_call(q, k, v, gc, gr, beta, cts):
    chains, nchunks = q.shape[:2]
    cb = _largest_divisor(nchunks, GDN_PREP_CHUNKS)
    mat, colv, rowv = _gdn_prep_specs(cb)

    def body(q_ref, k_ref, v_ref, gc_ref, gr_ref, b_ref, du, dw, di, dqg, dkd, dgl, *out_refs):
        _, vjp = jax.vjp(_gdn_prep, q_ref[...], k_ref[...], v_ref[...], gc_ref[...], gr_ref[...], b_ref[...])
        grads = vjp((du[...], dw[...], di[...], dqg[...], dkd[...], dgl[...]))
        for ref, val in zip(out_refs, grads):
            ref[...] = val

    return pl.pallas_call(
        body, name="gdn_prep_bwd", grid=(chains, nchunks // cb),
        in_specs=[mat, mat, mat, colv, rowv, colv] + [mat] * 5 + [rowv],
        out_specs=[mat, mat, mat, colv, rowv, colv],
        out_shape=[jax.ShapeDtypeStruct(t.shape, F32) for t in (q, k, v, gc, gr, beta)],
        compiler_params=pltpu.CompilerParams(dimension_semantics=("parallel", "parallel"), vmem_limit_bytes=VMEM_LIMIT),
    )(q, k, v, gc, gr, beta, *cts)


def _gdn_scan_fwd_call(u, w, intra, qg, kd, gl):
    chains, nchunks = u.shape[:2]
    cc = _largest_divisor(chains, GDN_SCAN_CHAINS)
    idx = lambda c, i: (c, i, 0, 0)
    mat = pl.BlockSpec((cc, None, CHUNK, GDN_DK), idx)
    rowv = pl.BlockSpec((cc, None, 1, CHUNK), idx)

    def body(u_ref, w_ref, i_ref, qg_ref, kd_ref, gl_ref, o_ref, hist_ref, state_ref):
        @pl.when(pl.program_id(1) == 0)
        def _():
            state_ref[...] = jnp.zeros_like(state_ref)

        state = state_ref[...]
        hist_ref[...] = state
        o, new = _gdn_step(state, u_ref[...], w_ref[...], i_ref[...], qg_ref[...], kd_ref[...], gl_ref[...])
        o_ref[...] = o
        state_ref[...] = new

    mshape = jax.ShapeDtypeStruct(u.shape, F32)
    return pl.pallas_call(
        body, name="gdn_scan_fwd", grid=(chains // cc, nchunks),
        in_specs=[mat] * 5 + [rowv], out_specs=[mat, mat], out_shape=[mshape, mshape],
        scratch_shapes=[pltpu.VMEM((cc, GDN_DK, GDN_DV), F32)],
        compiler_params=pltpu.CompilerParams(dimension_semantics=("parallel", "arbitrary"), vmem_limit_bytes=VMEM_LIMIT),
    )(u, w, intra, qg, kd, gl)


def _gdn_scan_bwd_call(u, w, intra, qg, kd, gl, hist, do):
    chains, nchunks = u.shape[:2]
    cc = _largest_divisor(chains, GDN_SCAN_CHAINS)
    idx = lambda c, i: (c, nchunks - 1 - i, 0, 0)
    mat = pl.BlockSpec((cc, None, CHUNK, GDN_DK), idx)
    rowv = pl.BlockSpec((cc, None, 1, CHUNK), idx)

    def body(u_ref, w_ref, i_ref, qg_ref, kd_ref, gl_ref, h_ref, do_ref, du, dw, di, dqg, dkd, dgl, dstate_ref):
        @pl.when(pl.program_id(1) == 0)
        def _():
            dstate_ref[...] = jnp.zeros_like(dstate_ref)

        _, vjp = jax.vjp(_gdn_step, h_ref[...], u_ref[...], w_ref[...], i_ref[...], qg_ref[...], kd_ref[...], gl_ref[...])
        grads = vjp((do_ref[...], dstate_ref[...]))
        dstate_ref[...] = grads[0]
        for ref, val in zip((du, dw, di, dqg, dkd, dgl), grads[1:]):
            ref[...] = val

    mshape = jax.ShapeDtypeStruct(u.shape, F32)
    return pl.pallas_call(
        body, name="gdn_scan_bwd", grid=(chains // cc, nchunks),
        in_specs=[mat] * 5 + [rowv, mat, mat], out_specs=[mat] * 5 + [rowv],
        out_shape=[mshape] * 5 + [jax.ShapeDtypeStruct(gl.shape, F32)],
        scratch_shapes=[pltpu.VMEM((cc, GDN_DK, GDN_DV), F32)],
        compiler_params=pltpu.CompilerParams(dimension_semantics=("parallel", "arbitrary"), vmem_limit_bytes=VMEM_LIMIT),
    )(u, w, intra, qg, kd, gl, hist, do)


@jax.custom_vjp
def _gdn_core(q, k, v, gc, gr, beta):
    return _gdn_scan_fwd_call(*_gdn_prep_fwd_call(q, k, v, gc, gr, beta))[0]


def _gdn_core_fwd(q, k, v, gc, gr, beta):
    prep = _gdn_prep_fwd_call(q, k, v, gc, gr, beta)
    o, hist = _gdn_scan_fwd_call(*prep)
    return o, (q, k, v, gc, gr, beta, prep, hist)


def _gdn_core_bwd(res, do):
    q, k, v, gc, gr, beta, prep, hist = res
    cts = _gdn_scan_bwd_call(*prep, hist, do)
    return tuple(_gdn_prep_bwd_call(q, k, v, gc, gr, beta, cts))


_gdn_core.defvjp(_gdn_core_fwd, _gdn_core_bwd)


def _gdn_inputs(qkv, b, a, conv_w, a_log, dt_bias):
    Bn, T, _ = qkv.shape
    qkv = jax.nn.silu(_dwconv_centred(qkv, conv_w))
    q, k, v = jnp.split(qkv, [GDN_HEADS * GDN_DK, 2 * GDN_HEADS * GDN_DK], axis=-1)

    def to_heads(t, d):
        return t.reshape(Bn, T, GDN_HEADS, d).transpose(0, 2, 1, 3)

    def dir_heads(t):
        return t.reshape(Bn, T, 2, GDN_HEADS).transpose(2, 0, 3, 1)

    q = _l2_normalize(to_heads(q, GDN_DK))
    k = _l2_normalize(to_heads(k, GDN_DK))
    v = to_heads(v, GDN_DV)
    beta = jax.nn.sigmoid(dir_heads(b))
    g = -jnp.exp(a_log)[:, None, :, None] * jax.nn.softplus(dir_heads(a) + dt_bias[:, None, :, None])
    return q, k, v, g, beta


def _gdn_gated_out(o, z, norm_w):
    Bn, H, T, dv = o.shape
    o = _rms_norm(o.transpose(0, 2, 1, 3), norm_w)
    o = o * jax.nn.silu(z.reshape(Bn, T, H, dv))
    return o.reshape(Bn, T, H * dv)


def _gdn_group(qkv, z, b, a, qkv_c, z_c, b_c, a_c, conv_w, a_log, dt_bias, norm_w, with_ctx_out):
    q, k, v, g, beta = _gdn_inputs(qkv, b, a, conv_w, a_log, dt_bias)
    qc, kc, vc, gc, betac = _gdn_inputs(qkv_c, b_c, a_c, conv_w, a_log, dt_bias)
    bn, heads, t_lat, _ = q.shape
    t_ctx = qc.shape[2]
    n = t_ctx + t_lat
    nchunks = n // CHUNK
    chains = 2 * bn * heads

    def both(tc, tl):
        return jnp.stack([jnp.concatenate([tc, tl], axis=2), jnp.concatenate([jnp.flip(tc, 2), jnp.flip(tl, 2)], axis=2)])

    def per_dir(tc, tl):
        return jnp.stack([jnp.concatenate([tc[0], tl[0]], axis=2),
                          jnp.concatenate([jnp.flip(tc[1], 2), jnp.flip(tl[1], 2)], axis=2)])

    def mats(t):
        return t.reshape(chains, nchunks, CHUNK, t.shape[-1])

    gcum = jnp.cumsum(per_dir(gc, g).reshape(chains, nchunks, CHUNK), axis=-1)
    o = _gdn_core(mats(both(qc, q) * GDN_DK ** -0.5), mats(both(kc, k)), mats(both(vc, v)),
                  gcum[..., None], gcum[:, :, None, :], per_dir(betac, beta).reshape(chains, nchunks, CHUNK, 1))
    o = o.reshape(2, bn, heads, n, GDN_DV)
    out = _gdn_gated_out(o[0, :, :, t_ctx:] + jnp.flip(o[1, :, :, t_ctx:], 2), z, norm_w)
    if not with_ctx_out:
        return out, None
    return out, _gdn_gated_out(o[0, :, :, :t_ctx] + jnp.flip(o[1, :, :, :t_ctx], 2), z_c, norm_w)


def _axial_rope_tables(rows):
    row = jnp.repeat(jnp.arange(rows), GRID_W)
    col = jnp.tile(jnp.arange(GRID_W), rows)
    inv_freq = ROPE_THETA ** (-jnp.arange(ROPE_PAIRS, dtype=F32) / ROPE_PAIRS)
    ang = jnp.stack([row, col], axis=-1).astype(F32)[..., None] * inv_freq
    return jnp.cos(ang), jnp.sin(ang)


def _rope_2d(x, cos, sin):
    shp = x.shape
    xr = x.reshape(*shp[:-1], 2, 2, ROPE_PAIRS)
    x1, x2 = xr[..., 0, :], xr[..., 1, :]
    bshape = (shp[1],) + (1,) * (x.ndim - 3) + (2, ROPE_PAIRS)
    c, s = cos.reshape(bshape), sin.reshape(bshape)
    out = jnp.stack([x1 * c - x2 * s, x2 * c + x1 * s], axis=-2)
    return out.reshape(shp)


def _attn_specs(tq, tk):
    q_spec = pl.BlockSpec((None, None, ATT_GROUP, tq, ATT_DH), lambda b, h, i: (b, h, 0, i, 0))
    kv_spec = pl.BlockSpec((None, None, tk, ATT_DH), lambda b, h, i: (b, h, 0, 0))
    return q_spec, kv_spec


def _softmax_rows(q, k):
    s = lax.dot_general(q.astype(BF16), k.astype(BF16), (((1,), (1,)), ((), ())), preferred_element_type=F32)
    s = s * (ATT_DH ** -0.5)
    e = jnp.exp(s - jnp.max(s, axis=-1, keepdims=True))
    return e / jnp.sum(e, axis=-1, keepdims=True)


def _attn_fwd_call(q, k, v):
    bn, _, _, t_q, _ = q.shape
    t_k = k.shape[2]
    tq = _pick(t_q, (Q_BLOCK, 64, 32, 16, 8))
    q_spec, kv_spec = _attn_specs(tq, t_k)

    def body(q_ref, k_ref, v_ref, o_ref):
        p = _softmax_rows(q_ref[...].reshape(ATT_GROUP * tq, ATT_DH), k_ref[...])
        o = lax.dot_general(p.astype(BF16), v_ref[...].astype(BF16), (((1,), (0,)), ((), ())), preferred_element_type=F32)
        o_ref[...] = o.reshape(ATT_GROUP, tq, ATT_DH)

    return pl.pallas_call(
        body, name=f"attention_fwd_{t_q}x{t_k}", grid=(bn, ATT_KV_HEADS, t_q // tq),
        in_specs=[q_spec, kv_spec, kv_spec], out_specs=q_spec,
        out_shape=jax.ShapeDtypeStruct(q.shape, F32),
        compiler_params=pltpu.CompilerParams(dimension_semantics=("parallel", "parallel", "parallel"),
                                             vmem_limit_bytes=VMEM_LIMIT),
    )(q, k, v)


def _attn_bwd_call(q, k, v, do):
    bn, _, _, t_q, _ = q.shape
    t_k = k.shape[2]
    tq = _pick(t_q, (Q_BLOCK, 64, 32, 16, 8))
    q_spec, kv_spec = _attn_specs(tq, t_k)

    def body(q_ref, k_ref, v_ref, do_ref, dq_ref, dk_ref, dv_ref):
        @pl.when(pl.program_id(2) == 0)
        def _():
            dk_ref[...] = jnp.zeros_like(dk_ref)
            dv_ref[...] = jnp.zeros_like(dv_ref)

        qv = q_ref[...].reshape(ATT_GROUP * tq, ATT_DH)
        dov = do_ref[...].reshape(ATT_GROUP * tq, ATT_DH).astype(BF16)
        kb, vb = k_ref[...].astype(BF16), v_ref[...].astype(BF16)
        p = _softmax_rows(qv, k_ref[...])
        dp = lax.dot_general(dov, vb, (((1,), (1,)), ((), ())), preferred_element_type=F32)
        ds = p * (dp - jnp.sum(p * dp, axis=-1, keepdims=True)) * (ATT_DH ** -0.5)
        dsb = ds.astype(BF16)
        dq = lax.dot_general(dsb, kb, (((1,), (0,)), ((), ())), preferred_element_type=F32)
        dq_ref[...] = dq.reshape(ATT_GROUP, tq, ATT_DH)
        dk_ref[...] += lax.dot_general(dsb, qv.astype(BF16), (((0,), (0,)), ((), ())), preferred_element_type=F32)
        dv_ref[...] += lax.dot_general(p.astype(BF16), dov, (((0,), (0,)), ((), ())), preferred_element_type=F32)

    return pl.pallas_call(
        body, name=f"attention_bwd_{t_q}x{t_k}", grid=(bn, ATT_KV_HEADS, t_q // tq),
        in_specs=[q_spec, kv_spec, kv_spec, q_spec], out_specs=[q_spec, kv_spec, kv_spec],
        out_shape=[jax.ShapeDtypeStruct(t.shape, F32) for t in (q, k, v)],
        compiler_params=pltpu.CompilerParams(dimension_semantics=("parallel", "parallel", "arbitrary"),
                                             vmem_limit_bytes=VMEM_LIMIT),
    )(q, k, v, do)


@jax.custom_vjp
def _attn_core(q, k, v):
    return _attn_fwd_call(q, k, v)


def _attn_core_fwd(q, k, v):
    return _attn_fwd_call(q, k, v), (q, k, v)


def _attn_core_bwd(res, do):
    return tuple(_attn_bwd_call(*res, do))


_attn_core.defvjp(_attn_core_fwd, _attn_core_bwd)


def _attention_group(q, k, v, q_c, k_c, v_c, q_norm_w, k_norm_w, cos, sin, with_ctx_out):
    Bn, T, _ = q.shape
    Tc = q_c.shape[1]
    q = _rope_2d(_rms_norm(q.reshape(Bn, T, ATT_KV_HEADS, ATT_GROUP, ATT_DH), q_norm_w), cos, sin)
    k = _rope_2d(_rms_norm(k.reshape(Bn, T, ATT_KV_HEADS, ATT_DH), k_norm_w), cos, sin)
    v = v.reshape(Bn, T, ATT_KV_HEADS, ATT_DH)
    qc = _rms_norm(q_c.reshape(Bn, Tc, ATT_KV_HEADS, ATT_GROUP, ATT_DH), q_norm_w)
    kc = _rms_norm(k_c.reshape(Bn, Tc, ATT_KV_HEADS, ATT_DH), k_norm_w)
    vc = v_c.reshape(Bn, Tc, ATT_KV_HEADS, ATT_DH)
    keys = jnp.concatenate([kc, k], axis=1).transpose(0, 2, 1, 3)
    vals = jnp.concatenate([vc, v], axis=1).transpose(0, 2, 1, 3)
    o = _attn_core(q.transpose(0, 2, 3, 1, 4), keys, vals)
    o = o.transpose(0, 3, 1, 2, 4).reshape(Bn, T, ATT_W)
    if not with_ctx_out:
        return o, None
    o_c = _attn_core(qc.transpose(0, 2, 3, 1, 4), kc.transpose(0, 2, 1, 3), vc.transpose(0, 2, 1, 3))
    return o, o_c.transpose(0, 3, 1, 2, 4).reshape(Bn, Tc, ATT_W)


def _s5_operators(lam_re, lam_im, log_dt, b_re, b_im, c_re, c_im):
    lc = S5_LC
    dt = jnp.exp(log_dt)[..., None]
    ar, ai = lam_re * dt, lam_im * dt
    mag = jnp.exp(ar)
    lbr, lbi = mag * jnp.cos(ai), mag * jnp.sin(ai)
    den = lam_re * lam_re + lam_im * lam_im
    fr = ((lbr - 1.0) * lam_re + lbi * lam_im) / den
    fi = (lbi * lam_re - (lbr - 1.0) * lam_im) / den
    bbr = fr[..., None] * b_re - fi[..., None] * b_im
    bbi = fr[..., None] * b_im + fi[..., None] * b_re
    m = jnp.arange(lc + 1, dtype=F32)[:, None, None, None]
    pmag = jnp.exp(m * ar)
    pr, pi = pmag * jnp.cos(m * ai), pmag * jnp.sin(m * ai)
    cpr = c_re[None] * pr[:, :, :, None, :] - c_im[None] * pi[:, :, :, None, :]
    cpi = c_re[None] * pi[:, :, :, None, :] + c_im[None] * pr[:, :, :, None, :]
    kern = (jnp.einsum('mdghp,dgpk->mdghk', cpr[:lc], bbr, precision=HI)
            - jnp.einsum('mdghp,dgpk->mdghk', cpi[:lc], bbi, precision=HI))
    tail = kern.shape[1:]
    lags = jnp.concatenate([jnp.zeros((lc - 1,) + tail, F32), kern, jnp.zeros((1,) + tail, F32)], axis=0)
    toep = jnp.tile(lags, (lc,) + (1,) * len(tail))[:lc * (2 * lc - 1)].reshape((lc, 2 * lc - 1) + tail)[:, lc - 1:]
    tm = toep.transpose(2, 3, 0, 5, 1, 4).reshape(2, S5_GROUPS, S5_CW, S5_CW)
    prr, pir = pr[lc - 1::-1], pi[lc - 1::-1]
    mre = prr[..., None] * bbr[None] - pir[..., None] * bbi[None]
    mim = prr[..., None] * bbi[None] + pir[..., None] * bbr[None]
    mm = jnp.concatenate([mre, mim], axis=3).transpose(1, 2, 0, 4, 3).reshape(2, S5_GROUPS, S5_CW, S5_SW)
    nm = jnp.concatenate([cpr[1:], -cpi[1:]], axis=-1)
    nm = nm.transpose(1, 2, 4, 0, 3).reshape(2, S5_GROUPS, S5_SW, S5_CW)
    a1 = jnp.concatenate([pr[lc], pr[lc]], axis=-1)
    a2 = jnp.concatenate([-pi[lc], pi[lc]], axis=-1)
    lam_rows = jnp.concatenate([a1[:, :, None], a2[:, :, None],
                                jnp.zeros((2, S5_GROUPS, SUBLANES - 2, S5_SW), F32)], axis=2)
    return tm, mm, nm, lam_rows


def _dot_hi(a, b, dims):
    return lax.dot_general(a, b, (dims, ((), ())), precision=BF16X3, preferred_element_type=F32)


def _s5_blocks(rows):
    def blk(r, c):
        return pl.BlockSpec((None, None, r, c), lambda d, g: (d, g, 0, 0))
    return (blk(rows, S5_CW), blk(S5_CW, S5_CW), blk(S5_CW, S5_SW), blk(S5_SW, S5_CW), blk(SUBLANES, S5_SW),
            blk(rows, S5_SW))


def _s5_core_fwd_call(s, tm, mm, nm, lam_rows):
    rows = s.shape[2]
    chunks = rows // S5_ROWS
    seq, top, mop, nop, lop, sta = _s5_blocks(rows)

    def body(s_ref, t_ref, m_ref, n_ref, l_ref, y_ref, h_ref, e_ref):
        sv = s_ref[...]
        e_ref[...] = _dot_hi(sv, m_ref[...], ((1,), (0,)))
        a1, a2 = l_ref[0:1, :], l_ref[1:2, :]
        h_ref[0:S5_ROWS, :] = jnp.zeros((S5_ROWS, S5_SW), F32)

        def step(k, carry):
            at = pl.multiple_of((k - 1) * S5_ROWS, S5_ROWS)
            prev = h_ref[pl.ds(at, S5_ROWS), :]
            new = a1 * prev + a2 * pltpu.roll(prev, S5_P, 1) + e_ref[pl.ds(at, S5_ROWS), :]
            h_ref[pl.ds(pl.multiple_of(k * S5_ROWS, S5_ROWS), S5_ROWS), :] = new
            return carry

        lax.fori_loop(1, chunks, step, 0)
        y_ref[...] = _dot_hi(sv, t_ref[...], ((1,), (0,))) + _dot_hi(h_ref[...], n_ref[...], ((1,), (0,)))

    return pl.pallas_call(
        body, name="s5_chunks_fwd", grid=(2, S5_GROUPS),
        in_specs=[seq, top, mop, nop, lop], out_specs=[seq, sta],
        out_shape=[jax.ShapeDtypeStruct(s.shape, F32), jax.ShapeDtypeStruct(s.shape[:3] + (S5_SW,), F32)],
        scratch_shapes=[pltpu.VMEM((rows, S5_SW), F32)],
        compiler_params=pltpu.CompilerParams(dimension_semantics=("parallel", "parallel"), vmem_limit_bytes=VMEM_LIMIT),
    )(s, tm, mm, nm, lam_rows)


def _s5_core_bwd_call(s, dy, tm, mm, nm, lam_rows, hin):
    rows = s.shape[2]
    chunks = rows // S5_ROWS
    seq, top, mop, nop, lop, sta = _s5_blocks(rows)

    def body(s_ref, dy_ref, t_ref, m_ref, n_ref, l_ref, h_ref, ds_ref, dt_ref, dm_ref, dn_ref, dl_ref, dh_ref, de_ref):
        sv, dyv, hv = s_ref[...], dy_ref[...], h_ref[...]
        dh_ref[...] = _dot_hi(dyv, n_ref[...], ((1,), (1,)))
        a1, a2 = l_ref[0:1, :], l_ref[1:2, :]
        last = (chunks - 1) * S5_ROWS
        de_ref[last:last + S5_ROWS, :] = jnp.zeros((S5_ROWS, S5_SW), F32)

        def step(i, g):
            k = chunks - 2 - i
            at = pl.multiple_of(k * S5_ROWS, S5_ROWS)
            de_ref[pl.ds(at, S5_ROWS), :] = g
            return dh_ref[pl.ds(at, S5_ROWS), :] + a1 * g + pltpu.roll(a2 * g, S5_P, 1)

        lax.fori_loop(0, chunks - 1, step, dh_ref[last:last + S5_ROWS, :])
        dev = de_ref[...]
        ds_ref[...] = _dot_hi(dyv, t_ref[...], ((1,), (1,))) + _dot_hi(dev, m_ref[...], ((1,), (1,)))
        dt_ref[...] = _dot_hi(sv, dyv, ((0,), (0,)))
        dm_ref[...] = _dot_hi(sv, dev, ((0,), (0,)))
        dn_ref[...] = _dot_hi(hv, dyv, ((0,), (0,)))
        da1 = jnp.sum(hv * dev, axis=0, keepdims=True)
        da2 = jnp.sum(pltpu.roll(hv, S5_P, 1) * dev, axis=0, keepdims=True)
        dl_ref[...] = jnp.concatenate([da1, da2, jnp.zeros((SUBLANES - 2, S5_SW), F32)], axis=0)

    return pl.pallas_call(
        body, name="s5_chunks_bwd", grid=(2, S5_GROUPS),
        in_specs=[seq, seq, top, mop, nop, lop, sta], out_specs=[seq, top, mop, nop, lop],
        out_shape=[jax.ShapeDtypeStruct(t.shape, F32) for t in (s, tm, mm, nm, lam_rows)],
        scratch_shapes=[pltpu.VMEM((rows, S5_SW), F32), pltpu.VMEM((rows, S5_SW), F32)],
        compiler_params=pltpu.CompilerParams(dimension_semantics=("parallel", "parallel"), vmem_limit_bytes=VMEM_LIMIT),
    )(s, dy, tm, mm, nm, lam_rows, hin)


@jax.custom_vjp
def _s5_core(s, tm, mm, nm, lam_rows):
    return _s5_core_fwd_call(s, tm, mm, nm, lam_rows)[0]


def _s5_core_fwd(s, tm, mm, nm, lam_rows):
    y, hin = _s5_core_fwd_call(s, tm, mm, nm, lam_rows)
    return y, (s, tm, mm, nm, lam_rows, hin)


def _s5_core_bwd(res, dy):
    s, tm, mm, nm, lam_rows, hin = res
    return tuple(_s5_core_bwd_call(s, dy, tm, mm, nm, lam_rows, hin))


_s5_core.defvjp(_s5_core_fwd, _s5_core_bwd)


def _s5_group(u, u_c, lam_re, lam_im, log_dt, b_re, b_im, c_re, c_im, d_skip, glu_w, glu_b, with_ctx_out):
    bn, t_lat, _ = u.shape
    t_ctx = u_c.shape[1]
    n = t_ctx + t_lat
    chunks = n // S5_LC
    assert bn <= S5_ROWS and t_ctx % S5_LC == 0 and t_lat % S5_LC == 0
    seqs = jnp.stack([jnp.concatenate([u_c, u], axis=1),
                      jnp.concatenate([jnp.flip(u_c, 1), jnp.flip(u, 1)], axis=1)])
    s = seqs.reshape(2, bn, chunks, S5_LC, S5_GROUPS, S5_GH).transpose(0, 4, 2, 1, 3, 5)
    s = jnp.pad(s, ((0, 0), (0, 0), (0, 0), (0, S5_ROWS - bn), (0, 0), (0, 0)))
    s = s.reshape(2, S5_GROUPS, chunks * S5_ROWS, S5_CW)
    y = _s5_core(s, *_s5_operators(lam_re, lam_im, log_dt, b_re, b_im, c_re, c_im))
    y = y.reshape(2, S5_GROUPS, chunks, S5_ROWS, S5_LC, S5_GH)[:, :, :, :bn]
    y = y.transpose(0, 3, 2, 4, 1, 5).reshape(2, bn, n, S5_W)
    yl = d_skip * u + y[0, :, t_ctx:] + jnp.flip(y[1, :, t_ctx:], 1)

    def glu(yy):
        zz = jax.nn.gelu(yy)
        return zz * jax.nn.sigmoid(zz @ glu_w + glu_b)

    if not with_ctx_out:
        return glu(yl), None
    yc = d_skip * u_c + y[0, :, :t_ctx] + jnp.flip(y[1, :, :t_ctx], 1)
    return glu(yl), glu(yc)


def _s5_wide_operators(lam_re, lam_im, log_dt, b_re, b_im, c_re, c_im):
    lc, g4, nd = S5W_LC, S5W_TILE // S5_SW, lam_re.shape[0]
    dt = jnp.exp(log_dt)[..., None]
    ar, ai = lam_re * dt, lam_im * dt
    mag = jnp.exp(ar)
    lbr, lbi = mag * jnp.cos(ai), mag * jnp.sin(ai)
    den = lam_re * lam_re + lam_im * lam_im
    fr = ((lbr - 1.0) * lam_re + lbi * lam_im) / den
    fi = (lbi * lam_re - (lbr - 1.0) * lam_im) / den
    bbr = fr[..., None] * b_re - fi[..., None] * b_im
    bbi = fr[..., None] * b_im + fi[..., None] * b_re
    m = jnp.arange(lc + 1, dtype=F32)[:, None, None, None]
    pmag = jnp.exp(m * ar)
    pr, pi = pmag * jnp.cos(m * ai), pmag * jnp.sin(m * ai)
    cpr = c_re[None] * pr[:, :, :, None, :] - c_im[None] * pi[:, :, :, None, :]
    cpi = c_re[None] * pi[:, :, :, None, :] + c_im[None] * pr[:, :, :, None, :]
    kern = (jnp.einsum('mdghp,dgpk->mdghk', cpr[:lc], bbr, precision=HI)
            - jnp.einsum('mdghp,dgpk->mdghk', cpi[:lc], bbi, precision=HI))
    lags = jnp.concatenate([jnp.zeros((lc - 1,) + kern.shape[1:], F32), kern], axis=0)
    toep = jnp.stack([lags[lc - 1 - j:2 * lc - 1 - j] for j in range(lc)])
    eye = jnp.eye(S5_GROUPS, dtype=F32)
    tm = jnp.einsum('jtdghk,gn->djgktnh', toep, eye).reshape(nd, S5W_RW, S5W_RW)
    back = (lc - 1.0 - jnp.arange(lc, dtype=F32))[:, None, None, None]
    bmag = jnp.exp(back * ar)
    prr, pir = bmag * jnp.cos(back * ai), bmag * jnp.sin(back * ai)
    left = jnp.stack([prr[..., None] * bbr[None] - pir[..., None] * bbi[None],
                      prr[..., None] * bbi[None] + pir[..., None] * bbr[None]])
    mm = jnp.einsum('ajdgpk,gn->djgknap', left, eye)
    mm = mm.reshape(nd, lc, S5_GROUPS, S5_GH, S5_GROUPS // g4, g4, 2, S5_P).transpose(0, 1, 2, 3, 4, 6, 5, 7)
    mm = mm.reshape(nd, S5W_RW, S5W_SW)
    right = jnp.stack([cpr[1:], -cpi[1:]])
    nm = jnp.einsum('atdghp,gn->dnaptgh', right, eye)
    nm = nm.reshape(nd, S5_GROUPS // g4, g4, 2, S5_P, lc, S5_GROUPS, S5_GH).transpose(0, 1, 3, 2, 4, 5, 6, 7)
    nm = nm.reshape(nd, S5W_SW, S5W_RW)

    def state_cols(re_part, im_part):
        t = jnp.stack([re_part, im_part], axis=1).reshape(nd, 2, S5_GROUPS // g4, g4, S5_P)
        return t.transpose(0, 2, 1, 3, 4).reshape(nd, S5W_SW)

    a1, a2 = state_cols(pr[lc], pr[lc]), state_cols(-pi[lc], pi[lc])
    lam_rows = jnp.concatenate([a1[:, None], a2[:, None], jnp.zeros((nd, SUBLANES - 2, S5W_SW), F32)], axis=1)
    return tm, mm, nm, lam_rows


def _s5_carry_specs(rows):
    wide = pl.BlockSpec((rows, S5W_TILE), lambda j: (0, j))
    lam = pl.BlockSpec((SUBLANES, S5W_TILE), lambda j: (0, j))
    return wide, lam


def _s5_carry_fwd(e, lam_rows):
    rows = e.shape[0]
    steps = rows // S5_ROWS
    wide, lam = _s5_carry_specs(rows)

    def body(e_ref, l_ref, h_ref):
        a1, a2 = l_ref[0:1, :], l_ref[1:2, :]
        h_ref[0:S5_ROWS, :] = jnp.zeros((S5_ROWS, S5W_TILE), F32)

        def step(k, carry):
            at = pl.multiple_of((k - 1) * S5_ROWS, S5_ROWS)
            prev = h_ref[pl.ds(at, S5_ROWS), :]
            new = a1 * prev + a2 * pltpu.roll(prev, S5W_TILE // 2, 1) + e_ref[pl.ds(at, S5_ROWS), :]
            h_ref[pl.ds(pl.multiple_of(k * S5_ROWS, S5_ROWS), S5_ROWS), :] = new
            return carry

        lax.fori_loop(1, steps, step, 0)

    return pl.pallas_call(
        body, name="s5_carry_fwd", grid=(S5W_SW // S5W_TILE,), in_specs=[wide, lam], out_specs=wide,
        out_shape=jax.ShapeDtypeStruct(e.shape, F32),
        compiler_params=pltpu.CompilerParams(dimension_semantics=("parallel",), vmem_limit_bytes=VMEM_LIMIT),
    )(e, lam_rows)


def _s5_carry_bwd(dh, hin, lam_rows):
    rows = dh.shape[0]
    steps = rows // S5_ROWS
    wide, lam = _s5_carry_specs(rows)

    def body(dh_ref, h_ref, l_ref, de_ref, dl_ref):
        a1, a2 = l_ref[0:1, :], l_ref[1:2, :]
        last = (steps - 1) * S5_ROWS
        de_ref[last:last + S5_ROWS, :] = jnp.zeros((S5_ROWS, S5W_TILE), F32)

        def step(i, g):
            at = pl.multiple_of((steps - 2 - i) * S5_ROWS, S5_ROWS)
            de_ref[pl.ds(at, S5_ROWS), :] = g
            return dh_ref[pl.ds(at, S5_ROWS), :] + a1 * g + pltpu.roll(a2 * g, S5W_TILE // 2, 1)

        lax.fori_loop(0, steps - 1, step, dh_ref[last:last + S5_ROWS, :])
        hv, dev = h_ref[...], de_ref[...]
        da1 = jnp.sum(hv * dev, axis=0, keepdims=True)
        da2 = jnp.sum(pltpu.roll(hv, S5W_TILE // 2, 1) * dev, axis=0, keepdims=True)
        dl_ref[...] = jnp.concatenate([da1, da2, jnp.zeros((SUBLANES - 2, S5W_TILE), F32)], axis=0)

    return pl.pallas_call(
        body, name="s5_carry_bwd", grid=(S5W_SW // S5W_TILE,), in_specs=[wide, wide, lam], out_specs=[wide, lam],
        out_shape=[jax.ShapeDtypeStruct(dh.shape, F32), jax.ShapeDtypeStruct(lam_rows.shape, F32)],
        compiler_params=pltpu.CompilerParams(dimension_semantics=("parallel",), vmem_limit_bytes=VMEM_LIMIT),
    )(dh, hin, lam_rows)


@jax.custom_vjp
def _s5_scan(s, tm, mm, nm, lam_rows):
    hin = _s5_carry_fwd(_mm(s, mm), lam_rows)
    return _mm(hin, nm, init=_mm(s, tm))


def _s5_scan_fwd(s, tm, mm, nm, lam_rows):
    hin = _s5_carry_fwd(_mm(s, mm), lam_rows)
    return _mm(hin, nm, init=_mm(s, tm)), (s, tm, mm, nm, lam_rows, hin)


def _s5_scan_bwd(res, dy):
    s, tm, mm, nm, lam_rows, hin = res
    de, dlam = _s5_carry_bwd(_mm(dy, nm, 'nt'), hin, lam_rows)
    ds = _mm(de, mm, 'nt', init=_mm(dy, tm, 'nt'))
    return ds, _mm(s, dy, 'tn'), _mm(s, de, 'tn'), _mm(hin, dy, 'tn'), dlam


_s5_scan.defvjp(_s5_scan_fwd, _s5_scan_bwd)


def _s5_wide_group(u, u_c, lam_re, lam_im, log_dt, b_re, b_im, c_re, c_im, d_skip, glu_w, glu_b):
    bn, t_lat, _ = u.shape
    t_ctx = u_c.shape[1]
    n = t_ctx + t_lat
    steps = n // S5W_LC
    assert bn <= S5_ROWS and t_ctx % S5W_LC == 0 and t_lat % S5W_LC == 0
    seqs = [jnp.concatenate([u_c, u], axis=1), jnp.concatenate([jnp.flip(u_c, 1), jnp.flip(u, 1)], axis=1)]
    ys = []
    for d, seq in enumerate(seqs):
        ops = _s5_wide_operators(*[t[d:d + 1] for t in (lam_re, lam_im, log_dt, b_re, b_im, c_re, c_im)])
        s = jnp.pad(seq.reshape(bn, steps, S5W_RW).transpose(1, 0, 2), ((0, 0), (0, S5_ROWS - bn), (0, 0)))
        y = _s5_scan(s.reshape(steps * S5_ROWS, S5W_RW), *[op[0] for op in ops])
        ys.append(y.reshape(steps, S5_ROWS, S5W_RW)[:, :bn].transpose(1, 0, 2).reshape(bn, n, S5_W))

    def glu(yy):
        zz = jax.nn.gelu(yy)
        return zz * jax.nn.sigmoid(zz @ glu_w + glu_b)

    yl = d_skip * u + ys[0][:, t_ctx:] + jnp.flip(ys[1][:, t_ctx:], 1)
    yc = d_skip * u_c + ys[0][:, :t_ctx] + jnp.flip(ys[1][:, :t_ctx], 1)
    return glu(yl), glu(yc)


def _heads(proj, proj_c, hp, cos, sin, with_ctx_out):
    g_qkv, g_z, g_b, g_a, a_q, a_k, a_v, s_u = jnp.split(proj[..., :IN_COLS], list(IN_CUTS), axis=-1)
    c_qkv, c_z, c_b, c_a, c_q, c_k, c_v, c_u = jnp.split(proj_c[..., :IN_COLS], list(IN_CUTS), axis=-1)
    o_gdn, oc_gdn = _gdn_group(g_qkv, g_z, g_b, g_a, c_qkv, c_z, c_b, c_a, hp['gdn_conv_w'], hp['gdn_a_log'],
                               hp['gdn_dt_bias'], hp['gdn_norm_w'], with_ctx_out)
    o_att, oc_att = _attention_group(a_q, a_k, a_v, c_q, c_k, c_v, hp['q_norm_w'], hp['k_norm_w'], cos, sin,
                                     with_ctx_out)
    o_s5, oc_s5 = _s5_wide_group(s_u, c_u, hp['s5_lam_re'], hp['s5_lam_im'], hp['s5_log_dt'], hp['s5_b_re'],
                                 hp['s5_b_im'], hp['s5_c_re'], hp['s5_c_im'], hp['s5_d'], hp['glu_w'], hp['glu_b'])
    o = jnp.concatenate([o_gdn, o_att, o_s5], axis=-1)
    if not with_ctx_out:
        return (o,)
    return o, jnp.concatenate([oc_gdn, oc_att, oc_s5], axis=-1)


def _mixer_fwd(st, x, mod, t_ctx, w_in, w_out, lg, lb, hp, cos, sin, alpha):
    rows = st.bn * st.n
    h = _modulate(st, x, mod, 1).reshape(rows, st.d)
    proj = _mm(h, w_in).reshape(st.bn, st.n, IN_COLS_PAD)
    (o_lat, o_ctx), heads_vjp = jax.vjp(lambda p, pc, hp_: _heads(p, pc, hp_, cos, sin, True),
                                         proj[:, t_ctx:], proj[:, :t_ctx], hp)
    o = jnp.concatenate([o_ctx, o_lat], axis=1).reshape(rows, st.d)
    y = _mm(o, w_out).reshape(st.bn, st.n, st.d)
    return _post_norm(st, x, y, mod, 1, lg, lb, 1.0, alpha), (h, heads_vjp, o, y)


def _mixer_bwd(st, x, mod, t_ctx, w_in, w_out, lg, alpha, kept, dout):
    h, heads_vjp, o, y = kept
    rows = st.bn * st.n
    dxr, dy, dgate, dlg, dlb = _post_norm_bwd(st, x, y, dout, mod, 1, lg, 1.0, alpha)
    dy = dy.reshape(rows, st.d)
    dw_out = _mm(o, dy, 'tn')
    do = _mm(dy, w_out, 'nt').reshape(st.bn, st.n, st.d)
    dproj, dproj_c, dhp = heads_vjp((do[:, t_ctx:], do[:, :t_ctx]))
    dp = jnp.concatenate([dproj_c, dproj], axis=1).reshape(rows, IN_COLS_PAD)
    dw_in = _mm(h, dp, 'tn')
    dh = _mm(dp, w_in, 'nt').reshape(st.bn, st.n, st.d)
    dx, dsh, dsc = _modulate_bwd(st, dh, x, dxr, mod, 1)
    return dx, (dsh, dsc, dgate), dw_in, dw_out, jnp.sum(dlg, axis=(0, 1)), jnp.sum(dlb, axis=(0, 1)), dhp


def _natural(name, gathered):
    ax = SHARD_AXIS[name]
    t = jnp.moveaxis(gathered, 0, ax)
    return t.reshape(t.shape[:ax] + (t.shape[ax] * t.shape[ax + 1],) + t.shape[ax + 2:])


def _to_shards(name, full):
    ax = SHARD_AXIS[name]
    t = full.reshape(full.shape[:ax] + (N_CHIPS, full.shape[ax] // N_CHIPS) + full.shape[ax + 1:])
    return jnp.moveaxis(t, ax, 0)


def _pack(arrays):
    flat = jnp.concatenate([a.reshape(-1) for a in arrays])
    pad = (-flat.size) % (SUBLANES * LANES)
    return jnp.pad(flat, (0, pad)).reshape(-1, LANES)


def _unpack(packed, like):
    flat = packed.reshape(-1)
    out, at = [], 0
    for a in like:
        out.append(flat[at:at + a.size].reshape(a.shape))
        at += a.size
    return out


def kernel(x, c, ctx, c_ctx, w_ada, b_ada, ln_g, ln_b, ffn_w1, ffn_w3, ffn_w2, w_in, w_out, gdn_conv_w, gdn_a_log, gdn_dt_bias, gdn_norm_w, q_norm_w, k_norm_w, s5_lam_re, s5_lam_im, s5_log_dt, s5_b_re, s5_b_im, s5_c_re, s5_c_im, s5_d, glu_w, glu_b, loss_target, m_c_ctx, m_w_ada, m_b_ada, m_ln_g, m_ln_b, m_ffn_w1, m_ffn_w3, m_ffn_w2, m_w_in, m_w_out, m_gdn_conv_w, m_gdn_a_log, m_gdn_dt_bias, m_gdn_norm_w, m_q_norm_w, m_k_norm_w, m_s5_lam_re, m_s5_lam_im, m_s5_log_dt, m_s5_b_re, m_s5_b_im, m_s5_c_re, m_s5_c_im, m_s5_d, m_glu_w, m_glu_b, v_c_ctx, v_w_ada, v_b_ada, v_ln_g, v_ln_b, v_ffn_w1, v_ffn_w3, v_ffn_w2, v_w_in, v_w_out, v_gdn_conv_w, v_gdn_a_log, v_gdn_dt_bias, v_gdn_norm_w, v_q_norm_w, v_k_norm_w, v_s5_lam_re, v_s5_lam_im, v_s5_log_dt, v_s5_b_re, v_s5_b_im, v_s5_c_re, v_s5_c_im, v_s5_d, v_glu_w, v_glu_b):
    given = dict(locals())
    w = {n: given[n] for n in WEIGHTS}
    mom = {n: given['m_' + n] for n in WEIGHTS}
    var = {n: given['v_' + n] for n in WEIGHTS}
    depth = w_ada.shape[0]
    bn, t_lat, d = x.shape
    alpha = (2.0 * depth) ** 0.25
    chip = 2 * lax.axis_index("x") + lax.axis_index("y")

    gathered = _gather_weight_shards([w[n].astype(BF16) for n in BIG])
    full = {n: _natural(n, g) for n, g in zip(BIG, gathered)}
    full['w_in'] = jnp.pad(full['w_in'], ((0, 0), (0, 0), (0, IN_COLS_PAD - IN_COLS)))
    small_sh = [w[n] for n in SMALL_SHARDED]
    small_all = _all_gather8(_pack(small_sh))
    for n, parts in zip(SMALL_SHARDED, zip(*[_unpack(small_all[2 * j], small_sh) for j in range(N_CHIPS)])):
        full[n] = _natural(n, jnp.stack(parts))
    for n in REPLICATED:
        full[n] = w[n]

    cos, sin = _axial_rope_tables(t_lat // GRID_W)
    act = jnp.zeros((ADA_ROWS, d), F32).at[:bn].set(jax.nn.silu(c)).at[bn].set(jax.nn.silu(c_ctx))
    t_ctx = ctx.shape[1]
    st = _Stream(bn, t_ctx, t_lat, d)
    xs = jnp.concatenate([ctx, x], axis=1)
    saved = []
    for l in range(depth):
        mod = (_mm(act, full['w_ada'][l]) + full['b_ada'][l]).reshape(ADA_ROWS, N_MOD, d)
        hp = {n: full[n][l] for n in HEAD_PARAMS}
        lg, lb = full['ln_g'][l], full['ln_b'][l]
        f1 = (full['ffn_w1'][l], full['ffn_w3'][l], full['ffn_w2'][l])
        x0 = xs
        x1, kept1 = _ffn_fwd(st, x0, mod, 0, f1[0][0], f1[1][0], f1[2][0], lg[0], lb[0], alpha)
        x2, kept2 = _mixer_fwd(st, x1, mod, t_ctx, full['w_in'][l], full['w_out'][l], lg[1], lb[1], hp, cos, sin, alpha)
        xs, kept3 = _ffn_fwd(st, x2, mod, 2, f1[0][1], f1[1][1], f1[2][1], lg[2], lb[2], alpha)
        saved.append((x0, x1, x2, mod, kept1, kept2, kept3))

    err = xs[:, t_ctx:] - loss_target
    loss = lax.psum(0.5 * jnp.sum(jnp.mean(err * err, axis=-1)), ("x", "y", "c"))
    dxs = jnp.concatenate([jnp.zeros((bn, t_ctx, d), F32), err / d], axis=1)

    grads = {n: [None] * depth for n in WEIGHTS if n != 'c_ctx'}
    dact = jnp.zeros((ADA_ROWS, d), F32)
    for l in reversed(range(depth)):
        x0, x1, x2, mod, kept1, kept2, kept3 = saved[l]
        lg = full['ln_g'][l]
        f1 = (full['ffn_w1'][l], full['ffn_w3'][l], full['ffn_w2'][l])
        dx2, dm3, dw1b, dw3b, dw2b, dlg2, dlb2 = _ffn_bwd(st, x2, mod, 2, f1[0][1], f1[1][1], f1[2][1], lg[2], alpha,
                                                           kept3, dxs)
        dx1, dm2, dw_in, dw_out, dlg1, dlb1, dhp = _mixer_bwd(st, x1, mod, t_ctx, full['w_in'][l], full['w_out'][l],
                                                              lg[1], alpha, kept2, dx2)
        dxs, dm1, dw1a, dw3a, dw2a, dlg0, dlb0 = _ffn_bwd(st, x0, mod, 0, f1[0][0], f1[1][0], f1[2][0], lg[0], alpha,
                                                          kept1, dx1)
        parts = jnp.stack([t[:, :, 0] for grp in (dm1, dm2, dm3) for t in grp], axis=2)
        dmod = jnp.concatenate([parts[:, 1], jnp.sum(parts[:, 0], axis=0, keepdims=True),
                                jnp.zeros((ADA_ROWS - bn - 1, N_MOD, d), F32)], axis=0).reshape(ADA_ROWS, N_MOD * d)
        grads['w_ada'][l] = _mm(act, dmod, 'tn')
        grads['b_ada'][l] = jnp.sum(dmod, axis=0)
        dact = dact + _mm(dmod, full['w_ada'][l], 'nt')
        grads['ffn_w1'][l] = jnp.stack([dw1a, dw1b])
        grads['ffn_w3'][l] = jnp.stack([dw3a, dw3b])
        grads['ffn_w2'][l] = jnp.stack([dw2a, dw2b])
        grads['w_in'][l] = dw_in[:, :IN_COLS]
        grads['w_out'][l] = dw_out
        grads['ln_g'][l] = jnp.stack([dlg0, dlg1, dlg2])
        grads['ln_b'][l] = jnp.stack([dlb0, dlb1, dlb2])
        for n in HEAD_PARAMS:
            grads[n][l] = dhp[n]
    grad = {n: jnp.stack(g) for n, g in grads.items()}
    sig = jax.nn.sigmoid(c_ctx)
    grad['c_ctx'] = dact[bn] * (sig * (1.0 + c_ctx * (1.0 - sig)))

    half = depth // 2
    laid = []
    for n in BIG:
        s = _to_shards(n, grad[n])
        s = s.reshape((N_CHIPS, 2, half) + s.shape[2:])
        laid.append(jnp.moveaxis(s, 1, 0))
    theirs = _swap_halves(laid)
    pair = [_add_sibling(g, r) for g, r in zip(laid, theirs)]
    landed = _scatter_partials([lo for _, lo in pair])
    mine = [_add_chips(p, r) for (p, _), r in zip(pair, landed)]
    reduced = dict(zip(BIG, _share_halves(mine)))

    small_names = REPLICATED + SMALL_SHARDED
    small_grads = [grad[n] for n in small_names]
    summed = _unpack(_sum_devices(_all_gather8(_pack(small_grads))), small_grads)
    for n, g in zip(small_names, summed):
        if n in SMALL_SHARDED:
            ax = SHARD_AXIS[n]
            width = g.shape[ax] // N_CHIPS
            g = lax.dynamic_slice_in_dim(g, chip * width, width, axis=ax)
        reduced[n] = g

    delta, new_m, new_v = {}, {}, {}
    for n in BIG:
        delta[n], new_m[n], new_v[n] = _adamw(w[n], reduced[n], mom[n], var[n])
    packs = [_pack([t[n] for n in small_names]) for t in (w, reduced, mom, var)]
    like = [w[n] for n in small_names]
    for res, packed in zip((delta, new_m, new_v), _adamw(*packs)):
        res.update(zip(small_names, _unpack(packed, like)))

    return (loss, dxs[:, t_ctx:], *[reduced[n] for n in WEIGHTS], *[delta[n] for n in WEIGHTS],
            *[new_m[n] for n in WEIGHTS], *[new_v[n] for n in WEIGHTS])
```

```python
import functools
import math

import jax
import jax.numpy as jnp
from jax import lax
from jax.experimental import pallas as pl
from jax.experimental.pallas import tpu as pltpu

F32 = jnp.float32
BF16 = jnp.bfloat16
MESH_IDS = pl.DeviceIdType.MESH
ANY = pl.BlockSpec(memory_space=pl.ANY)
N_CHIPS = 4
N_DEV = 8
LANES = 128
SUBLANES = 8
ADA_ROWS = 128
VMEM_LIMIT = 48 * 1024 * 1024

D_MODEL = 1024
GRID_W = 64
GDN_HEADS = 6
GDN_DK = 64
GDN_DV = 64
GDN_W = GDN_HEADS * GDN_DV
GDN_QKV = GDN_HEADS * (2 * GDN_DK + GDN_DV)
CONV_K = 5
CHUNK = 64
ATT_HEADS = 6
ATT_KV_HEADS = 2
ATT_DH = 64
ATT_W = ATT_HEADS * ATT_DH
ATT_GROUP = ATT_HEADS // ATT_KV_HEADS
Q_BLOCK = 128
ROPE_THETA = 10000.0
ROPE_PAIRS = ATT_DH // 4
S5_GROUPS = 16
S5_GH = 16
S5_P = 64
S5_W = S5_GROUPS * S5_GH
S5_LC = 32
S5_ROWS = SUBLANES
S5_CW = S5_LC * S5_GH
S5_SW = 2 * S5_P
S5W_LC = 8
S5W_RW = S5W_LC * S5_W
S5W_SW = S5_GROUPS * S5_SW
S5W_TILE = 4 * S5_SW
GDN_PREP_CHUNKS = 12
GDN_SCAN_CHAINS = 24
HI = lax.Precision.HIGHEST
BF16X3 = lax.Precision.HIGH
N_MOD = 9
EPS = 1e-6
OFF_GDN_Z = GDN_QKV
OFF_GDN_B = OFF_GDN_Z + GDN_W
OFF_GDN_A = OFF_GDN_B + 2 * GDN_HEADS
OFF_ATT_Q = OFF_GDN_A + 2 * GDN_HEADS
OFF_ATT_K = OFF_ATT_Q + ATT_W
OFF_ATT_V = OFF_ATT_K + ATT_KV_HEADS * ATT_DH
OFF_S5 = OFF_ATT_V + ATT_KV_HEADS * ATT_DH
IN_COLS = OFF_S5 + S5_W
IN_COLS_PAD = 2560
IN_CUTS = (OFF_GDN_Z, OFF_GDN_B, OFF_GDN_A, OFF_ATT_Q, OFF_ATT_K, OFF_ATT_V, OFF_S5)

ADAM_LR = 0.001
ADAM_B1 = 0.9
ADAM_B2 = 0.999
ADAM_EPS = 1e-08
ADAM_WD = 0.01
ADAM_STEP = 10

WEIGHTS = ['c_ctx', 'w_ada', 'b_ada', 'ln_g', 'ln_b', 'ffn_w1', 'ffn_w3', 'ffn_w2', 'w_in', 'w_out', 'gdn_conv_w',
           'gdn_a_log', 'gdn_dt_bias', 'gdn_norm_w', 'q_norm_w', 'k_norm_w', 's5_lam_re', 's5_lam_im', 's5_log_dt',
           's5_b_re', 's5_b_im', 's5_c_re', 's5_c_im', 's5_d', 'glu_w', 'glu_b']
BIG = ['w_ada', 'ffn_w1', 'ffn_w3', 'ffn_w2', 'w_in', 'w_out']
SMALL_SHARDED = ['ln_g', 'ln_b', 'gdn_conv_w', 'glu_w']
SHARD_AXIS = {'w_ada': 2, 'ffn_w1': 3, 'ffn_w3': 3, 'ffn_w2': 2, 'w_in': 2, 'w_out': 1,
              'ln_g': 2, 'ln_b': 2, 'gdn_conv_w': 2, 'glu_w': 1}
REPLICATED = [n for n in WEIGHTS if n not in BIG and n not in SMALL_SHARDED]
HEAD_PARAMS = ['gdn_conv_w', 'gdn_a_log', 'gdn_dt_bias', 'gdn_norm_w', 'q_norm_w', 'k_norm_w', 's5_lam_re',
               's5_lam_im', 's5_log_dt', 's5_b_re', 's5_b_im', 's5_c_re', 's5_c_im', 's5_d', 'glu_w', 'glu_b']


def _place():
    return lax.axis_index("x"), lax.axis_index("y"), lax.axis_index("c")


def _pick(n, cands):
    for t in cands:
        if n % t == 0:
            return t
    return n


def _remote(src, dst, send_sem, recv_sem, to):
    return pltpu.make_async_remote_copy(src_ref=src, dst_ref=dst, send_sem=send_sem, recv_sem=recv_sem,
                                        device_id=to, device_id_type=MESH_IDS)


def _gather_weight_shards(shards):
    n = len(shards)
    halves = [s.shape[0] // 2 for s in shards]

    def body(*refs):
        ins, outs = refs[:n], refs[n:2 * n]
        send_sems, recv_sems = refs[2 * n:]
        x, y, c = _place()
        chip = 2 * x + y
        sibling = (x, y, 1 - c)
        others = [(1 - x, y), (x, 1 - y), (1 - x, 1 - y)]
        sends = []
        for i in range(n):
            h = halves[i]
            for j, (px, py) in enumerate(others):
                k = 6 * i + j
                cp = _remote(ins[i].at[pl.ds(c * h, h)], outs[i].at[chip, pl.ds(c * h, h)],
                             send_sems.at[k], recv_sems.at[k], (px, py, c))
                cp.start()
                sends.append(cp)
        for i in range(n):
            h = halves[i]
            for j, (px, py) in enumerate(others):
                slab = outs[i].at[2 * px + py, pl.ds(c * h, h)]
                _remote(slab, slab, send_sems.at[6 * i + j], recv_sems.at[6 * i + j], (px, py, c)).wait_recv()
                fw = _remote(slab, slab, send_sems.at[6 * i + 3 + j], recv_sems.at[6 * i + 3 + j], sibling)
                fw.start()
                sends.append(fw)
        for i in range(n):
            h = halves[i]
            for j, (px, py) in enumerate(others):
                slab = outs[i].at[2 * px + py, pl.ds((1 - c) * h, h)]
                _remote(slab, slab, send_sems.at[6 * i + 3 + j], recv_sems.at[6 * i + 3 + j], sibling).wait_recv()
        for cp in sends:
            cp.wait_send()

    gathered = pl.pallas_call(
        body, name="gather_weight_shards",
        out_shape=[jax.ShapeDtypeStruct((N_CHIPS,) + s.shape, s.dtype) for s in shards],
        in_specs=[ANY] * n, out_specs=[ANY] * n,
        scratch_shapes=[pltpu.SemaphoreType.DMA((6 * n,)), pltpu.SemaphoreType.DMA((6 * n,))],
    )(*shards)
    chip = 2 * lax.axis_index("x") + lax.axis_index("y")
    return [lax.dynamic_update_slice_in_dim(g, s[None], chip, axis=0) for g, s in zip(gathered, shards)]


def _all_gather8(v):
    def body(v_ref, out_ref, send_sems, recv_sems, local_sem):
        x, y, c = _place()
        me, sibling = (x, y, c), (x, y, 1 - c)
        chips = [(1 - x, y), (x, 1 - y), (1 - x, 1 - y)]

        def slot(px, py, pc):
            return out_ref.at[4 * px + 2 * py + pc]

        def copy(k, block, to, src=None):
            return _remote(slot(*block) if src is None else src, slot(*block), send_sems.at[k], recv_sems.at[k], to)

        mine = pltpu.make_async_copy(v_ref, slot(*me), local_sem)
        mine.start()
        first = [copy(0, me, sibling, src=v_ref)]
        first += [copy(1 + j, me, (*chip, c), src=v_ref) for j, chip in enumerate(chips)]
        for cp in first:
            cp.start()
        passed = [copy(4 + j, (*chip, c), sibling) for j, chip in enumerate(chips)]
        for j, chip in enumerate(chips):
            copy(1 + j, (*chip, c), me).wait_recv()
            passed[j].start()
        copy(0, sibling, me).wait_recv()
        for j, chip in enumerate(chips):
            copy(4 + j, (*chip, 1 - c), me).wait_recv()
        for cp in first + passed:
            cp.wait_send()
        mine.wait()

    return pl.pallas_call(
        body, name="all_gather8",
        out_shape=jax.ShapeDtypeStruct((N_DEV,) + v.shape, v.dtype),
        in_specs=[ANY], out_specs=ANY,
        scratch_shapes=[pltpu.SemaphoreType.DMA((7,)), pltpu.SemaphoreType.DMA((7,)), pltpu.SemaphoreType.DMA],
    )(v)


def _swap_halves(grads):
    n = len(grads)

    def body(*refs):
        ins, outs = refs[:n], refs[n:2 * n]
        send_sems, recv_sems = refs[2 * n:]
        x, y, c = _place()
        cps = [_remote(ins[i].at[1 - c], outs[i], send_sems.at[i], recv_sems.at[i], (x, y, 1 - c)) for i in range(n)]
        for cp in cps:
            cp.start()
        for cp in cps:
            cp.wait()

    return pl.pallas_call(
        body, name="swap_halves",
        out_shape=[jax.ShapeDtypeStruct(g.shape[1:], g.dtype) for g in grads],
        in_specs=[ANY] * n, out_specs=[ANY] * n,
        scratch_shapes=[pltpu.SemaphoreType.DMA((n,)), pltpu.SemaphoreType.DMA((n,))],
    )(*grads)


def _scatter_partials(parts):
    n = len(parts)

    def body(*refs):
        ins, outs = refs[:n], refs[n:2 * n]
        send_sems, recv_sems = refs[2 * n:]
        x, y, c = _place()
        others = [(1 - x, y), (x, 1 - y), (1 - x, 1 - y)]
        cps = []
        for i in range(n):
            for j, (px, py) in enumerate(others):
                k = 3 * i + j
                cps.append(_remote(ins[i].at[2 * px + py], outs[i].at[j], send_sems.at[k], recv_sems.at[k], (px, py, c)))
        for cp in cps:
            cp.start()
        for cp in cps:
            cp.wait()

    return pl.pallas_call(
        body, name="scatter_partials",
        out_shape=[jax.ShapeDtypeStruct((3,) + p.shape[1:], p.dtype) for p in parts],
        in_specs=[ANY] * n, out_specs=[ANY] * n,
        scratch_shapes=[pltpu.SemaphoreType.DMA((3 * n,)), pltpu.SemaphoreType.DMA((3 * n,))],
    )(*parts)


def _share_halves(halves):
    n = len(halves)

    def body(*refs):
        ins, outs = refs[:n], refs[n:2 * n]
        send_sems, recv_sems = refs[2 * n:]
        x, y, c = _place()
        cps = [_remote(ins[i], outs[i], send_sems.at[i], recv_sems.at[i], (x, y, 1 - c)) for i in range(n)]
        for cp in cps:
            cp.start()
        for cp in cps:
            cp.wait()

    theirs = pl.pallas_call(
        body, name="share_halves",
        out_shape=[jax.ShapeDtypeStruct(p.shape, p.dtype) for p in halves],
        in_specs=[ANY] * n, out_specs=[ANY] * n,
        scratch_shapes=[pltpu.SemaphoreType.DMA((n,)), pltpu.SemaphoreType.DMA((n,))],
    )(*halves)
    south = lax.axis_index("c") == 0
    return [jnp.concatenate([jnp.where(south, a, b), jnp.where(south, b, a)], axis=0) for a, b in zip(halves, theirs)]


def _row_tile(rows, cols, n_arrays):
    budget = (VMEM_LIMIT // 3) // (2 * n_arrays * 4 * max(cols, LANES))
    for t in (1024, 512, 256, 128, 64, 32, 16, 8):
        if t <= budget and rows % t == 0:
            return t
    return rows


def _add_sibling(grad, recv):
    cols = grad.shape[-1]
    rows = recv.size // cols
    g3 = grad.reshape(2, rows, cols)
    r2 = recv.reshape(rows, cols)
    tr = _row_tile(rows, cols, 4)

    def body(c_ref, g_ref, r_ref, o_ref, lo_ref):
        total = g_ref[...] + r_ref[...]
        o_ref[...] = total
        lo_ref[...] = total.astype(BF16)

    spec = pl.BlockSpec((tr, cols), lambda i, c_ref: (i, 0))
    out, lo = pl.pallas_call(
        body, name="add_sibling",
        grid_spec=pltpu.PrefetchScalarGridSpec(
            num_scalar_prefetch=1, grid=(rows // tr,),
            in_specs=[pl.BlockSpec((None, tr, cols), lambda i, c_ref: (c_ref[0], i, 0)), spec],
            out_specs=[spec, spec]),
        out_shape=[jax.ShapeDtypeStruct((rows, cols), F32), jax.ShapeDtypeStruct((rows, cols), BF16)],
        compiler_params=pltpu.CompilerParams(vmem_limit_bytes=VMEM_LIMIT),
    )(lax.axis_index("c").astype(jnp.int32).reshape(1), g3, r2)
    return out.reshape(recv.shape), lo.reshape(recv.shape)


def _add_chips(part, recv):
    cols = part.shape[-1]
    rows = part[0].size // cols
    p3 = part.reshape(N_CHIPS, rows, cols)
    r3 = recv.reshape(3, rows, cols)
    tr = _row_tile(rows, cols, 5)

    def body(chip_ref, p_ref, r0_ref, r1_ref, r2_ref, o_ref):
        o_ref[...] = ((p_ref[...] + r0_ref[...].astype(F32)) + r1_ref[...].astype(F32)) + r2_ref[...].astype(F32)

    def recv_spec(j):
        return pl.BlockSpec((None, tr, cols), lambda i, chip_ref: (j, i, 0))

    chip = (2 * lax.axis_index("x") + lax.axis_index("y")).astype(jnp.int32).reshape(1)
    out = pl.pallas_call(
        body, name="add_chips",
        grid_spec=pltpu.PrefetchScalarGridSpec(
            num_scalar_prefetch=1, grid=(rows // tr,),
            in_specs=[pl.BlockSpec((None, tr, cols), lambda i, chip_ref: (chip_ref[0], i, 0)),
                      recv_spec(0), recv_spec(1), recv_spec(2)],
            out_specs=pl.BlockSpec((tr, cols), lambda i, chip_ref: (i, 0))),
        out_shape=jax.ShapeDtypeStruct((rows, cols), F32),
        compiler_params=pltpu.CompilerParams(vmem_limit_bytes=VMEM_LIMIT),
    )(chip, p3, r3, r3, r3)
    return out.reshape(part.shape[1:])


def _sum_devices(gathered):
    _, rows, cols = gathered.shape
    tr = _row_tile(rows, cols, 9)

    def body(g_ref, o_ref):
        acc = g_ref[0]
        for k in range(1, N_DEV):
            acc = acc + g_ref[k]
        o_ref[...] = acc

    return pl.pallas_call(
        body, name="sum_devices", grid=(rows // tr,),
        in_specs=[pl.BlockSpec((N_DEV, tr, cols), lambda i: (0, i, 0))],
        out_specs=pl.BlockSpec((tr, cols), lambda i: (i, 0)),
        out_shape=jax.ShapeDtypeStruct((rows, cols), F32),
        compiler_params=pltpu.CompilerParams(vmem_limit_bytes=VMEM_LIMIT),
    )(gathered)


def _adamw(w, g, m, v):
    shape = w.shape
    cols = shape[-1]
    rows = w.size // cols
    tr = _row_tile(rows, cols, 7)

    def body(w_ref, g_ref, m_ref, v_ref, d_ref, nm_ref, nv_ref):
        gv = g_ref[...]
        nm = ADAM_B1 * m_ref[...] + (1.0 - ADAM_B1) * gv
        nv = ADAM_B2 * v_ref[...] + (1.0 - ADAM_B2) * (gv * gv)
        m_hat = nm / (1.0 - ADAM_B1 ** ADAM_STEP)
        v_hat = nv / (1.0 - ADAM_B2 ** ADAM_STEP)
        d_ref[...] = -ADAM_LR * (m_hat / (jnp.sqrt(v_hat) + ADAM_EPS) + ADAM_WD * w_ref[...])
        nm_ref[...] = nm
        nv_ref[...] = nv

    spec = pl.BlockSpec((tr, cols), lambda i: (i, 0))
    outs = pl.pallas_call(
        body, name="adamw", grid=(rows // tr,),
        in_specs=[spec] * 4, out_specs=[spec] * 3,
        out_shape=[jax.ShapeDtypeStruct((rows, cols), F32)] * 3,
        compiler_params=pltpu.CompilerParams(vmem_limit_bytes=VMEM_LIMIT),
    )(*[t.reshape(rows, cols) for t in (w, g, m, v)])
    return tuple(o.reshape(shape) for o in outs)


_DOT_DIMS = {'nn': (((1,), (0,)), ((), ())), 'nt': (((1,), (1,)), ((), ())), 'tn': (((0,), (0,)), ((), ()))}


def _mm(a, b, mode='nn', a_gate=None, init=None):
    if mode == 'nn':
        (m, k), (_, n) = a.shape, b.shape
    elif mode == 'nt':
        (m, k), (n, _) = a.shape, b.shape
    else:
        (k, m), (_, n) = a.shape, b.shape
    tn = _pick(n, (1408, 1280, 1024, 512, 256, 128))
    if mode == 'tn':
        tm = _pick(m, (1408, 1024, 512, 256, 128))
        tk = _pick(k, (512, 256, 128, 64, 32, 16, 8))
    else:
        tm = _pick(m, (1024, 768, 512, 256, 128, 64, 32, 16, 8))
        tk = _pick(k, (1408, 1280, 1024, 512, 256, 128))
    nk = k // tk

    gated, seeded = a_gate is not None, init is not None

    def body(*refs):
        a_ref, b_ref = refs[0], refs[1 + gated]
        o_ref, acc_ref = refs[-2], refs[-1]
        step = pl.program_id(2)

        @pl.when(step == 0)
        def _():
            acc_ref[...] = refs[2 + gated][...] if seeded else jnp.zeros_like(acc_ref)

        left = a_ref[...]
        if gated:
            left = left * jax.nn.sigmoid(left) * refs[1][...]
        acc_ref[...] += lax.dot_general(left.astype(BF16), b_ref[...].astype(BF16), _DOT_DIMS[mode],
                                        preferred_element_type=F32)

        @pl.when(step == nk - 1)
        def _():
            o_ref[...] = acc_ref[...]

    if mode == 'tn':
        a_spec = pl.BlockSpec((tk, tm), lambda i, j, s: (s, i))
    else:
        a_spec = pl.BlockSpec((tm, tk), lambda i, j, s: (i, s))
    if mode == 'nt':
        b_spec = pl.BlockSpec((tn, tk), lambda i, j, s: (j, s))
    else:
        b_spec = pl.BlockSpec((tk, tn), lambda i, j, s: (s, j))
    o_spec = pl.BlockSpec((tm, tn), lambda i, j, s: (i, j))
    operands = [a] + ([a_gate] if gated else []) + [b] + ([init] if seeded else [])
    return pl.pallas_call(
        body, name=f"mm_{mode}{'_gated' if gated else ''}{'_seeded' if seeded else ''}_{m}x{k}x{n}",
        grid=(m // tm, n // tn, nk),
        in_specs=[a_spec] * (1 + gated) + [b_spec] + [o_spec] * seeded, out_specs=o_spec,
        out_shape=jax.ShapeDtypeStruct((m, n), F32),
        scratch_shapes=[pltpu.VMEM((tm, tn), F32)],
        compiler_params=pltpu.CompilerParams(dimension_semantics=("parallel", "parallel", "arbitrary"),
                                             vmem_limit_bytes=VMEM_LIMIT),
    )(*operands)


class _Stream:
    def __init__(self, bn, t_ctx, t_lat, d):
        self.bn, self.n, self.d = bn, t_ctx + t_lat, d
        self.tr = _pick(math.gcd(t_ctx, t_lat), (256, 128, 64, 32, 16, 8))
        self.ctx_tiles = t_ctx // self.tr
        self.grid = (bn, self.n // self.tr)
        nct, rows = self.ctx_tiles, bn
        self.tok = pl.BlockSpec((None, self.tr, d), lambda b, i: (b, i, 0))
        self.mod = pl.BlockSpec((None, N_MOD, d), lambda b, i: (jnp.where(i < nct, rows, b), 0, 0))
        self.vec = pl.BlockSpec((1, d), lambda b, i: (0, 0))
        self.part = pl.BlockSpec((None, None, 1, d), lambda b, i: (b, jnp.where(i < nct, 0, 1), 0, 0))
        self.per_example = pl.BlockSpec((None, 1, d), lambda b, i: (b, 0, 0))
        self.tok_shape = jax.ShapeDtypeStruct((bn, self.n, d), F32)
        self.part_shape = jax.ShapeDtypeStruct((bn, 2, 1, d), F32)
        self.example_shape = jax.ShapeDtypeStruct((bn, 1, d), F32)
        self.params = pltpu.CompilerParams(dimension_semantics=("parallel", "arbitrary"), vmem_limit_bytes=VMEM_LIMIT)

    def starts_part(self, i):
        return (i == 0) | (i == self.ctx_tiles)


def _modulate(st, x, mod, k):
    def body(x_ref, m_ref, o_ref):
        o_ref[...] = x_ref[...] * (1.0 + m_ref[3 * k + 1:3 * k + 2, :]) + m_ref[3 * k:3 * k + 1, :]

    return pl.pallas_call(body, name=f"modulate_{k}", grid=st.grid, in_specs=[st.tok, st.mod], out_specs=st.tok,
                          out_shape=st.tok_shape, compiler_params=st.params)(x, mod)


def _norm_stats(z):
    mu = jnp.mean(z, axis=-1, keepdims=True)
    zc = z - mu
    rstd = lax.rsqrt(jnp.mean(zc * zc, axis=-1, keepdims=True) + EPS)
    return zc * rstd, rstd


def _post_norm(st, x, y, mod, k, lg, lb, rw, alpha):
    def body(x_ref, y_ref, m_ref, g_ref, b_ref, o_ref):
        xhat, _ = _norm_stats(alpha * x_ref[...] + (rw * m_ref[3 * k + 2:3 * k + 3, :]) * y_ref[...])
        o_ref[...] = xhat * g_ref[...] + b_ref[...]

    return pl.pallas_call(body, name=f"post_norm_{k}", grid=st.grid,
                          in_specs=[st.tok, st.tok, st.mod, st.vec, st.vec], out_specs=st.tok,
                          out_shape=st.tok_shape, compiler_params=st.params)(x, y, mod, lg[None], lb[None])


def _post_norm_bwd(st, x, y, dout, mod, k, lg, rw, alpha):
    def body(x_ref, y_ref, do_ref, m_ref, g_ref, dxr_ref, dy_ref, dgate_ref, dlg_ref, dlb_ref):
        i = pl.program_id(1)
        gate = rw * m_ref[3 * k + 2:3 * k + 3, :]
        yv, dov = y_ref[...], do_ref[...]
        xhat, rstd = _norm_stats(alpha * x_ref[...] + gate * yv)
        dxhat = dov * g_ref[...]
        dz = rstd * (dxhat - jnp.mean(dxhat, axis=-1, keepdims=True)
                     - xhat * jnp.mean(dxhat * xhat, axis=-1, keepdims=True))
        dxr_ref[...] = alpha * dz
        dy_ref[...] = gate * dz

        @pl.when(i == 0)
        def _():
            dlg_ref[...] = jnp.zeros_like(dlg_ref)
            dlb_ref[...] = jnp.zeros_like(dlb_ref)

        @pl.when(st.starts_part(i))
        def _():
            dgate_ref[...] = jnp.zeros_like(dgate_ref)

        dlg_ref[...] += jnp.sum(dov * xhat, axis=0, keepdims=True)
        dlb_ref[...] += jnp.sum(dov, axis=0, keepdims=True)
        dgate_ref[...] += jnp.sum(rw * yv * dz, axis=0, keepdims=True)

    return pl.pallas_call(
        body, name=f"post_norm_bwd_{k}", grid=st.grid, in_specs=[st.tok, st.tok, st.tok, st.mod, st.vec],
        out_specs=[st.tok, st.tok, st.part, st.per_example, st.per_example],
        out_shape=[st.tok_shape, st.tok_shape, st.part_shape, st.example_shape, st.example_shape],
        compiler_params=st.params)(x, y, dout, mod, lg[None])


def _modulate_bwd(st, dh, x, dxr, mod, k):
    def body(dh_ref, x_ref, dxr_ref, m_ref, dx_ref, dsh_ref, dsc_ref):
        dhv = dh_ref[...]
        dx_ref[...] = dxr_ref[...] + dhv * (1.0 + m_ref[3 * k + 1:3 * k + 2, :])

        @pl.when(st.starts_part(pl.program_id(1)))
        def _():
            dsh_ref[...] = jnp.zeros_like(dsh_ref)
            dsc_ref[...] = jnp.zeros_like(dsc_ref)

        dsh_ref[...] += jnp.sum(dhv, axis=0, keepdims=True)
        dsc_ref[...] += jnp.sum(dhv * x_ref[...], axis=0, keepdims=True)

    return pl.pallas_call(
        body, name=f"modulate_bwd_{k}", grid=st.grid, in_specs=[st.tok, st.tok, st.tok, st.mod],
        out_specs=[st.tok, st.part, st.part], out_shape=[st.tok_shape, st.part_shape, st.part_shape],
        compiler_params=st.params)(dh, x, dxr, mod)


def _swiglu_bwd(ds, a, b):
    rows, cols = a.shape
    tr = _row_tile(rows, cols, 5)

    def body(ds_ref, a_ref, b_ref, da_ref, db_ref):
        av, dsv = a_ref[...], ds_ref[...]
        sig = jax.nn.sigmoid(av)
        da_ref[...] = dsv * b_ref[...] * (sig * (1.0 + av * (1.0 - sig)))
        db_ref[...] = dsv * (av * sig)

    spec = pl.BlockSpec((tr, cols), lambda i: (i, 0))
    return pl.pallas_call(body, name="swiglu_bwd", grid=(rows // tr,), in_specs=[spec] * 3, out_specs=[spec] * 2,
                          out_shape=[jax.ShapeDtypeStruct(a.shape, F32)] * 2,
                          compiler_params=pltpu.CompilerParams(vmem_limit_bytes=VMEM_LIMIT))(ds, a, b)


def _rms_norm(x, w):
    return x * lax.rsqrt(jnp.mean(x * x, axis=-1, keepdims=True) + EPS) * w


def _l2_normalize(x):
    return x * lax.rsqrt(jnp.sum(x * x, axis=-1, keepdims=True) + EPS)


def _ffn_fwd(st, x, mod, k, w1, w3, w2, lg, lb, alpha):
    rows = st.bn * st.n
    h = _modulate(st, x, mod, k).reshape(rows, st.d)
    a, b = _mm(h, w1), _mm(h, w3)
    y = _mm(a, w2, a_gate=b).reshape(st.bn, st.n, st.d)
    return _post_norm(st, x, y, mod, k, lg, lb, 0.5, alpha), (h, a, b, y)


def _ffn_bwd(st, x, mod, k, w1, w3, w2, lg, alpha, kept, dout):
    h, a, b, y = kept
    rows = st.bn * st.n
    dxr, dy, dgate, dlg, dlb = _post_norm_bwd(st, x, y, dout, mod, k, lg, 0.5, alpha)
    dy = dy.reshape(rows, st.d)
    ds = _mm(dy, w2, 'nt')
    dw2 = _mm(a, dy, 'tn', a_gate=b)
    da, db = _swiglu_bwd(ds, a, b)
    dw1 = _mm(h, da, 'tn')
    dw3 = _mm(h, db, 'tn')
    dh = _mm(db, w3, 'nt', init=_mm(da, w1, 'nt')).reshape(st.bn, st.n, st.d)
    dx, dsh, dsc = _modulate_bwd(st, dh, x, dxr, mod, k)
    return dx, (dsh, dsc, dgate), dw1, dw3, dw2, jnp.sum(dlg, axis=(0, 1)), jnp.sum(dlb, axis=(0, 1))


def _dwconv_centred(x, w):
    pad = CONV_K // 2
    return lax.conv_general_dilated(
        x, w[:, None, :].astype(x.dtype), window_strides=(1,), padding=[(pad, pad)],
        dimension_numbers=('NWC', 'WIO', 'NWC'), feature_group_count=x.shape[-1])


def _largest_divisor(n, cap):
    return max(t for t in range(1, cap + 1) if n % t == 0)


def _bmm(a, b):
    return lax.dot_general(a, b, (((2,), (1,)), ((0,), (0,))), precision=BF16X3, preferred_element_type=F32)


_BATCHED_DIMS = {'nn': (((2,), (1,)), ((0,), (0,))), 'nt': (((2,), (2,)), ((0,), (0,))),
                 'tn': (((1,), (1,)), ((0,), (0,)))}


def _bdot(a, b, mode):
    return lax.dot_general(a.astype(BF16), b.astype(BF16), _BATCHED_DIMS[mode], preferred_element_type=F32)


@functools.partial(jax.custom_vjp, nondiff_argnums=(2,))
def _bmm_lo(a, b, mode):
    return _bdot(a, b, mode)


def _bmm_lo_fwd(a, b, mode):
    return _bdot(a, b, mode), (a, b)


def _bmm_lo_bwd(mode, res, ct):
    a, b = res
    if mode == 'nn':
        return _bdot(ct, b, 'nt'), _bdot(a, ct, 'tn')
    if mode == 'nt':
        return _bdot(ct, b, 'nn'), _bdot(ct, a, 'tn')
    return _bdot(b, ct, 'nt'), _bdot(a, ct, 'nn')


_bmm_lo.defvjp(_bmm_lo_fwd, _bmm_lo_bwd)


def _lower_inverse_product(lower):
    n = lower.shape[0]
    eye = jnp.where(lax.broadcasted_iota(jnp.int32, (n, CHUNK, CHUNK), 1)
                    == lax.broadcasted_iota(jnp.int32, (n, CHUNK, CHUNK), 2), 1.0, 0.0)
    inv, power = eye - lower, lower
    for _ in range(5):
        power = _bmm(power, power)
        inv = _bmm(inv, eye + power)
    return inv


@jax.custom_vjp
def _unit_lower_inverse(lower):
    return _lower_inverse_product(lower)


def _unit_lower_inverse_fwd(lower):
    inv = _lower_inverse_product(lower)
    return inv, inv


def _unit_lower_inverse_bwd(inv, g):
    right = lax.dot_general(g, inv, _BATCHED_DIMS['nt'], precision=BF16X3, preferred_element_type=F32)
    return (-lax.dot_general(inv, right, _BATCHED_DIMS['tn'], precision=BF16X3, preferred_element_type=F32),)


_unit_lower_inverse.defvjp(_unit_lower_inverse_fwd, _unit_lower_inverse_bwd)


def _gdn_prep(q, k, v, gc, gr, beta):
    n = q.shape[0]
    row = lax.broadcasted_iota(jnp.int32, (n, CHUNK, CHUNK), 1)
    col = lax.broadcasted_iota(jnp.int32, (n, CHUNK, CHUNK), 2)
    incl, strict = row >= col, row > col
    decay = jnp.where(incl, jnp.exp(jnp.where(incl, gc - gr, 0.0)), 0.0)
    kb = k * beta
    lower = jnp.where(strict, _bmm_lo(kb, k, 'nt') * decay, 0.0)
    inv = _unit_lower_inverse(lower)
    u = _bmm(inv, v * beta)
    w = _bmm(inv, kb * jnp.exp(gc))
    intra = jnp.where(incl, _bmm_lo(q, k, 'nt') * decay, 0.0)
    is_last = lax.broadcasted_iota(jnp.int32, (n, CHUNK, 1), 1) == CHUNK - 1
    g_last = jnp.sum(jnp.where(is_last, gc, 0.0), axis=1, keepdims=True)
    return u, w, intra, q * jnp.exp(gc), k * jnp.exp(g_last - gc), jnp.exp(g_last) * jnp.ones((n, 1, GDN_DV), F32)


def _gdn_step(state, u, w, intra, qg, kd, gl):
    v_new = u - _bmm_lo(w, state, 'nn')
    o = _bmm_lo(qg, state, 'nn') + _bmm_lo(intra, v_new, 'nn')
    return o, state * gl + _bmm_lo(kd, v_new, 'tn')


def _gdn_prep_specs(cb):
    idx = lambda c, i: (c, i, 0, 0)
    mat = pl.BlockSpec((None, cb, CHUNK, GDN_DK), idx)
    colv = pl.BlockSpec((None, cb, CHUNK, 1), idx)
    rowv = pl.BlockSpec((None, cb, 1, CHUNK), idx)
    return mat, colv, rowv


def _gdn_prep_fwd_call(q, k, v, gc, gr, beta):
    chains, nchunks = q.shape[:2]
    cb = _largest_divisor(nchunks, GDN_PREP_CHUNKS)
    mat, colv, rowv = _gdn_prep_specs(cb)

    def body(q_ref, k_ref, v_ref, gc_ref, gr_ref, b_ref, *out_refs):
        outs = _gdn_prep(q_ref[...], k_ref[...], v_ref[...], gc_ref[...], gr_ref[...], b_ref[...])
        for ref, val in zip(out_refs, outs):
            ref[...] = val

    mshape = jax.ShapeDtypeStruct(q.shape, F32)
    return pl.pallas_call(
        body, name="gdn_prep_fwd", grid=(chains, nchunks // cb),
        in_specs=[mat, mat, mat, colv, rowv, colv], out_specs=[mat] * 5 + [rowv],
        out_shape=[mshape] * 5 + [jax.ShapeDtypeStruct(gr.shape, F32)],
        compiler_params=pltpu.CompilerParams(dimension_semantics=("parallel", "parallel"), vmem_limit_bytes=VMEM_LIMIT),
    )(q, k, v, gc, gr, beta)


def _gdn_prep_bwd_call(q, k, v, gc, gr, beta, cts):
    chains, nchunks = q.shape[:2]
    cb = _largest_divisor(nchunks, GDN_PREP_CHUNKS)
    mat, colv, rowv = _gdn_prep_specs(cb)

    def body(q_ref, k_ref, v_ref, gc_ref, gr_ref, b_ref, du, dw, di, dqg, dkd, dgl, *out_refs):
        _, vjp = jax.vjp(_gdn_prep, q_ref[...], k_ref[...], v_ref[...], gc_ref[...], gr_ref[...], b_ref[...])
        grads = vjp((du[...], dw[...], di[...], dqg[...], dkd[...], dgl[...]))
        for ref, val in zip(out_refs, grads):
            ref[...] = val

    return pl.pallas_call(
        body, name="gdn_prep_bwd", grid=(chains, nchunks // cb),
        in_specs=[mat, mat, mat, colv, rowv, colv] + [mat] * 5 + [rowv],
        out_specs=[mat, mat, mat, colv, rowv, colv],
        out_shape=[jax.ShapeDtypeStruct(t.shape, F32) for t in (q, k, v, gc, gr, beta)],
        compiler_params=pltpu.CompilerParams(dimension_semantics=("parallel", "parallel"), vmem_limit_bytes=VMEM_LIMIT),
    )(q, k, v, gc, gr, beta, *cts)


def _gdn_scan_fwd_call(u, w, intra, qg, kd, gl):
    chains, nchunks = u.shape[:2]
    cc = _largest_divisor(chains, GDN_SCAN_CHAINS)
    idx = lambda c, i: (c, i, 0, 0)
    mat = pl.BlockSpec((cc, None, CHUNK, GDN_DK), idx)
    rowv = pl.BlockSpec((cc, None, 1, CHUNK), idx)

    def body(u_ref, w_ref, i_ref, qg_ref, kd_ref, gl_ref, o_ref, hist_ref, state_ref):
        @pl.when(pl.program_id(1) == 0)
        def _():
            state_ref[...] = jnp.zeros_like(state_ref)

        state = state_ref[...]
        hist_ref[...] = state
        o, new = _gdn_step(state, u_ref[...], w_ref[...], i_ref[...], qg_ref[...], kd_ref[...], gl_ref[...])
        o_ref[...] = o
        state_ref[...] = new

    mshape = jax.ShapeDtypeStruct(u.shape, F32)
    return pl.pallas_call(
        body, name="gdn_scan_fwd", grid=(chains // cc, nchunks),
        in_specs=[mat] * 5 + [rowv], out_specs=[mat, mat], out_shape=[mshape, mshape],
        scratch_shapes=[pltpu.VMEM((cc, GDN_DK, GDN_DV), F32)],
        compiler_params=pltpu.CompilerParams(dimension_semantics=("parallel", "arbitrary"), vmem_limit_bytes=VMEM_LIMIT),
    )(u, w, intra, qg, kd, gl)


def _gdn_scan_bwd_call(u, w, intra, qg, kd, gl, hist, do):
    chains, nchunks = u.shape[:2]
    cc = _largest_divisor(chains, GDN_SCAN_CHAINS)
    idx = lambda c, i: (c, nchunks - 1 - i, 0, 0)
    mat = pl.BlockSpec((cc, None, CHUNK, GDN_DK), idx)
    rowv = pl.BlockSpec((cc, None, 1, CHUNK), idx)

    def body(u_ref, w_ref, i_ref, qg_ref, kd_ref, gl_ref, h_ref, do_ref, du, dw, di, dqg, dkd, dgl, dstate_ref):
        @pl.when(pl.program_id(1) == 0)
        def _():
            dstate_ref[...] = jnp.zeros_like(dstate_ref)

        _, vjp = jax.vjp(_gdn_step, h_ref[...], u_ref[...], w_ref[...], i_ref[...], qg_ref[...], kd_ref[...], gl_ref[...])
        grads = vjp((do_ref[...], dstate_ref[...]))
        dstate_ref[...] = grads[0]
        for ref, val in zip((du, dw, di, dqg, dkd, dgl), grads[1:]):
            ref[...] = val

    mshape = jax.ShapeDtypeStruct(u.shape, F32)
    return pl.pallas_call(
        body, name="gdn_scan_bwd", grid=(chains // cc, nchunks),
        in_specs=[mat] * 5 + [rowv, mat, mat], out_specs=[mat] * 5 + [rowv],
        out_shape=[mshape] * 5 + [jax.ShapeDtypeStruct(gl.shape, F32)],
        scratch_shapes=[pltpu.VMEM((cc, GDN_DK, GDN_DV), F32)],
        compiler_params=pltpu.CompilerParams(dimension_semantics=("parallel", "arbitrary"), vmem_limit_bytes=VMEM_LIMIT),
    )(u, w, intra, qg, kd, gl, hist, do)


@jax.custom_vjp
def _gdn_core(q, k, v, gc, gr, beta):
    return _gdn_scan_fwd_call(*_gdn_prep_fwd_call(q, k, v, gc, gr, beta))[0]


def _gdn_core_fwd(q, k, v, gc, gr, beta):
    prep = _gdn_prep_fwd_call(q, k, v, gc, gr, beta)
    o, hist = _gdn_scan_fwd_call(*prep)
    return o, (q, k, v, gc, gr, beta, prep, hist)


def _gdn_core_bwd(res, do):
    q, k, v, gc, gr, beta, prep, hist = res
    cts = _gdn_scan_bwd_call(*prep, hist, do)
    return tuple(_gdn_prep_bwd_call(q, k, v, gc, gr, beta, cts))


_gdn_core.defvjp(_gdn_core_fwd, _gdn_core_bwd)


def _gdn_inputs(qkv, b, a, conv_w, a_log, dt_bias):
    Bn, T, _ = qkv.shape
    qkv = jax.nn.silu(_dwconv_centred(qkv, conv_w))
    q, k, v = jnp.split(qkv, [GDN_HEADS * GDN_DK, 2 * GDN_HEADS * GDN_DK], axis=-1)

    def to_heads(t, d):
        return t.reshape(Bn, T, GDN_HEADS, d).transpose(0, 2, 1, 3)

    def dir_heads(t):
        return t.reshape(Bn, T, 2, GDN_HEADS).transpose(2, 0, 3, 1)

    q = _l2_normalize(to_heads(q, GDN_DK))
    k = _l2_normalize(to_heads(k, GDN_DK))
    v = to_heads(v, GDN_DV)
    beta = jax.nn.sigmoid(dir_heads(b))
    g = -jnp.exp(a_log)[:, None, :, None] * jax.nn.softplus(dir_heads(a) + dt_bias[:, None, :, None])
    return q, k, v, g, beta


def _gdn_gated_out(o, z, norm_w):
    Bn, H, T, dv = o.shape
    o = _rms_norm(o.transpose(0, 2, 1, 3), norm_w)
    o = o * jax.nn.silu(z.reshape(Bn, T, H, dv))
    return o.reshape(Bn, T, H * dv)


def _gdn_group(qkv, z, b, a, qkv_c, z_c, b_c, a_c, conv_w, a_log, dt_bias, norm_w, with_ctx_out):
    q, k, v, g, beta = _gdn_inputs(qkv, b, a, conv_w, a_log, dt_bias)
    qc, kc, vc, gc, betac = _gdn_inputs(qkv_c, b_c, a_c, conv_w, a_log, dt_bias)
    bn, heads, t_lat, _ = q.shape
    t_ctx = qc.shape[2]
    n = t_ctx + t_lat
    nchunks = n // CHUNK
    chains = 2 * bn * heads

    def both(tc, tl):
        return jnp.stack([jnp.concatenate([tc, tl], axis=2), jnp.concatenate([jnp.flip(tc, 2), jnp.flip(tl, 2)], axis=2)])

    def per_dir(tc, tl):
        return jnp.stack([jnp.concatenate([tc[0], tl[0]], axis=2),
                          jnp.concatenate([jnp.flip(tc[1], 2), jnp.flip(tl[1], 2)], axis=2)])

    def mats(t):
        return t.reshape(chains, nchunks, CHUNK, t.shape[-1])

    gcum = jnp.cumsum(per_dir(gc, g).reshape(chains, nchunks, CHUNK), axis=-1)
    o = _gdn_core(mats(both(qc, q) * GDN_DK ** -0.5), mats(both(kc, k)), mats(both(vc, v)),
                  gcum[..., None], gcum[:, :, None, :], per_dir(betac, beta).reshape(chains, nchunks, CHUNK, 1))
    o = o.reshape(2, bn, heads, n, GDN_DV)
    out = _gdn_gated_out(o[0, :, :, t_ctx:] + jnp.flip(o[1, :, :, t_ctx:], 2), z, norm_w)
    if not with_ctx_out:
        return out, None
    return out, _gdn_gated_out(o[0, :, :, :t_ctx] + jnp.flip(o[1, :, :, :t_ctx], 2), z_c, norm_w)


def _axial_rope_tables(rows):
    row = jnp.repeat(jnp.arange(rows), GRID_W)
    col = jnp.tile(jnp.arange(GRID_W), rows)
    inv_freq = ROPE_THETA ** (-jnp.arange(ROPE_PAIRS, dtype=F32) / ROPE_PAIRS)
    ang = jnp.stack([row, col], axis=-1).astype(F32)[..., None] * inv_freq
    return jnp.cos(ang), jnp.sin(ang)


def _rope_2d(x, cos, sin):
    shp = x.shape
    xr = x.reshape(*shp[:-1], 2, 2, ROPE_PAIRS)
    x1, x2 = xr[..., 0, :], xr[..., 1, :]
    bshape = (shp[1],) + (1,) * (x.ndim - 3) + (2, ROPE_PAIRS)
    c, s = cos.reshape(bshape), sin.reshape(bshape)
    out = jnp.stack([x1 * c - x2 * s, x2 * c + x1 * s], axis=-2)
    return out.reshape(shp)


def _attn_specs(tq, tk):
    q_spec = pl.BlockSpec((None, None, ATT_GROUP, tq, ATT_DH), lambda b, h, i: (b, h, 0, i, 0))
    kv_spec = pl.BlockSpec((None, None, tk, ATT_DH), lambda b, h, i: (b, h, 0, 0))
    return q_spec, kv_spec


def _softmax_rows(q, k):
    s = lax.dot_general(q.astype(BF16), k.astype(BF16), (((1,), (1,)), ((), ())), preferred_element_type=F32)
    s = s * (ATT_DH ** -0.5)
    e = jnp.exp(s - jnp.max(s, axis=-1, keepdims=True))
    return e / jnp.sum(e, axis=-1, keepdims=True)


def _attn_fwd_call(q, k, v):
    bn, _, _, t_q, _ = q.shape
    t_k = k.shape[2]
    tq = _pick(t_q, (Q_BLOCK, 64, 32, 16, 8))
    q_spec, kv_spec = _attn_specs(tq, t_k)

    def body(q_ref, k_ref, v_ref, o_ref):
        p = _softmax_rows(q_ref[...].reshape(ATT_GROUP * tq, ATT_DH), k_ref[...])
        o = lax.dot_general(p.astype(BF16), v_ref[...].astype(BF16), (((1,), (0,)), ((), ())), preferred_element_type=F32)
        o_ref[...] = o.reshape(ATT_GROUP, tq, ATT_DH)

    return pl.pallas_call(
        body, name=f"attention_fwd_{t_q}x{t_k}", grid=(bn, ATT_KV_HEADS, t_q // tq),
        in_specs=[q_spec, kv_spec, kv_spec], out_specs=q_spec,
        out_shape=jax.ShapeDtypeStruct(q.shape, F32),
        compiler_params=pltpu.CompilerParams(dimension_semantics=("parallel", "parallel", "parallel"),
                                             vmem_limit_bytes=VMEM_LIMIT),
    )(q, k, v)


def _attn_bwd_call(q, k, v, do):
    bn, _, _, t_q, _ = q.shape
    t_k = k.shape[2]
    tq = _pick(t_q, (Q_BLOCK, 64, 32, 16, 8))
    q_spec, kv_spec = _attn_specs(tq, t_k)

    def body(q_ref, k_ref, v_ref, do_ref, dq_ref, dk_ref, dv_ref):
        @pl.when(pl.program_id(2) == 0)
        def _():
            dk_ref[...] = jnp.zeros_like(dk_ref)
            dv_ref[...] = jnp.zeros_like(dv_ref)

        qv = q_ref[...].reshape(ATT_GROUP * tq, ATT_DH)
        dov = do_ref[...].reshape(ATT_GROUP * tq, ATT_DH).astype(BF16)
        kb, vb = k_ref[...].astype(BF16), v_ref[...].astype(BF16)
        p = _softmax_rows(qv, k_ref[...])
        dp = lax.dot_general(dov, vb, (((1,), (1,)), ((), ())), preferred_element_type=F32)
        ds = p * (dp - jnp.sum(p * dp, axis=-1, keepdims=True)) * (ATT_DH ** -0.5)
        dsb = ds.astype(BF16)
        dq = lax.dot_general(dsb, kb, (((1,), (0,)), ((), ())), preferred_element_type=F32)
        dq_ref[...] = dq.reshape(ATT_GROUP, tq, ATT_DH)
        dk_ref[...] += lax.dot_general(dsb, qv.astype(BF16), (((0,), (0,)), ((), ())), preferred_element_type=F32)
        dv_ref[...] += lax.dot_general(p.astype(BF16), dov, (((0,), (0,)), ((), ())), preferred_element_type=F32)

    return pl.pallas_call(
        body, name=f"attention_bwd_{t_q}x{t_k}", grid=(bn, ATT_KV_HEADS, t_q // tq),
        in_specs=[q_spec, kv_spec, kv_spec, q_spec], out_specs=[q_spec, kv_spec, kv_spec],
        out_shape=[jax.ShapeDtypeStruct(t.shape, F32) for t in (q, k, v)],
        compiler_params=pltpu.CompilerParams(dimension_semantics=("parallel", "parallel", "arbitrary"),
                                             vmem_limit_bytes=VMEM_LIMIT),
    )(q, k, v, do)


@jax.custom_vjp
def _attn_core(q, k, v):
    return _attn_fwd_call(q, k, v)


def _attn_core_fwd(q, k, v):
    return _attn_fwd_call(q, k, v), (q, k, v)


def _attn_core_bwd(res, do):
    return tuple(_attn_bwd_call(*res, do))


_attn_core.defvjp(_attn_core_fwd, _attn_core_bwd)


def _attention_group(q, k, v, q_c, k_c, v_c, q_norm_w, k_norm_w, cos, sin, with_ctx_out):
    Bn, T, _ = q.shape
    Tc = q_c.shape[1]
    q = _rope_2d(_rms_norm(q.reshape(Bn, T, ATT_KV_HEADS, ATT_GROUP, ATT_DH), q_norm_w), cos, sin)
    k = _rope_2d(_rms_norm(k.reshape(Bn, T, ATT_KV_HEADS, ATT_DH), k_norm_w), cos, sin)
    v = v.reshape(Bn, T, ATT_KV_HEADS, ATT_DH)
    qc = _rms_norm(q_c.reshape(Bn, Tc, ATT_KV_HEADS, ATT_GROUP, ATT_DH), q_norm_w)
    kc = _rms_norm(k_c.reshape(Bn, Tc, ATT_KV_HEADS, ATT_DH), k_norm_w)
    vc = v_c.reshape(Bn, Tc, ATT_KV_HEADS, ATT_DH)
    keys = jnp.concatenate([kc, k], axis=1).transpose(0, 2, 1, 3)
    vals = jnp.concatenate([vc, v], axis=1).transpose(0, 2, 1, 3)
    o = _attn_core(q.transpose(0, 2, 3, 1, 4), keys, vals)
    o = o.transpose(0, 3, 1, 2, 4).reshape(Bn, T, ATT_W)
    if not with_ctx_out:
        return o, None
    o_c = _attn_core(qc.transpose(0, 2, 3, 1, 4), kc.transpose(0, 2, 1, 3), vc.transpose(0, 2, 1, 3))
    return o, o_c.transpose(0, 3, 1, 2, 4).reshape(Bn, Tc, ATT_W)


def _s5_operators(lam_re, lam_im, log_dt, b_re, b_im, c_re, c_im):
    lc = S5_LC
    dt = jnp.exp(log_dt)[..., None]
    ar, ai = lam_re * dt, lam_im * dt
    mag = jnp.exp(ar)
    lbr, lbi = mag * jnp.cos(ai), mag * jnp.sin(ai)
    den = lam_re * lam_re + lam_im * lam_im
    fr = ((lbr - 1.0) * lam_re + lbi * lam_im) / den
    fi = (lbi * lam_re - (lbr - 1.0) * lam_im) / den
    bbr = fr[..., None] * b_re - fi[..., None] * b_im
    bbi = fr[..., None] * b_im + fi[..., None] * b_re
    m = jnp.arange(lc + 1, dtype=F32)[:, None, None, None]
    pmag = jnp.exp(m * ar)
    pr, pi = pmag * jnp.cos(m * ai), pmag * jnp.sin(m * ai)
    cpr = c_re[None] * pr[:, :, :, None, :] - c_im[None] * pi[:, :, :, None, :]
    cpi = c_re[None] * pi[:, :, :, None, :] + c_im[None] * pr[:, :, :, None, :]
    kern = (jnp.einsum('mdghp,dgpk->mdghk', cpr[:lc], bbr, precision=HI)
            - jnp.einsum('mdghp,dgpk->mdghk', cpi[:lc], bbi, precision=HI))
    tail = kern.shape[1:]
    lags = jnp.concatenate([jnp.zeros((lc - 1,) + tail, F32), kern, jnp.zeros((1,) + tail, F32)], axis=0)
    toep = jnp.tile(lags, (lc,) + (1,) * len(tail))[:lc * (2 * lc - 1)].reshape((lc, 2 * lc - 1) + tail)[:, lc - 1:]
    tm = toep.transpose(2, 3, 0, 5, 1, 4).reshape(2, S5_GROUPS, S5_CW, S5_CW)
    prr, pir = pr[lc - 1::-1], pi[lc - 1::-1]
    mre = prr[..., None] * bbr[None] - pir[..., None] * bbi[None]
    mim = prr[..., None] * bbi[None] + pir[..., None] * bbr[None]
    mm = jnp.concatenate([mre, mim], axis=3).transpose(1, 2, 0, 4, 3).reshape(2, S5_GROUPS, S5_CW, S5_SW)
    nm = jnp.concatenate([cpr[1:], -cpi[1:]], axis=-1)
    nm = nm.transpose(1, 2, 4, 0, 3).reshape(2, S5_GROUPS, S5_SW, S5_CW)
    a1 = jnp.concatenate([pr[lc], pr[lc]], axis=-1)
    a2 = jnp.concatenate([-pi[lc], pi[lc]], axis=-1)
    lam_rows = jnp.concatenate([a1[:, :, None], a2[:, :, None],
                                jnp.zeros((2, S5_GROUPS, SUBLANES - 2, S5_SW), F32)], axis=2)
    return tm, mm, nm, lam_rows


def _dot_hi(a, b, dims):
    return lax.dot_general(a, b, (dims, ((), ())), precision=BF16X3, preferred_element_type=F32)


def _s5_blocks(rows):
    def blk(r, c):
        return pl.BlockSpec((None, None, r, c), lambda d, g: (d, g, 0, 0))
    return (blk(rows, S5_CW), blk(S5_CW, S5_CW), blk(S5_CW, S5_SW), blk(S5_SW, S5_CW), blk(SUBLANES, S5_SW),
            blk(rows, S5_SW))


def _s5_core_fwd_call(s, tm, mm, nm, lam_rows):
    rows = s.shape[2]
    chunks = rows // S5_ROWS
    seq, top, mop, nop, lop, sta = _s5_blocks(rows)

    def body(s_ref, t_ref, m_ref, n_ref, l_ref, y_ref, h_ref, e_ref):
        sv = s_ref[...]
        e_ref[...] = _dot_hi(sv, m_ref[...], ((1,), (0,)))
        a1, a2 = l_ref[0:1, :], l_ref[1:2, :]
        h_ref[0:S5_ROWS, :] = jnp.zeros((S5_ROWS, S5_SW), F32)

        def step(k, carry):
            at = pl.multiple_of((k - 1) * S5_ROWS, S5_ROWS)
            prev = h_ref[pl.ds(at, S5_ROWS), :]
            new = a1 * prev + a2 * pltpu.roll(prev, S5_P, 1) + e_ref[pl.ds(at, S5_ROWS), :]
            h_ref[pl.ds(pl.multiple_of(k * S5_ROWS, S5_ROWS), S5_ROWS), :] = new
            return carry

        lax.fori_loop(1, chunks, step, 0)
        y_ref[...] = _dot_hi(sv, t_ref[...], ((1,), (0,))) + _dot_hi(h_ref[...], n_ref[...], ((1,), (0,)))

    return pl.pallas_call(
        body, name="s5_chunks_fwd", grid=(2, S5_GROUPS),
        in_specs=[seq, top, mop, nop, lop], out_specs=[seq, sta],
        out_shape=[jax.ShapeDtypeStruct(s.shape, F32), jax.ShapeDtypeStruct(s.shape[:3] + (S5_SW,), F32)],
        scratch_shapes=[pltpu.VMEM((rows, S5_SW), F32)],
        compiler_params=pltpu.CompilerParams(dimension_semantics=("parallel", "parallel"), vmem_limit_bytes=VMEM_LIMIT),
    )(s, tm, mm, nm, lam_rows)


def _s5_core_bwd_call(s, dy, tm, mm, nm, lam_rows, hin):
    rows = s.shape[2]
    chunks = rows // S5_ROWS
    seq, top, mop, nop, lop, sta = _s5_blocks(rows)

    def body(s_ref, dy_ref, t_ref, m_ref, n_ref, l_ref, h_ref, ds_ref, dt_ref, dm_ref, dn_ref, dl_ref, dh_ref, de_ref):
        sv, dyv, hv = s_ref[...], dy_ref[...], h_ref[...]
        dh_ref[...] = _dot_hi(dyv, n_ref[...], ((1,), (1,)))
        a1, a2 = l_ref[0:1, :], l_ref[1:2, :]
        last = (chunks - 1) * S5_ROWS
        de_ref[last:last + S5_ROWS, :] = jnp.zeros((S5_ROWS, S5_SW), F32)

        def step(i, g):
            k = chunks - 2 - i
            at = pl.multiple_of(k * S5_ROWS, S5_ROWS)
            de_ref[pl.ds(at, S5_ROWS), :] = g
            return dh_ref[pl.ds(at, S5_ROWS), :] + a1 * g + pltpu.roll(a2 * g, S5_P, 1)

        lax.fori_loop(0, chunks - 1, step, dh_ref[last:last + S5_ROWS, :])
        dev = de_ref[...]
        ds_ref[...] = _dot_hi(dyv, t_ref[...], ((1,), (1,))) + _dot_hi(dev, m_ref[...], ((1,), (1,)))
        dt_ref[...] = _dot_hi(sv, dyv, ((0,), (0,)))
        dm_ref[...] = _dot_hi(sv, dev, ((0,), (0,)))
        dn_ref[...] = _dot_hi(hv, dyv, ((0,), (0,)))
        da1 = jnp.sum(hv * dev, axis=0, keepdims=True)
        da2 = jnp.sum(pltpu.roll(hv, S5_P, 1) * dev, axis=0, keepdims=True)
        dl_ref[...] = jnp.concatenate([da1, da2, jnp.zeros((SUBLANES - 2, S5_SW), F32)], axis=0)

    return pl.pallas_call(
        body, name="s5_chunks_bwd", grid=(2, S5_GROUPS),
        in_specs=[seq, seq, top, mop, nop, lop, sta], out_specs=[seq, top, mop, nop, lop],
        out_shape=[jax.ShapeDtypeStruct(t.shape, F32) for t in (s, tm, mm, nm, lam_rows)],
        scratch_shapes=[pltpu.VMEM((rows, S5_SW), F32), pltpu.VMEM((rows, S5_SW), F32)],
        compiler_params=pltpu.CompilerParams(dimension_semantics=("parallel", "parallel"), vmem_limit_bytes=VMEM_LIMIT),
    )(s, dy, tm, mm, nm, lam_rows, hin)


@jax.custom_vjp
def _s5_core(s, tm, mm, nm, lam_rows):
    return _s5_core_fwd_call(s, tm, mm, nm, lam_rows)[0]


def _s5_core_fwd(s, tm, mm, nm, lam_rows):
    y, hin = _s5_core_fwd_call(s, tm, mm, nm, lam_rows)
    return y, (s, tm, mm, nm, lam_rows, hin)


def _s5_core_bwd(res, dy):
    s, tm, mm, nm, lam_rows, hin = res
    return tuple(_s5_core_bwd_call(s, dy, tm, mm, nm, lam_rows, hin))


_s5_core.defvjp(_s5_core_fwd, _s5_core_bwd)


def _s5_group(u, u_c, lam_re, lam_im, log_dt, b_re, b_im, c_re, c_im, d_skip, glu_w, glu_b, with_ctx_out):
    bn, t_lat, _ = u.shape
    t_ctx = u_c.shape[1]
    n = t_ctx + t_lat
    chunks = n // S5_LC
    assert bn <= S5_ROWS and t_ctx % S5_LC == 0 and t_lat % S5_LC == 0
    seqs = jnp.stack([jnp.concatenate([u_c, u], axis=1),
                      jnp.concatenate([jnp.flip(u_c, 1), jnp.flip(u, 1)], axis=1)])
    s = seqs.reshape(2, bn, chunks, S5_LC, S5_GROUPS, S5_GH).transpose(0, 4, 2, 1, 3, 5)
    s = jnp.pad(s, ((0, 0), (0, 0), (0, 0), (0, S5_ROWS - bn), (0, 0), (0, 0)))
    s = s.reshape(2, S5_GROUPS, chunks * S5_ROWS, S5_CW)
    y = _s5_core(s, *_s5_operators(lam_re, lam_im, log_dt, b_re, b_im, c_re, c_im))
    y = y.reshape(2, S5_GROUPS, chunks, S5_ROWS, S5_LC, S5_GH)[:, :, :, :bn]
    y = y.transpose(0, 3, 2, 4, 1, 5).reshape(2, bn, n, S5_W)
    yl = d_skip * u + y[0, :, t_ctx:] + jnp.flip(y[1, :, t_ctx:], 1)

    def glu(yy):
        zz = jax.nn.gelu(yy)
        return zz * jax.nn.sigmoid(zz @ glu_w + glu_b)

    if not with_ctx_out:
        return glu(yl), None
    yc = d_skip * u_c + y[0, :, :t_ctx] + jnp.flip(y[1, :, :t_ctx], 1)
    return glu(yl), glu(yc)


def _s5_wide_operators(lam_re, lam_im, log_dt, b_re, b_im, c_re, c_im):
    lc, g4, nd = S5W_LC, S5W_TILE // S5_SW, lam_re.shape[0]
    dt = jnp.exp(log_dt)[..., None]
    ar, ai = lam_re * dt, lam_im * dt
    mag = jnp.exp(ar)
    lbr, lbi = mag * jnp.cos(ai), mag * jnp.sin(ai)
    den = lam_re * lam_re + lam_im * lam_im
    fr = ((lbr - 1.0) * lam_re + lbi * lam_im) / den
    fi = (lbi * lam_re - (lbr - 1.0) * lam_im) / den
    bbr = fr[..., None] * b_re - fi[..., None] * b_im
    bbi = fr[..., None] * b_im + fi[..., None] * b_re
    m = jnp.arange(lc + 1, dtype=F32)[:, None, None, None]
    pmag = jnp.exp(m * ar)
    pr, pi = pmag * jnp.cos(m * ai), pmag * jnp.sin(m * ai)
    cpr = c_re[None] * pr[:, :, :, None, :] - c_im[None] * pi[:, :, :, None, :]
    cpi = c_re[None] * pi[:, :, :, None, :] + c_im[None] * pr[:, :, :, None, :]
    kern = (jnp.einsum('mdghp,dgpk->mdghk', cpr[:lc], bbr, precision=HI)
            - jnp.einsum('mdghp,dgpk->mdghk', cpi[:lc], bbi, precision=HI))
    lags = jnp.concatenate([jnp.zeros((lc - 1,) + kern.shape[1:], F32), kern], axis=0)
    toep = jnp.stack([lags[lc - 1 - j:2 * lc - 1 - j] for j in range(lc)])
    eye = jnp.eye(S5_GROUPS, dtype=F32)
    tm = jnp.einsum('jtdghk,gn->djgktnh', toep, eye).reshape(nd, S5W_RW, S5W_RW)
    back = (lc - 1.0 - jnp.arange(lc, dtype=F32))[:, None, None, None]
    bmag = jnp.exp(back * ar)
    prr, pir = bmag * jnp.cos(back * ai), bmag * jnp.sin(back * ai)
    left = jnp.stack([prr[..., None] * bbr[None] - pir[..., None] * bbi[None],
                      prr[..., None] * bbi[None] + pir[..., None] * bbr[None]])
    mm = jnp.einsum('ajdgpk,gn->djgknap', left, eye)
    mm = mm.reshape(nd, lc, S5_GROUPS, S5_GH, S5_GROUPS // g4, g4, 2, S5_P).transpose(0, 1, 2, 3, 4, 6, 5, 7)
    mm = mm.reshape(nd, S5W_RW, S5W_SW)
    right = jnp.stack([cpr[1:], -cpi[1:]])
    nm = jnp.einsum('atdghp,gn->dnaptgh', right, eye)
    nm = nm.reshape(nd, S5_GROUPS // g4, g4, 2, S5_P, lc, S5_GROUPS, S5_GH).transpose(0, 1, 3, 2, 4, 5, 6, 7)
    nm = nm.reshape(nd, S5W_SW, S5W_RW)

    def state_cols(re_part, im_part):
        t = jnp.stack([re_part, im_part], axis=1).reshape(nd, 2, S5_GROUPS // g4, g4, S5_P)
        return t.transpose(0, 2, 1, 3, 4).reshape(nd, S5W_SW)

    a1, a2 = state_cols(pr[lc], pr[lc]), state_cols(-pi[lc], pi[lc])
    lam_rows = jnp.concatenate([a1[:, None], a2[:, None], jnp.zeros((nd, SUBLANES - 2, S5W_SW), F32)], axis=1)
    return tm, mm, nm, lam_rows


def _s5_carry_specs(rows):
    wide = pl.BlockSpec((rows, S5W_TILE), lambda j: (0, j))
    lam = pl.BlockSpec((SUBLANES, S5W_TILE), lambda j: (0, j))
    return wide, lam


def _s5_carry_fwd(e, lam_rows):
    rows = e.shape[0]
    steps = rows // S5_ROWS
    wide, lam = _s5_carry_specs(rows)

    def body(e_ref, l_ref, h_ref):
        a1, a2 = l_ref[0:1, :], l_ref[1:2, :]
        h_ref[0:S5_ROWS, :] = jnp.zeros((S5_ROWS, S5W_TILE), F32)

        def step(k, carry):
            at = pl.multiple_of((k - 1) * S5_ROWS, S5_ROWS)
            prev = h_ref[pl.ds(at, S5_ROWS), :]
            new = a1 * prev + a2 * pltpu.roll(prev, S5W_TILE // 2, 1) + e_ref[pl.ds(at, S5_ROWS), :]
            h_ref[pl.ds(pl.multiple_of(k * S5_ROWS, S5_ROWS), S5_ROWS), :] = new
            return carry

        lax.fori_loop(1, steps, step, 0)

    return pl.pallas_call(
        body, name="s5_carry_fwd", grid=(S5W_SW // S5W_TILE,), in_specs=[wide, lam], out_specs=wide,
        out_shape=jax.ShapeDtypeStruct(e.shape, F32),
        compiler_params=pltpu.CompilerParams(dimension_semantics=("parallel",), vmem_limit_bytes=VMEM_LIMIT),
    )(e, lam_rows)


def _s5_carry_bwd(dh, hin, lam_rows):
    rows = dh.shape[0]
    steps = rows // S5_ROWS
    wide, lam = _s5_carry_specs(rows)

    def body(dh_ref, h_ref, l_ref, de_ref, dl_ref):
        a1, a2 = l_ref[0:1, :], l_ref[1:2, :]
        last = (steps - 1) * S5_ROWS
        de_ref[last:last + S5_ROWS, :] = jnp.zeros((S5_ROWS, S5W_TILE), F32)

        def step(i, g):
            at = pl.multiple_of((steps - 2 - i) * S5_ROWS, S5_ROWS)
            de_ref[pl.ds(at, S5_ROWS), :] = g
            return dh_ref[pl.ds(at, S5_ROWS), :] + a1 * g + pltpu.roll(a2 * g, S5W_TILE // 2, 1)

        lax.fori_loop(0, steps - 1, step, dh_ref[last:last + S5_ROWS, :])
        hv, dev = h_ref[...], de_ref[...]
        da1 = jnp.sum(hv * dev, axis=0, keepdims=True)
        da2 = jnp.sum(pltpu.roll(hv, S5W_TILE // 2, 1) * dev, axis=0, keepdims=True)
        dl_ref[...] = jnp.concatenate([da1, da2, jnp.zeros((SUBLANES - 2, S5W_TILE), F32)], axis=0)

    return pl.pallas_call(
        body, name="s5_carry_bwd", grid=(S5W_SW // S5W_TILE,), in_specs=[wide, wide, lam], out_specs=[wide, lam],
        out_shape=[jax.ShapeDtypeStruct(dh.shape, F32), jax.ShapeDtypeStruct(lam_rows.shape, F32)],
        compiler_params=pltpu.CompilerParams(dimension_semantics=("parallel",), vmem_limit_bytes=VMEM_LIMIT),
    )(dh, hin, lam_rows)


@jax.custom_vjp
def _s5_scan(s, tm, mm, nm, lam_rows):
    hin = _s5_carry_fwd(_mm(s, mm), lam_rows)
    return _mm(hin, nm, init=_mm(s, tm))


def _s5_scan_fwd(s, tm, mm, nm, lam_rows):
    hin = _s5_carry_fwd(_mm(s, mm), lam_rows)
    return _mm(hin, nm, init=_mm(s, tm)), (s, tm, mm, nm, lam_rows, hin)


def _s5_scan_bwd(res, dy):
    s, tm, mm, nm, lam_rows, hin = res
    de, dlam = _s5_carry_bwd(_mm(dy, nm, 'nt'), hin, lam_rows)
    ds = _mm(de, mm, 'nt', init=_mm(dy, tm, 'nt'))
    return ds, _mm(s, dy, 'tn'), _mm(s, de, 'tn'), _mm(hin, dy, 'tn'), dlam


_s5_scan.defvjp(_s5_scan_fwd, _s5_scan_bwd)


def _s5_wide_group(u, u_c, lam_re, lam_im, log_dt, b_re, b_im, c_re, c_im, d_skip, glu_w, glu_b):
    bn, t_lat, _ = u.shape
    t_ctx = u_c.shape[1]
    n = t_ctx + t_lat
    steps = n // S5W_LC
    assert bn <= S5_ROWS and t_ctx % S5W_LC == 0 and t_lat % S5W_LC == 0
    seqs = [jnp.concatenate([u_c, u], axis=1), jnp.concatenate([jnp.flip(u_c, 1), jnp.flip(u, 1)], axis=1)]
    ys = []
    for d, seq in enumerate(seqs):
        ops = _s5_wide_operators(*[t[d:d + 1] for t in (lam_re, lam_im, log_dt, b_re, b_im, c_re, c_im)])
        s = jnp.pad(seq.reshape(bn, steps, S5W_RW).transpose(1, 0, 2), ((0, 0), (0, S5_ROWS - bn), (0, 0)))
        y = _s5_scan(s.reshape(steps * S5_ROWS, S5W_RW), *[op[0] for op in ops])
        ys.append(y.reshape(steps, S5_ROWS, S5W_RW)[:, :bn].transpose(1, 0, 2).reshape(bn, n, S5_W))

    def glu(yy):
        zz = jax.nn.gelu(yy)
        return zz * jax.nn.sigmoid(zz @ glu_w + glu_b)

    yl = d_skip * u + ys[0][:, t_ctx:] + jnp.flip(ys[1][:, t_ctx:], 1)
    yc = d_skip * u_c + ys[0][:, :t_ctx] + jnp.flip(ys[1][:, :t_ctx], 1)
    return glu(yl), glu(yc)


def _heads(proj, proj_c, hp, cos, sin, with_ctx_out):
    g_qkv, g_z, g_b, g_a, a_q, a_k, a_v, s_u = jnp.split(proj[..., :IN_COLS], list(IN_CUTS), axis=-1)
    c_qkv, c_z, c_b, c_a, c_q, c_k, c_v, c_u = jnp.split(proj_c[..., :IN_COLS], list(IN_CUTS), axis=-1)
    o_gdn, oc_gdn = _gdn_group(g_qkv, g_z, g_b, g_a, c_qkv, c_z, c_b, c_a, hp['gdn_conv_w'], hp['gdn_a_log'],
                               hp['gdn_dt_bias'], hp['gdn_norm_w'], with_ctx_out)
    o_att, oc_att = _attention_group(a_q, a_k, a_v, c_q, c_k, c_v, hp['q_norm_w'], hp['k_norm_w'], cos, sin,
                                     with_ctx_out)
    o_s5, oc_s5 = _s5_wide_group(s_u, c_u, hp['s5_lam_re'], hp['s5_lam_im'], hp['s5_log_dt'], hp['s5_b_re'],
                                 hp['s5_b_im'], hp['s5_c_re'], hp['s5_c_im'], hp['s5_d'], hp['glu_w'], hp['glu_b'])
    o = jnp.concatenate([o_gdn, o_att, o_s5], axis=-1)
    if not with_ctx_out:
        return (o,)
    return o, jnp.concatenate([oc_gdn, oc_att, oc_s5], axis=-1)


def _mixer_fwd(st, x, mod, t_ctx, w_in, w_out, lg, lb, hp, cos, sin, alpha):
    rows = st.bn * st.n
    h = _modulate(st, x, mod, 1).reshape(rows, st.d)
    proj = _mm(h, w_in).reshape(st.bn, st.n, IN_COLS_PAD)
    (o_lat, o_ctx), heads_vjp = jax.vjp(lambda p, pc, hp_: _heads(p, pc, hp_, cos, sin, True),
                                         proj[:, t_ctx:], proj[:, :t_ctx], hp)
    o = jnp.concatenate([o_ctx, o_lat], axis=1).reshape(rows, st.d)
    y = _mm(o, w_out).reshape(st.bn, st.n, st.d)
    return _post_norm(st, x, y, mod, 1, lg, lb, 1.0, alpha), (h, heads_vjp, o, y)


def _mixer_bwd(st, x, mod, t_ctx, w_in, w_out, lg, alpha, kept, dout):
    h, heads_vjp, o, y = kept
    rows = st.bn * st.n
    dxr, dy, dgate, dlg, dlb = _post_norm_bwd(st, x, y, dout, mod, 1, lg, 1.0, alpha)
    dy = dy.reshape(rows, st.d)
    dw_out = _mm(o, dy, 'tn')
    do = _mm(dy, w_out, 'nt').reshape(st.bn, st.n, st.d)
    dproj, dproj_c, dhp = heads_vjp((do[:, t_ctx:], do[:, :t_ctx]))
    dp = jnp.concatenate([dproj_c, dproj], axis=1).reshape(rows, IN_COLS_PAD)
    dw_in = _mm(h, dp, 'tn')
    dh = _mm(dp, w_in, 'nt').reshape(st.bn, st.n, st.d)
    dx, dsh, dsc = _modulate_bwd(st, dh, x, dxr, mod, 1)
    return dx, (dsh, dsc, dgate), dw_in, dw_out, jnp.sum(dlg, axis=(0, 1)), jnp.sum(dlb, axis=(0, 1)), dhp


def _natural(name, gathered):
    ax = SHARD_AXIS[name]
    t = jnp.moveaxis(gathered, 0, ax)
    return t.reshape(t.shape[:ax] + (t.shape[ax] * t.shape[ax + 1],) + t.shape[ax + 2:])


def _to_shards(name, full):
    ax = SHARD_AXIS[name]
    t = full.reshape(full.shape[:ax] + (N_CHIPS, full.shape[ax] // N_CHIPS) + full.shape[ax + 1:])
    return jnp.moveaxis(t, ax, 0)


def _pack(arrays):
    flat = jnp.concatenate([a.reshape(-1) for a in arrays])
    pad = (-flat.size) % (SUBLANES * LANES)
    return jnp.pad(flat, (0, pad)).reshape(-1, LANES)


def _unpack(packed, like):
    flat = packed.reshape(-1)
    out, at = [], 0
    for a in like:
        out.append(flat[at:at + a.size].reshape(a.shape))
        at += a.size
    return out


def kernel(x, c, ctx, c_ctx, w_ada, b_ada, ln_g, ln_b, ffn_w1, ffn_w3, ffn_w2, w_in, w_out, gdn_conv_w, gdn_a_log, gdn_dt_bias, gdn_norm_w, q_norm_w, k_norm_w, s5_lam_re, s5_lam_im, s5_log_dt, s5_b_re, s5_b_im, s5_c_re, s5_c_im, s5_d, glu_w, glu_b, loss_target, m_c_ctx, m_w_ada, m_b_ada, m_ln_g, m_ln_b, m_ffn_w1, m_ffn_w3, m_ffn_w2, m_w_in, m_w_out, m_gdn_conv_w, m_gdn_a_log, m_gdn_dt_bias, m_gdn_norm_w, m_q_norm_w, m_k_norm_w, m_s5_lam_re, m_s5_lam_im, m_s5_log_dt, m_s5_b_re, m_s5_b_im, m_s5_c_re, m_s5_c_im, m_s5_d, m_glu_w, m_glu_b, v_c_ctx, v_w_ada, v_b_ada, v_ln_g, v_ln_b, v_ffn_w1, v_ffn_w3, v_ffn_w2, v_w_in, v_w_out, v_gdn_conv_w, v_gdn_a_log, v_gdn_dt_bias, v_gdn_norm_w, v_q_norm_w, v_k_norm_w, v_s5_lam_re, v_s5_lam_im, v_s5_log_dt, v_s5_b_re, v_s5_b_im, v_s5_c_re, v_s5_c_im, v_s5_d, v_glu_w, v_glu_b):
    given = dict(locals())
    w = {n: given[n] for n in WEIGHTS}
    mom = {n: given['m_' + n] for n in WEIGHTS}
    var = {n: given['v_' + n] for n in WEIGHTS}
    depth = w_ada.shape[0]
    bn, t_lat, d = x.shape
    alpha = (2.0 * depth) ** 0.25
    chip = 2 * lax.axis_index("x") + lax.axis_index("y")

    gathered = _gather_weight_shards([w[n].astype(BF16) for n in BIG])
    full = {n: _natural(n, g) for n, g in zip(BIG, gathered)}
    full['w_in'] = jnp.pad(full['w_in'], ((0, 0), (0, 0), (0, IN_COLS_PAD - IN_COLS)))
    small_sh = [w[n] for n in SMALL_SHARDED]
    small_all = _all_gather8(_pack(small_sh))
    for n, parts in zip(SMALL_SHARDED, zip(*[_unpack(small_all[2 * j], small_sh) for j in range(N_CHIPS)])):
        full[n] = _natural(n, jnp.stack(parts))
    for n in REPLICATED:
        full[n] = w[n]

    cos, sin = _axial_rope_tables(t_lat // GRID_W)
    act = jnp.zeros((ADA_ROWS, d), F32).at[:bn].set(jax.nn.silu(c)).at[bn].set(jax.nn.silu(c_ctx))
    t_ctx = ctx.shape[1]
    st = _Stream(bn, t_ctx, t_lat, d)
    xs = jnp.concatenate([ctx, x], axis=1)
    saved = []
    for l in range(depth):
        mod = (_mm(act, full['w_ada'][l]) + full['b_ada'][l]).reshape(ADA_ROWS, N_MOD, d)
        hp = {n: full[n][l] for n in HEAD_PARAMS}
        lg, lb = full['ln_g'][l], full['ln_b'][l]
        f1 = (full['ffn_w1'][l], full['ffn_w3'][l], full['ffn_w2'][l])
        x0 = xs
        x1, kept1 = _ffn_fwd(st, x0, mod, 0, f1[0][0], f1[1][0], f1[2][0], lg[0], lb[0], alpha)
        x2, kept2 = _mixer_fwd(st, x1, mod, t_ctx, full['w_in'][l], full['w_out'][l], lg[1], lb[1], hp, cos, sin, alpha)
        xs, kept3 = _ffn_fwd(st, x2, mod, 2, f1[0][1], f1[1][1], f1[2][1], lg[2], lb[2], alpha)
        saved.append((x0, x1, x2, mod, kept1, kept2, kept3))

    err = xs[:, t_ctx:] - loss_target
    loss = lax.psum(0.5 * jnp.sum(jnp.mean(err * err, axis=-1)), ("x", "y", "c"))
    dxs = jnp.concatenate([jnp.zeros((bn, t_ctx, d), F32), err / d], axis=1)

    grads = {n: [None] * depth for n in WEIGHTS if n != 'c_ctx'}
    dact = jnp.zeros((ADA_ROWS, d), F32)
    for l in reversed(range(depth)):
        x0, x1, x2, mod, kept1, kept2, kept3 = saved[l]
        lg = full['ln_g'][l]
        f1 = (full['ffn_w1'][l], full['ffn_w3'][l], full['ffn_w2'][l])
        dx2, dm3, dw1b, dw3b, dw2b, dlg2, dlb2 = _ffn_bwd(st, x2, mod, 2, f1[0][1], f1[1][1], f1[2][1], lg[2], alpha,
                                                           kept3, dxs)
        dx1, dm2, dw_in, dw_out, dlg1, dlb1, dhp = _mixer_bwd(st, x1, mod, t_ctx, full['w_in'][l], full['w_out'][l],
                                                              lg[1], alpha, kept2, dx2)
        dxs, dm1, dw1a, dw3a, dw2a, dlg0, dlb0 = _ffn_bwd(st, x0, mod, 0, f1[0][0], f1[1][0], f1[2][0], lg[0], alpha,
                                                          kept1, dx1)
        parts = jnp.stack([t[:, :, 0] for grp in (dm1, dm2, dm3) for t in grp], axis=2)
        dmod = jnp.concatenate([parts[:, 1], jnp.sum(parts[:, 0], axis=0, keepdims=True),
                                jnp.zeros((ADA_ROWS - bn - 1, N_MOD, d), F32)], axis=0).reshape(ADA_ROWS, N_MOD * d)
        grads['w_ada'][l] = _mm(act, dmod, 'tn')
        grads['b_ada'][l] = jnp.sum(dmod, axis=0)
        dact = dact + _mm(dmod, full['w_ada'][l], 'nt')
        grads['ffn_w1'][l] = jnp.stack([dw1a, dw1b])
        grads['ffn_w3'][l] = jnp.stack([dw3a, dw3b])
        grads['ffn_w2'][l] = jnp.stack([dw2a, dw2b])
        grads['w_in'][l] = dw_in[:, :IN_COLS]
        grads['w_out'][l] = dw_out
        grads['ln_g'][l] = jnp.stack([dlg0, dlg1, dlg2])
        grads['ln_b'][l] = jnp.stack([dlb0, dlb1, dlb2])
        for n in HEAD_PARAMS:
            grads[n][l] = dhp[n]
    grad = {n: jnp.stack(g) for n, g in grads.items()}
    sig = jax.nn.sigmoid(c_ctx)
    grad['c_ctx'] = dact[bn] * (sig * (1.0 + c_ctx * (1.0 - sig)))

    half = depth // 2
    laid = []
    for n in BIG:
        s = _to_shards(n, grad[n])
        s = s.reshape((N_CHIPS, 2, half) + s.shape[2:])
        laid.append(jnp.moveaxis(s, 1, 0))
    theirs = _swap_halves(laid)
    pair = [_add_sibling(g, r) for g, r in zip(laid, theirs)]
    landed = _scatter_partials([lo for _, lo in pair])
    mine = [_add_chips(p, r) for (p, _), r in zip(pair, landed)]
    reduced = dict(zip(BIG, _share_halves(mine)))

    small_names = REPLICATED + SMALL_SHARDED
    small_grads = [grad[n] for n in small_names]
    summed = _unpack(_sum_devices(_all_gather8(_pack(small_grads))), small_grads)
    for n, g in zip(small_names, summed):
        if n in SMALL_SHARDED:
            ax = SHARD_AXIS[n]
            width = g.shape[ax] // N_CHIPS
            g = lax.dynamic_slice_in_dim(g, chip * width, width, axis=ax)
        reduced[n] = g

    delta, new_m, new_v = {}, {}, {}
    for n in BIG:
        delta[n], new_m[n], new_v[n] = _adamw(w[n], reduced[n], mom[n], var[n])
    packs = [_pack([t[n] for n in small_names]) for t in (w, reduced, mom, var)]
    like = [w[n] for n in small_names]
    for res, packed in zip((delta, new_m, new_v), _adamw(*packs)):
        res.update(zip(small_names, _unpack(packed, like)))

    return (loss, dxs[:, t_ctx:], *[reduced[n] for n in WEIGHTS], *[delta[n] for n in WEIGHTS],
            *[new_m[n] for n in WEIGHTS], *[new_v[n] for n in WEIGHTS])
```
